```python
import math
import jax, jax.numpy as jnp
from jax import lax
import numpy as np


D_MODEL = 1024
BATCH = 4
SEQ = 8192
DEPTH = 4

N_MIXERS = 2
N_A = (DEPTH + 1) // 2
N_B = DEPTH // 2
A_HEADS = 16
A_QK_DIM = 64
A_V_DIM = 64
A_KV_RANK = 256
IDX_HEADS = 8
IDX_DIM = 64
TOPK_MAX = 256
A_SPLITS = [A_HEADS * A_QK_DIM,
            A_HEADS * A_QK_DIM + A_KV_RANK,
            A_HEADS * A_QK_DIM + A_KV_RANK + IDX_HEADS * IDX_DIM,
            A_HEADS * A_QK_DIM + A_KV_RANK + IDX_HEADS * IDX_DIM + IDX_DIM]
A_IN = A_SPLITS[-1] + IDX_HEADS
B_HEADS = 8
B_HEAD_DIM = 64
B_QK = 2 * B_HEADS * B_HEAD_DIM
B_IN = 3 * B_QK
REL_BUCKETS = 32
REL_MAX_DIST = 128
BIAS_HEADS = A_HEADS
D_FF = 4 * D_MODEL
Q_BLOCK = 128
LN_EPS = 1e-5
NEG = -1e30
DN_ALPHA = (2 * DEPTH) ** 0.25
DN_BETA = (8 * DEPTH) ** -0.25

kernel_name = "hybrid_dsa_diffattn_deepnorm_adaln"


def _layer_norm(x, g, b):
    xf = x.astype(jnp.float32)
    mu = jnp.mean(xf, -1, keepdims=True)
    var = jnp.mean(jnp.square(xf - mu), -1, keepdims=True)
    return ((xf - mu) * lax.rsqrt(var + LN_EPS)).astype(x.dtype) * g + b


def _rms_norm(x, g):
    xf = x.astype(jnp.float32)
    return (xf * lax.rsqrt(jnp.mean(xf * xf, -1, keepdims=True) + LN_EPS)).astype(x.dtype) * g


def _rel_bucket(dist):
    n = jnp.maximum(dist, 0)
    max_exact = REL_BUCKETS // 2
    nf = jnp.maximum(n, 1).astype(jnp.float32)
    large = max_exact + (jnp.log(nf / max_exact) / math.log(REL_MAX_DIST / max_exact)
                         * (REL_BUCKETS - max_exact)).astype(jnp.int32)
    large = jnp.minimum(large, REL_BUCKETS - 1)
    return jnp.where(n < max_exact, n, large)


def _to_blocks(a):
    B, S = a.shape[:2]
    return jnp.moveaxis(a.reshape(B, S // Q_BLOCK, Q_BLOCK, *a.shape[2:]), 1, 0)


def _from_blocks(a):
    nb, B, Q = a.shape[:3]
    return jnp.moveaxis(a, 0, 1).reshape(B, nb * Q, *a.shape[3:])


def _dsa_mixer(h, w_in, kv_norm, w_uk, w_uv, w_o, rel_bias):
    B, S, _ = h.shape
    topk = min(TOPK_MAX, S // 4)
    proj = h @ w_in
    q, ckv, iq, ik, iw = jnp.split(proj, A_SPLITS, axis=-1)
    q = q.reshape(B, S, A_HEADS, A_QK_DIM)
    ckv = _rms_norm(ckv, kv_norm)
    iq = iq.reshape(B, S, IDX_HEADS, IDX_DIM)
    iw = iw * (IDX_HEADS ** -0.5 * IDX_DIM ** -0.5)
    pos = jnp.arange(S, dtype=jnp.int32)

    def block(args):
        qb, iqb, iwb, tpos = args
        idx_logits = jnp.einsum('bqhd,bsd->bhqs', iqb, ik).astype(jnp.float32)
        score = jnp.einsum('bhqs,bqh->bqs', jax.nn.relu(idx_logits), iwb.astype(jnp.float32))
        causal = pos[None, :] <= tpos[:, None]
        score = jnp.where(causal[None], score, -jnp.inf)
        _, sel = lax.top_k(score, topk)
        valid = sel <= tpos[None, :, None]
        kv_sel = jax.vmap(lambda cb, ib: cb[ib])(ckv, sel)
        q_lat = jnp.einsum('bqhd,hdr->bqhr', qb, w_uk) * (A_QK_DIM ** -0.5)
        logits = jnp.einsum('bqhr,bqkr->bhqk', q_lat, kv_sel).astype(jnp.float32)
        bias = rel_bias[_rel_bucket(tpos[None, :, None] - sel)]
        logits = logits + jnp.transpose(bias, (0, 3, 1, 2))
        logits = jnp.where(valid[:, None], logits, NEG)
        p = jax.nn.softmax(logits, axis=-1).astype(h.dtype)
        o_lat = jnp.einsum('bhqk,bqkr->bqhr', p, kv_sel)
        return jnp.einsum('bqhr,hrd->bqhd', o_lat, w_uv)

    o = lax.map(block, (_to_blocks(q), _to_blocks(iq), _to_blocks(iw),
                        pos.reshape(S // Q_BLOCK, Q_BLOCK)))
    o = _from_blocks(o)
    return o.reshape(B, S, A_HEADS * A_V_DIM) @ w_o


def _diff_mixer(h, w_in, lam, subln_g, w_o, rel_bias, layer_idx):
    B, S, _ = h.shape
    lam_init = 0.8 - 0.6 * math.exp(-0.3 * layer_idx)
    proj = h @ w_in
    q, k, v = jnp.split(proj, 3, axis=-1)
    q = q.reshape(B, S, B_HEADS, 2, B_HEAD_DIM) * (B_HEAD_DIM ** -0.5)
    k = k.reshape(B, S, B_HEADS, 2, B_HEAD_DIM)
    v = v.reshape(B, S, B_HEADS, 2 * B_HEAD_DIM)
    lam_f = lam.astype(jnp.float32)
    lam_full = (jnp.exp(jnp.sum(lam_f[0] * lam_f[1])) - jnp.exp(jnp.sum(lam_f[2] * lam_f[3]))
                + lam_init)
    pos = jnp.arange(S, dtype=jnp.int32)

    def block(args):
        qb, tpos = args
        logits = jnp.einsum('bqhmd,bshmd->bhmqs', qb, k).astype(jnp.float32)
        bias = rel_bias[_rel_bucket(tpos[:, None] - pos[None, :])]
        bias = jnp.transpose(bias.reshape(Q_BLOCK, S, B_HEADS, 2), (2, 3, 0, 1))
        causal = pos[None, :] <= tpos[:, None]
        logits = jnp.where(causal, logits + bias, NEG)
        p = jax.nn.softmax(logits, axis=-1)
        a = p[:, :, 0] - lam_full * p[:, :, 1]
        return jnp.einsum('bhqs,bshe->bqhe', a.astype(h.dtype), v)

    o = _from_blocks(lax.map(block, (_to_blocks(q), pos.reshape(S // Q_BLOCK, Q_BLOCK))))
    o = _rms_norm(o, subln_g) * (1.0 - lam_init)
    return o.reshape(B, S, B_HEADS * 2 * B_HEAD_DIM) @ w_o


def _sqrelu_mlp(h, w1, w2):
    return jnp.square(jax.nn.relu(h @ w1)) @ w2


def setup_inputs(seed: int = 0) -> dict:
    key = jax.random.key(seed)
    ks = jax.random.split(key, 20)
    nrm = lambda k, shape, s: jax.random.normal(k, shape, jnp.float32) * s
    D = D_MODEL
    b_w_in = nrm(ks[10], (N_B, D, B_IN), D ** -0.5)
    b_w_in = b_w_in.at[..., 2 * B_QK:].multiply(DN_BETA)
    return {
        'x': nrm(ks[0], (BATCH, SEQ, D), 1.0),
        'c': nrm(ks[1], (BATCH, D), 1.0),
        'rel_bias': nrm(ks[2], (REL_BUCKETS, BIAS_HEADS), 0.5),
        'ada_w': nrm(ks[3], (DEPTH, D, 6 * D), 0.1 * D ** -0.5),
        'ada_b': nrm(ks[4], (DEPTH, 6 * D), 0.01),
        'ln_g': 1.0 + nrm(ks[5], (DEPTH, 2, D), 0.02),
        'ln_b': nrm(ks[6], (DEPTH, 2, D), 0.02),
        'a_w_in': nrm(ks[7], (N_A, D, A_IN), D ** -0.5),
        'a_kv_norm': 1.0 + nrm(ks[8], (N_A, A_KV_RANK), 0.02),
        'a_w_uk': nrm(ks[9], (N_A, A_HEADS, A_QK_DIM, A_KV_RANK), A_KV_RANK ** -0.5),
        'a_w_uv': nrm(ks[11], (N_A, A_HEADS, A_KV_RANK, A_V_DIM), DN_BETA * A_KV_RANK ** -0.5),
        'a_w_o': nrm(ks[12], (N_A, A_HEADS * A_V_DIM, D), DN_BETA * (A_HEADS * A_V_DIM) ** -0.5),
        'b_w_in': b_w_in,
        'b_lambda': nrm(ks[13], (N_B, 4, B_HEAD_DIM), 0.1),
        'b_subln': 1.0 + nrm(ks[14], (N_B, 2 * B_HEAD_DIM), 0.02),
        'b_w_o': nrm(ks[15], (N_B, B_QK, D), DN_BETA * B_QK ** -0.5),
        'mlp_w1': nrm(ks[16], (DEPTH, D, D_FF), D ** -0.5),
        'mlp_w2': nrm(ks[17], (DEPTH, D_FF, D), DN_BETA * D_FF ** -0.5),
    }


def reference(x, c, rel_bias, ada_w, ada_b, ln_g, ln_b,
              a_w_in, a_kv_norm, a_w_uk, a_w_uv, a_w_o,
              b_w_in, b_lambda, b_subln, b_w_o,
              mlp_w1, mlp_w2):
    mod = jnp.einsum('bd,lde->lbe', jax.nn.silu(c), ada_w) + ada_b[:, None]
    for i in range(DEPTH):
        sh_t, sc_t, g_t, sh_c, sc_c, g_c = [m[:, None] for m in jnp.split(mod[i], 6, axis=-1)]
        h = x * (1.0 + sc_t) + sh_t
        j = i // N_MIXERS
        if i % N_MIXERS == 0:
            y = _dsa_mixer(h, a_w_in[j], a_kv_norm[j], a_w_uk[j], a_w_uv[j], a_w_o[j], rel_bias)
        else:
            y = _diff_mixer(h, b_w_in[j], b_lambda[j], b_subln[j], b_w_o[j], rel_bias, i)
        x = _layer_norm(DN_ALPHA * x + (1.0 + g_t) * y, ln_g[i, 0], ln_b[i, 0])
        h = x * (1.0 + sc_c) + sh_c
        y = _sqrelu_mlp(h, mlp_w1[i], mlp_w2[i])
        x = _layer_norm(DN_ALPHA * x + (1.0 + g_c) * y, ln_g[i, 1], ln_b[i, 1])
    return x
```

```python
import functools
import math

import jax
import jax.numpy as jnp
from jax import lax
from jax.experimental import pallas as pl
from jax.experimental.pallas import tpu as pltpu

N_MIXERS = 2
A_HEADS = 16
A_QK_DIM = 64
A_V_DIM = 64
A_KV_RANK = 256
IDX_HEADS = 8
IDX_DIM = 64
TOPK_MAX = 256
B_HEADS = 8
B_HEAD_DIM = 64
B_MAPS = 2 * B_HEADS
REL_BUCKETS = 32
REL_MAX_DIST = 128
LN_EPS = 1e-5
NEG = -1e30
M_INIT = 0.5 * NEG

LANE = 128
VMEM_LIMIT = 56 * 1024 * 1024

MXU_DTYPE = jnp.bfloat16


def _dot(a, b):
    return jnp.dot(a, b, preferred_element_type=jnp.float32)


def _cparams(*sem):
    return pltpu.CompilerParams(dimension_semantics=sem, vmem_limit_bytes=VMEM_LIMIT)


def _const_spec(shape):
    nd = len(shape)
    return pl.BlockSpec(shape, lambda *_: (0,) * nd)


def _layer_norm_rows(z, g, b):
    mu = jnp.mean(z, axis=-1, keepdims=True)
    zc = z - mu
    var = jnp.mean(zc * zc, axis=-1, keepdims=True)
    return zc * lax.rsqrt(var + LN_EPS) * g + b


def _ada_kernel(ct_ref, w_ref, b_ref, o_ref, *, nb):
    ct = ct_ref[...]
    s = ct * jax.nn.sigmoid(ct)
    w = w_ref[0]
    rows = [jnp.sum(w * s[:, b:b + 1], axis=0, keepdims=True) for b in range(nb)]
    o_ref[0] = jnp.concatenate(rows, axis=0) + b_ref[0]


def _ada_mod(c, ada_w, ada_b):
    depth, d, n = ada_w.shape
    nb = c.shape[0]
    tn = 768
    out = pl.pallas_call(
        functools.partial(_ada_kernel, nb=nb),
        grid=(depth, n // tn),
        in_specs=[
            pl.BlockSpec((d, nb), lambda l, j: (0, 0)),
            pl.BlockSpec((1, d, tn), lambda l, j: (l, 0, j)),
            pl.BlockSpec((1, 1, tn), lambda l, j: (l, 0, j)),
        ],
        out_specs=pl.BlockSpec((1, nb, tn), lambda l, j: (l, 0, j)),
        out_shape=jax.ShapeDtypeStruct((depth, nb, n), jnp.float32),
        compiler_params=_cparams("arbitrary", "arbitrary"),
        name="ada_mod",
    )(c.T, ada_w, ada_b.reshape(depth, 1, n))
    return out.reshape(depth, nb, 6, 1, d)


def _mod_spec(layer, k, d):
    return pl.BlockSpec((None, None, None, 1, d), lambda b, *_: (layer, b, k, 0, 0))


def _rel_bucket(dist):
    n = jnp.maximum(dist, 0)
    max_exact = REL_BUCKETS // 2
    nf = jnp.maximum(n, 1).astype(jnp.float32)
    large = max_exact + (jnp.log(nf / max_exact) / math.log(REL_MAX_DIST / max_exact)
                         * (REL_BUCKETS - max_exact)).astype(jnp.int32)
    large = jnp.minimum(large, REL_BUCKETS - 1)
    return jnp.where(n < max_exact, n, large)


def _bias_tile_kernel(tab_ref, bk_ref, o_ref, *, heads):
    bk = bk_ref[0]
    for h in range(heads):
        far = tab_ref[REL_BUCKETS - 1, h]
        acc = jnp.zeros(bk.shape, jnp.float32)
        for b in range(REL_BUCKETS - 1):
            acc = jnp.where(bk == b, tab_ref[b, h] - far, acc)
        o_ref[0, h] = jnp.where(bk < 0, NEG, acc)


def _bias_tiles(rel_bias, tq):
    heads = rel_bias.shape[1]
    r_sub = tq // LANE
    rels = jnp.arange(-2, r_sub + 1, dtype=jnp.int32)
    ql = jnp.arange(tq, dtype=jnp.int32)[None, :, None]
    sl = jnp.arange(LANE, dtype=jnp.int32)[None, None, :]
    dist = ql - rels[:, None, None] * LANE - sl
    bk = jnp.where(dist < 0, -1, _rel_bucket(dist))
    nk = r_sub + 3
    return pl.pallas_call(
        functools.partial(_bias_tile_kernel, heads=heads),
        grid=(nk,),
        in_specs=[
            pl.BlockSpec(memory_space=pltpu.SMEM),
            pl.BlockSpec((1, tq, LANE), lambda k: (k, 0, 0)),
        ],
        out_specs=pl.BlockSpec((1, heads, tq, LANE), lambda k: (k, 0, 0, 0)),
        out_shape=jax.ShapeDtypeStruct((nk, heads, tq, LANE), jnp.float32),
        compiler_params=_cparams("arbitrary"),
        name="bias_tiles",
    )(rel_bias, bk)


def _bias_index(i, j, c, tq, tk):
    rel = j * (tk // LANE) + c - i * (tq // LANE)
    return jnp.clip(rel + 2, 0, tq // LANE + 2)


def _causal_pairs(s, tq, tk):
    ii, jj = [], []
    for i in range(s // tq):
        for j in range((i * tq + tq - 1) // tk + 1):
            ii.append(i)
            jj.append(j)
    return jnp.asarray(ii, jnp.int32), jnp.asarray(jj, jnp.int32)


def _proj_dsa_kernel(x_ref, sc_ref, sh_ref, wq_ref, wc_ref, wi_ref, wk_ref, kvg_ref,
                     q_ref, ckv_ref, iq_ref, ikw_ref):
    h = (x_ref[0] * (1.0 + sc_ref[...]) + sh_ref[...]).astype(MXU_DTYPE)
    q_ref[0] = _dot(h, wq_ref[...]).astype(q_ref.dtype)
    ckv = _dot(h, wc_ref[...])
    ckv = ckv * lax.rsqrt(jnp.mean(ckv * ckv, axis=-1, keepdims=True) + LN_EPS) * kvg_ref[...]
    ckv_ref[0] = ckv.astype(ckv_ref.dtype)
    iq_ref[0] = _dot(h, wi_ref[...]).astype(iq_ref.dtype)
    ikw_ref[0] = _dot(h, wk_ref[...])


def _proj_dsa(x, mod, layer, w_in, kv_norm):
    b, s, d = x.shape
    tm = 512
    nq = A_HEADS * A_QK_DIM
    ni = IDX_HEADS * IDX_DIM
    o1, o2, o3 = nq, nq + A_KV_RANK, nq + A_KV_RANK + ni
    wq = w_in[:, :o1].astype(MXU_DTYPE)
    wc = w_in[:, o1:o2].astype(MXU_DTYPE)
    wi = w_in[:, o2:o3].astype(MXU_DTYPE)
    wk = jnp.pad(w_in[:, o3:], ((0, 0), (0, LANE - (w_in.shape[1] - o3)))).astype(MXU_DTYPE)
    row = lambda bb, t: (bb, t, 0)
    return pl.pallas_call(
        _proj_dsa_kernel,
        grid=(b, s // tm),
        in_specs=[
            pl.BlockSpec((1, tm, d), row),
            _mod_spec(layer, 1, d), _mod_spec(layer, 0, d),
            _const_spec((d, nq)), _const_spec((d, A_KV_RANK)), _const_spec((d, ni)),
            _const_spec((d, LANE)), _const_spec((1, A_KV_RANK)),
        ],
        out_specs=[
            pl.BlockSpec((1, tm, nq), row), pl.BlockSpec((1, tm, A_KV_RANK), row),
            pl.BlockSpec((1, tm, ni), row), pl.BlockSpec((1, tm, LANE), row),
        ],
        out_shape=[
            jax.ShapeDtypeStruct((b, s, nq), MXU_DTYPE),
            jax.ShapeDtypeStruct((b, s, A_KV_RANK), MXU_DTYPE),
            jax.ShapeDtypeStruct((b, s, ni), MXU_DTYPE),
            jax.ShapeDtypeStruct((b, s, LANE), jnp.float32),
        ],
        compiler_params=_cparams("parallel", "parallel"),
        name="proj_dsa",
    )(x, mod, mod, wq, wc, wi, wk, kv_norm.reshape(1, A_KV_RANK))


def _indexer_kernel(iq_ref, ikt_ref, ikw_ref, mask_ref, keys_ref, p_ref, *, tq, ck, topk, s_len):
    i = pl.program_id(1)
    q0 = i * tq
    nkc = (q0 + tq + ck - 1) // ck
    iq = iq_ref[0]
    iqs = jnp.concatenate([iq[:, h * IDX_DIM:(h + 1) * IDX_DIM] for h in range(IDX_HEADS)], axis=0)
    w = ikw_ref[0][:, IDX_DIM:IDX_DIM + IDX_HEADS] * (IDX_HEADS ** -0.5 * IDX_DIM ** -0.5)
    tpos = q0 + lax.broadcasted_iota(jnp.int32, (tq, 1), 0)
    lane_pos = lax.broadcasted_iota(jnp.int32, (1, ck), 1)

    def score_chunk(kc, carry):
        k0 = pl.multiple_of(kc * ck, ck)
        lg = _dot(iqs, ikt_ref[0, :, pl.ds(k0, ck)])
        sc = jnp.zeros((tq, ck), jnp.float32)
        for h in range(IDX_HEADS):
            sc = sc + jnp.maximum(lg[h * tq:(h + 1) * tq], 0.0) * w[:, h:h + 1]
        sc = jnp.where(k0 + lane_pos <= tpos, sc, -jnp.inf)
        bits = pltpu.bitcast(sc, jnp.int32)
        keys_ref[:, pl.ds(k0, ck)] = bits ^ ((bits >> 31) & 0x7FFFFFFF)
        return carry

    lax.fori_loop(0, nkc, score_chunk, 0)

    def count(pred):
        def body(kc, acc):
            k0 = pl.multiple_of(kc * ck, ck)
            ind = pred(keys_ref[:, pl.ds(k0, ck)], k0 + lane_pos).astype(jnp.int32)
            for c in range(ck // LANE):
                acc = acc + ind[:, c * LANE:(c + 1) * LANE]
            return acc
        acc = lax.fori_loop(0, nkc, body, jnp.zeros((tq, LANE), jnp.int32))
        return jnp.sum(acc, axis=1, keepdims=True)

    k_eff = jnp.minimum(topk, tpos + 1)
    int_min = jnp.int32(-2 ** 31)
    zero = jnp.zeros((tq, 1), jnp.int32)
    prefix = jnp.where(count(lambda k, sp: k >= zero) >= k_eff, zero, int_min)

    def bit_step(it, prefix):
        cand = prefix + jnp.left_shift(jnp.int32(1), 30 - it)
        return jnp.where(count(lambda k, sp: k >= cand) >= k_eff, cand, prefix)

    tau = lax.fori_loop(0, 31, bit_step, prefix)
    cnt_gt = count(lambda k, sp: k > tau)
    cnt_eq = count(lambda k, sp: k == tau)
    need = k_eff - cnt_gt
    overflow = cnt_eq > need
    p_ref[...] = jnp.full((tq, 1), s_len, jnp.int32)

    @pl.when(jnp.max(overflow.astype(jnp.int32)) > 0)
    def _():
        def pos_step(it, pos):
            cand = pos + jnp.left_shift(jnp.int32(1), (s_len.bit_length() - 1) - it)
            below = count(lambda k, sp: (k == tau) & (sp < cand))
            return jnp.where(below < need, cand, pos)
        pos = lax.fori_loop(0, s_len.bit_length(), pos_step, zero)
        p_ref[...] = jnp.where(overflow, pos, s_len)

    last_eq = p_ref[...]

    def write_chunk(kc, carry):
        k0 = pl.multiple_of(kc * ck, ck)

        @pl.when(kc < nkc)
        def _():
            k = keys_ref[:, pl.ds(k0, ck)]
            sel = (k > tau) | ((k == tau) & (k0 + lane_pos <= last_eq))
            mask_ref[0, :, pl.ds(k0, ck)] = jnp.where(sel, 0.0, NEG).astype(mask_ref.dtype)

        @pl.when(kc >= nkc)
        def _():
            mask_ref[0, :, pl.ds(k0, ck)] = jnp.full((tq, ck), NEG, mask_ref.dtype)
        return carry

    lax.fori_loop(0, s_len // ck, write_chunk, 0)


def _indexer_mask(iq, ikt, ikw, topk):
    b, s, ni = iq.shape
    tq, ck = 128, 512
    return pl.pallas_call(
        functools.partial(_indexer_kernel, tq=tq, ck=ck, topk=topk, s_len=s),
        grid=(b, s // tq),
        in_specs=[
            pl.BlockSpec((1, tq, ni), lambda bb, i: (bb, i, 0)),
            pl.BlockSpec((1, IDX_DIM, s), lambda bb, i: (bb, 0, 0)),
            pl.BlockSpec((1, tq, LANE), lambda bb, i: (bb, i, 0)),
        ],
        out_specs=pl.BlockSpec((1, tq, s), lambda bb, i: (bb, i, 0)),
        out_shape=jax.ShapeDtypeStruct((b, s, s), jnp.bfloat16),
        scratch_shapes=[pltpu.VMEM((tq, s), jnp.int32), pltpu.VMEM((tq, 1), jnp.int32)],
        compiler_params=_cparams("parallel", "parallel"),
        name="indexer_topk",
    )(iq, ikt, ikw)


def _online_softmax(s, m_ref, l_ref, rows):
    m_old = m_ref[rows, :]
    m_new = jnp.maximum(m_old, jnp.max(s, axis=-1, keepdims=True))
    alpha = jnp.exp(m_old - m_new)
    p = jnp.exp(s - m_new)
    l_ref[rows, :] = alpha * l_ref[rows, :] + jnp.sum(p, axis=-1, keepdims=True)
    m_ref[rows, :] = m_new
    return p, alpha


def _dsa_attn_kernel(itab, jtab, q_ref, wuk_ref, ckvt_ref, ckv_ref, mask_ref, nb_ref, wuv_ref,
                     o_ref, qlat_ref, m_ref, l_ref, acc_ref, *, tq, tk, hc):
    pidx = pl.program_id(1)
    i = itab[pidx]
    j = jtab[pidx]
    jlast = (i * tq + tq - 1) // tk
    heads = A_HEADS

    @pl.when(j == 0)
    def _():
        for h in range(heads):
            qh = q_ref[0, :, h * A_QK_DIM:(h + 1) * A_QK_DIM]
            ql = _dot(qh, wuk_ref[h]) * (A_QK_DIM ** -0.5)
            qlat_ref[h * tq:(h + 1) * tq, :] = ql.astype(qlat_ref.dtype)
        m_ref[...] = jnp.full(m_ref.shape, M_INIT, jnp.float32)
        l_ref[...] = jnp.zeros(l_ref.shape, jnp.float32)
        acc_ref[...] = jnp.zeros(acc_ref.shape, jnp.float32)

    ckvt = ckvt_ref[0]
    ckv = ckv_ref[0]
    maskf = mask_ref[0].astype(jnp.float32)
    bidx = [_bias_index(i, j, c, tq, tk) for c in range(tk // LANE)]

    def chunk(cidx, carry):
        r0 = pl.multiple_of(cidx * (hc * tq), hc * tq)
        s_all = _dot(qlat_ref[pl.ds(r0, hc * tq), :], ckvt)
        ps, alphas = [], []
        for hh in range(hc):
            h = cidx * hc + hh
            bias = jnp.concatenate([nb_ref[bidx[c], h] for c in range(tk // LANE)], axis=1)
            s = s_all[hh * tq:(hh + 1) * tq] + (maskf + bias)
            rows = pl.ds(pl.multiple_of(r0 + hh * tq, tq), tq)
            p, alpha = _online_softmax(s, m_ref, l_ref, rows)
            ps.append(p.astype(MXU_DTYPE))
            alphas.append(alpha)
        pv = _dot(jnp.concatenate(ps, axis=0), ckv)
        rows = pl.ds(r0, hc * tq)
        acc_ref[rows, :] = jnp.concatenate(alphas, axis=0) * acc_ref[rows, :] + pv
        return carry

    lax.fori_loop(0, heads // hc, chunk, 0)

    @pl.when(j == jlast)
    def _():
        for h in range(heads):
            rows = slice(h * tq, (h + 1) * tq)
            o_lat = acc_ref[rows, :] / l_ref[rows, :]
            oh = _dot(o_lat.astype(MXU_DTYPE), wuv_ref[h])
            o_ref[0, :, h * A_V_DIM:(h + 1) * A_V_DIM] = oh.astype(o_ref.dtype)


def _dsa_attention(q, ckv, ckvt, mask, nb, w_uk, w_uv):
    b, s, nq = q.shape
    tq, tk, hc = 128, 512, 4
    itab, jtab = _causal_pairs(s, tq, tk)
    rows = A_HEADS * tq
    grid_spec = pltpu.PrefetchScalarGridSpec(
        num_scalar_prefetch=2,
        grid=(b, int(itab.shape[0])),
        in_specs=[
            pl.BlockSpec((1, tq, nq), lambda bb, p, it, jt: (bb, it[p], 0)),
            pl.BlockSpec(w_uk.shape, lambda bb, p, it, jt: (0, 0, 0)),
            pl.BlockSpec((1, A_KV_RANK, tk), lambda bb, p, it, jt: (bb, 0, jt[p])),
            pl.BlockSpec((1, tk, A_KV_RANK), lambda bb, p, it, jt: (bb, jt[p], 0)),
            pl.BlockSpec((1, tq, tk), lambda bb, p, it, jt: (bb, it[p], jt[p])),
            pl.BlockSpec(nb.shape, lambda bb, p, it, jt: (0, 0, 0, 0)),
            pl.BlockSpec(w_uv.shape, lambda bb, p, it, jt: (0, 0, 0)),
        ],
        out_specs=pl.BlockSpec((1, tq, A_HEADS * A_V_DIM), lambda bb, p, it, jt: (bb, it[p], 0)),
        scratch_shapes=[
            pltpu.VMEM((rows, A_KV_RANK), MXU_DTYPE),
            pltpu.VMEM((rows, 1), jnp.float32),
            pltpu.VMEM((rows, 1), jnp.float32),
            pltpu.VMEM((rows, A_KV_RANK), jnp.float32),
        ],
    )
    return pl.pallas_call(
        functools.partial(_dsa_attn_kernel, tq=tq, tk=tk, hc=hc),
        grid_spec=grid_spec,
        out_shape=jax.ShapeDtypeStruct((b, s, A_HEADS * A_V_DIM), MXU_DTYPE),
        compiler_params=_cparams("parallel", "arbitrary"),
        name="dsa_attention",
    )(itab, jtab, q, w_uk, ckvt, ckv, mask, nb, w_uv)


def _oproj_ln_kernel(o_ref, x_ref, g_ref, w_ref, lng_ref, lnb_ref, out_ref, *, alpha):
    y = _dot(o_ref[0], w_ref[...])
    z = alpha * x_ref[0] + (1.0 + g_ref[...]) * y
    out_ref[0] = _layer_norm_rows(z, lng_ref[...], lnb_ref[...])


def _oproj_ln(o, x, mod, layer, w_o, ln_g, ln_b, alpha):
    b, s, d = x.shape
    tm = 512
    n_in = o.shape[-1]
    row = lambda bb, t: (bb, t, 0)
    return pl.pallas_call(
        functools.partial(_oproj_ln_kernel, alpha=alpha),
        grid=(b, s // tm),
        in_specs=[
            pl.BlockSpec((1, tm, n_in), row), pl.BlockSpec((1, tm, d), row),
            _mod_spec(layer, 2, d), _const_spec((n_in, d)),
            _const_spec((1, d)), _const_spec((1, d)),
        ],
        out_specs=pl.BlockSpec((1, tm, d), row),
        out_shape=jax.ShapeDtypeStruct((b, s, d), jnp.float32),
        compiler_params=_cparams("parallel", "parallel"),
        name="oproj_ln",
    )(o, x, mod, w_o.astype(MXU_DTYPE), ln_g.reshape(1, d), ln_b.reshape(1, d))


def _proj_diff_kernel(x_ref, sc_ref, sh_ref, wq_ref, wk_ref, wv_ref, q_ref, k_ref, v_ref):
    h = (x_ref[0] * (1.0 + sc_ref[...]) + sh_ref[...]).astype(MXU_DTYPE)
    q_ref[0] = (_dot(h, wq_ref[...]) * (B_HEAD_DIM ** -0.5)).astype(q_ref.dtype)
    k_ref[0] = _dot(h, wk_ref[...]).astype(k_ref.dtype)
    v_ref[0] = _dot(h, wv_ref[...]).astype(v_ref.dtype)


def _proj_diff(x, mod, layer, w_in):
    b, s, d = x.shape
    tm = 512
    n = w_in.shape[1] // 3
    ws = [w_in[:, k * n:(k + 1) * n].astype(MXU_DTYPE) for k in range(3)]
    row = lambda bb, t: (bb, t, 0)
    return pl.pallas_call(
        _proj_diff_kernel,
        grid=(b, s // tm),
        in_specs=[pl.BlockSpec((1, tm, d), row), _mod_spec(layer, 1, d), _mod_spec(layer, 0, d)]
        + [_const_spec((d, n))] * 3,
        out_specs=[pl.BlockSpec((1, tm, n), row)] * 3,
        out_shape=[jax.ShapeDtypeStruct((b, s, n), MXU_DTYPE)] * 3,
        compiler_params=_cparams("parallel", "parallel"),
        name="proj_diff",
    )(x, mod, mod, *ws)


def _diff_attn_kernel(itab, jtab, q_ref, kt_ref, v_ref, nb_ref, lam_ref, g_ref, o_ref,
                      m_ref, l_ref, acc_ref, *, tq, tk, lam_init):
    pidx = pl.program_id(1)
    i = itab[pidx]
    j = jtab[pidx]
    jlast = (i * tq + tq - 1) // tk
    dv = 2 * B_HEAD_DIM

    @pl.when(j == 0)
    def _():
        m_ref[...] = jnp.full(m_ref.shape, M_INIT, jnp.float32)
        l_ref[...] = jnp.zeros(l_ref.shape, jnp.float32)
        acc_ref[...] = jnp.zeros(acc_ref.shape, jnp.float32)

    bidx = [_bias_index(i, j, c, tq, tk) for c in range(tk // LANE)]

    def head(h, carry):
        ps, alphas = [], []
        for mm in range(2):
            col = 2 * h + mm
            s = _dot(q_ref[0, col], kt_ref[0, col])
            s = s + jnp.concatenate([nb_ref[bidx[c], col] for c in range(tk // LANE)], axis=1)
            rows = pl.ds(pl.multiple_of(col * tq, tq), tq)
            p, alpha = _online_softmax(s, m_ref, l_ref, rows)
            ps.append(p.astype(MXU_DTYPE))
            alphas.append(alpha)
        pv = _dot(jnp.concatenate(ps, axis=0), v_ref[0, h])
        rows = pl.ds(pl.multiple_of(2 * h * tq, 2 * tq), 2 * tq)
        acc_ref[rows, :] = jnp.concatenate(alphas, axis=0) * acc_ref[rows, :] + pv
        return carry

    lax.fori_loop(0, B_HEADS, head, 0)

    @pl.when(j == jlast)
    def _():
        lam = lam_ref[...]
        lam_full = (jnp.exp(jnp.sum(lam[0:1] * lam[1:2], axis=-1, keepdims=True))
                    - jnp.exp(jnp.sum(lam[2:3] * lam[3:4], axis=-1, keepdims=True)) + lam_init)
        for h in range(B_HEADS):
            r1 = slice(2 * h * tq, (2 * h + 1) * tq)
            r2 = slice((2 * h + 1) * tq, (2 * h + 2) * tq)
            o = acc_ref[r1, :] / l_ref[r1, :] - lam_full * (acc_ref[r2, :] / l_ref[r2, :])
            o = o * lax.rsqrt(jnp.mean(o * o, axis=-1, keepdims=True) + LN_EPS) * g_ref[...]
            o_ref[0, :, h * dv:(h + 1) * dv] = (o * (1.0 - lam_init)).astype(o_ref.dtype)


def _diff_attention(qm, ktm, vh, nb, lam, subln_g, lam_init):
    b, maps, s, dh = qm.shape
    tq, tk = 256, 512
    dv = 2 * B_HEAD_DIM
    itab, jtab = _causal_pairs(s, tq, tk)
    grid_spec = pltpu.PrefetchScalarGridSpec(
        num_scalar_prefetch=2,
        grid=(b, int(itab.shape[0])),
        in_specs=[
            pl.BlockSpec((1, maps, tq, dh), lambda bb, p, it, jt: (bb, 0, it[p], 0)),
            pl.BlockSpec((1, maps, dh, tk), lambda bb, p, it, jt: (bb, 0, 0, jt[p])),
            pl.BlockSpec((1, B_HEADS, tk, dv), lambda bb, p, it, jt: (bb, 0, jt[p], 0)),
            pl.BlockSpec(nb.shape, lambda bb, p, it, jt: (0, 0, 0, 0)),
            pl.BlockSpec(lam.shape, lambda bb, p, it, jt: (0, 0)),
            pl.BlockSpec((1, dv), lambda bb, p, it, jt: (0, 0)),
        ],
        out_specs=pl.BlockSpec((1, tq, B_HEADS * dv), lambda bb, p, it, jt: (bb, it[p], 0)),
        scratch_shapes=[
            pltpu.VMEM((maps * tq, 1), jnp.float32),
            pltpu.VMEM((maps * tq, 1), jnp.float32),
            pltpu.VMEM((maps * tq, dv), jnp.float32),
        ],
    )
    return pl.pallas_call(
        functools.partial(_diff_attn_kernel, tq=tq, tk=tk, lam_init=lam_init),
        grid_spec=grid_spec,
        out_shape=jax.ShapeDtypeStruct((b, s, B_HEADS * dv), MXU_DTYPE),
        compiler_params=_cparams("parallel", "arbitrary"),
        name="diff_attention",
    )(itab, jtab, qm, ktm, vh, nb, lam, subln_g.reshape(1, dv))


def _mlp_kernel(x_ref, sc_ref, sh_ref, g_ref, w1_ref, w2_ref, lng_ref, lnb_ref, out_ref, *, tf, alpha):
    x = x_ref[0]
    h = (x * (1.0 + sc_ref[...]) + sh_ref[...]).astype(MXU_DTYPE)
    y = jnp.zeros(x.shape, jnp.float32)
    for c in range(w1_ref.shape[1] // tf):
        a = jnp.maximum(_dot(h, w1_ref[:, c * tf:(c + 1) * tf]), 0.0)
        y = y + _dot((a * a).astype(MXU_DTYPE), w2_ref[c * tf:(c + 1) * tf, :])
    z = alpha * x + (1.0 + g_ref[...]) * y
    out_ref[0] = _layer_norm_rows(z, lng_ref[...], lnb_ref[...])


def _mlp(x, mod, layer, w1, w2, ln_g, ln_b, alpha):
    b, s, d = x.shape
    dff = w1.shape[1]
    tm, tf = 512, 1024
    row = lambda bb, t: (bb, t, 0)
    once = pl.Buffered(1)
    return pl.pallas_call(
        functools.partial(_mlp_kernel, tf=tf, alpha=alpha),
        grid=(b, s // tm),
        in_specs=[
            pl.BlockSpec((1, tm, d), row),
            _mod_spec(layer, 4, d), _mod_spec(layer, 3, d), _mod_spec(layer, 5, d),
            pl.BlockSpec((d, dff), lambda bb, t: (0, 0), pipeline_mode=once),
            pl.BlockSpec((dff, d), lambda bb, t: (0, 0), pipeline_mode=once),
            _const_spec((1, d)), _const_spec((1, d)),
        ],
        out_specs=pl.BlockSpec((1, tm, d), row),
        out_shape=jax.ShapeDtypeStruct((b, s, d), jnp.float32),
        compiler_params=_cparams("parallel", "parallel"),
        name="sqrelu_mlp",
    )(x, mod, mod, mod, w1.astype(MXU_DTYPE), w2.astype(MXU_DTYPE),
      ln_g.reshape(1, d), ln_b.reshape(1, d))


def kernel(x, c, rel_bias, ada_w, ada_b, ln_g, ln_b, a_w_in, a_kv_norm, a_w_uk, a_w_uv, a_w_o,
           b_w_in, b_lambda, b_subln, b_w_o, mlp_w1, mlp_w2):
    depth = ada_w.shape[0]
    b, s, d = x.shape
    alpha = (2 * depth) ** 0.25
    topk = min(TOPK_MAX, s // 4)
    mod = _ada_mod(c, ada_w, ada_b)
    nb_a = _bias_tiles(rel_bias, 128)
    nb_b = _bias_tiles(rel_bias, 256)
    for i in range(depth):
        jm = i // N_MIXERS
        if i % N_MIXERS == 0:
            q, ckv, iq, ikw = _proj_dsa(x, mod, i, a_w_in[jm], a_kv_norm[jm])
            ikt = jnp.swapaxes(ikw[..., :IDX_DIM].astype(MXU_DTYPE), 1, 2)
            mask = _indexer_mask(iq, ikt, ikw, topk)
            o = _dsa_attention(q, ckv, jnp.swapaxes(ckv, 1, 2), mask, nb_a,
                               a_w_uk[jm].astype(MXU_DTYPE), a_w_uv[jm].astype(MXU_DTYPE))
            w_o = a_w_o[jm]
        else:
            lam_init = 0.8 - 0.6 * math.exp(-0.3 * i)
            qf, kf, vf = _proj_diff(x, mod, i, b_w_in[jm])
            qm = jnp.transpose(qf.reshape(b, s, B_MAPS, B_HEAD_DIM), (0, 2, 1, 3))
            ktm = jnp.transpose(kf.reshape(b, s, B_MAPS, B_HEAD_DIM), (0, 2, 3, 1))
            vh = jnp.transpose(vf.reshape(b, s, B_HEADS, 2 * B_HEAD_DIM), (0, 2, 1, 3))
            o = _diff_attention(qm, ktm, vh, nb_b, b_lambda[jm], b_subln[jm], lam_init)
            w_o = b_w_o[jm]
        x = _oproj_ln(o, x, mod, i, w_o, ln_g[i, 0], ln_b[i, 0], alpha)
        x = _mlp(x, mod, i, mlp_w1[i], mlp_w2[i], ln_g[i, 1], ln_b[i, 1], alpha)
    return x
```

```python
import functools
import math

import jax
import jax.numpy as jnp
from jax import lax
from jax.experimental import pallas as pl
from jax.experimental.pallas import tpu as pltpu

N_MIXERS = 2
A_HEADS = 16
A_QK_DIM = 64
A_V_DIM = 64
A_KV_RANK = 256
IDX_HEADS = 8
IDX_DIM = 64
TOPK_MAX = 256
B_HEADS = 8
B_HEAD_DIM = 64
B_MAPS = 2 * B_HEADS
REL_BUCKETS = 32
REL_MAX_DIST = 128
LN_EPS = 1e-5
NEG = -1e30
M_INIT = 0.5 * NEG
LOG2E = math.log2(math.e)

LANE = 128
VMEM_LIMIT = 56 * 1024 * 1024

MXU_DTYPE = jnp.bfloat16


def _dot(a, b):
    return jnp.dot(a, b, preferred_element_type=jnp.float32)


def _cparams(*sem):
    return pltpu.CompilerParams(dimension_semantics=sem, vmem_limit_bytes=VMEM_LIMIT)


def _const_spec(shape):
    nd = len(shape)
    return pl.BlockSpec(shape, lambda *_: (0,) * nd)


def _layer_norm_rows(z, g, b):
    mu = jnp.mean(z, axis=-1, keepdims=True)
    zc = z - mu
    var = jnp.mean(zc * zc, axis=-1, keepdims=True)
    return zc * lax.rsqrt(var + LN_EPS) * g + b


def _ada_kernel(ct_ref, w_ref, b_ref, o_ref, *, nb):
    ct = ct_ref[...]
    s = ct * jax.nn.sigmoid(ct)
    w = w_ref[0]
    rows = [jnp.sum(w * s[:, b:b + 1], axis=0, keepdims=True) for b in range(nb)]
    o_ref[0] = jnp.concatenate(rows, axis=0) + b_ref[0]


def _ada_mod(c, ada_w, ada_b):
    depth, d, n = ada_w.shape
    nb = c.shape[0]
    tn = 768
    out = pl.pallas_call(
        functools.partial(_ada_kernel, nb=nb),
        grid=(depth, n // tn),
        in_specs=[
            pl.BlockSpec((d, nb), lambda l, j: (0, 0)),
            pl.BlockSpec((1, d, tn), lambda l, j: (l, 0, j)),
            pl.BlockSpec((1, 1, tn), lambda l, j: (l, 0, j)),
        ],
        out_specs=pl.BlockSpec((1, nb, tn), lambda l, j: (l, 0, j)),
        out_shape=jax.ShapeDtypeStruct((depth, nb, n), jnp.float32),
        compiler_params=_cparams("arbitrary", "arbitrary"),
        name="ada_mod",
    )(c.T, ada_w, ada_b.reshape(depth, 1, n))
    return out.reshape(depth, nb, 6, 1, d)


def _mod_spec(layer, k, d):
    return pl.BlockSpec((None, None, None, 1, d), lambda b, *_: (layer, b, k, 0, 0))


def _rel_bucket(dist):
    n = jnp.maximum(dist, 0)
    max_exact = REL_BUCKETS // 2
    nf = jnp.maximum(n, 1).astype(jnp.float32)
    large = max_exact + (jnp.log(nf / max_exact) / math.log(REL_MAX_DIST / max_exact)
                         * (REL_BUCKETS - max_exact)).astype(jnp.int32)
    large = jnp.minimum(large, REL_BUCKETS - 1)
    return jnp.where(n < max_exact, n, large)


def _bias_tile_kernel(tab_ref, bk_ref, o_ref, *, heads):
    bk = bk_ref[0]
    for h in range(heads):
        far = tab_ref[REL_BUCKETS - 1, h]
        acc = jnp.zeros(bk.shape, jnp.float32)
        for b in range(REL_BUCKETS - 1):
            acc = jnp.where(bk == b, (tab_ref[b, h] - far) * LOG2E, acc)
        o_ref[0, h] = jnp.where(bk < 0, NEG, acc)


def _bias_tiles(rel_bias, tq):
    heads = rel_bias.shape[1]
    r_sub = tq // LANE
    rels = jnp.arange(-2, r_sub + 1, dtype=jnp.int32)
    ql = jnp.arange(tq, dtype=jnp.int32)[None, :, None]
    sl = jnp.arange(LANE, dtype=jnp.int32)[None, None, :]
    dist = ql - rels[:, None, None] * LANE - sl
    bk = jnp.where(dist < 0, -1, _rel_bucket(dist))
    nk = r_sub + 3
    return pl.pallas_call(
        functools.partial(_bias_tile_kernel, heads=heads),
        grid=(nk,),
        in_specs=[
            pl.BlockSpec(memory_space=pltpu.SMEM),
            pl.BlockSpec((1, tq, LANE), lambda k: (k, 0, 0)),
        ],
        out_specs=pl.BlockSpec((1, heads, tq, LANE), lambda k: (k, 0, 0, 0)),
        out_shape=jax.ShapeDtypeStruct((nk, heads, tq, LANE), jnp.float32),
        compiler_params=_cparams("arbitrary"),
        name="bias_tiles",
    )(rel_bias, bk)


def _bias_index(i, j, c, tq, tk):
    rel = j * (tk // LANE) + c - i * (tq // LANE)
    return jnp.clip(rel + 2, 0, tq // LANE + 2)


def _causal_pairs(s, tq, tk):
    ii, jj = [], []
    for i in range(s // tq):
        for j in range((i * tq + tq - 1) // tk + 1):
            ii.append(i)
            jj.append(j)
    return jnp.asarray(ii, jnp.int32), jnp.asarray(jj, jnp.int32)


def _proj_dsa_kernel(x_ref, sc_ref, sh_ref, wq_ref, wc_ref, wi_ref, wk_ref, kvg_ref,
                     q_ref, ckv_ref, iq_ref, ikw_ref):
    h = (x_ref[0] * (1.0 + sc_ref[...]) + sh_ref[...]).astype(MXU_DTYPE)
    q_ref[0] = _dot(h, wq_ref[...]).astype(q_ref.dtype)
    ckv = _dot(h, wc_ref[...])
    ckv = ckv * lax.rsqrt(jnp.mean(ckv * ckv, axis=-1, keepdims=True) + LN_EPS) * kvg_ref[...]
    ckv_ref[0] = ckv.astype(ckv_ref.dtype)
    iq_ref[0] = _dot(h, wi_ref[...]).astype(iq_ref.dtype)
    ikw_ref[0] = _dot(h, wk_ref[...])


def _proj_dsa(x, mod, layer, w_in, kv_norm):
    b, s, d = x.shape
    tm = 512
    nq = A_HEADS * A_QK_DIM
    ni = IDX_HEADS * IDX_DIM
    o1, o2, o3 = nq, nq + A_KV_RANK, nq + A_KV_RANK + ni
    wq = w_in[:, :o1].astype(MXU_DTYPE)
    wc = w_in[:, o1:o2].astype(MXU_DTYPE)
    wi = w_in[:, o2:o3].astype(MXU_DTYPE)
    wk = jnp.pad(w_in[:, o3:], ((0, 0), (0, LANE - (w_in.shape[1] - o3)))).astype(MXU_DTYPE)
    row = lambda bb, t: (bb, t, 0)
    return pl.pallas_call(
        _proj_dsa_kernel,
        grid=(b, s // tm),
        in_specs=[
            pl.BlockSpec((1, tm, d), row),
            _mod_spec(layer, 1, d), _mod_spec(layer, 0, d),
            _const_spec((d, nq)), _const_spec((d, A_KV_RANK)), _const_spec((d, ni)),
            _const_spec((d, LANE)), _const_spec((1, A_KV_RANK)),
        ],
        out_specs=[
            pl.BlockSpec((1, tm, nq), row), pl.BlockSpec((1, tm, A_KV_RANK), row),
            pl.BlockSpec((1, tm, ni), row), pl.BlockSpec((1, tm, LANE), row),
        ],
        out_shape=[
            jax.ShapeDtypeStruct((b, s, nq), MXU_DTYPE),
            jax.ShapeDtypeStruct((b, s, A_KV_RANK), MXU_DTYPE),
            jax.ShapeDtypeStruct((b, s, ni), MXU_DTYPE),
            jax.ShapeDtypeStruct((b, s, LANE), jnp.float32),
        ],
        compiler_params=_cparams("parallel", "parallel"),
        name="proj_dsa",
    )(x, mod, mod, wq, wc, wi, wk, kv_norm.reshape(1, A_KV_RANK))


def _indexer_kernel(iq_ref, ikt_ref, ikw_ref, mask_ref, keys_ref, p_ref, *, tq, ck, topk, s_len):
    i = pl.program_id(1)
    q0 = i * tq
    nkc = (q0 + tq + ck - 1) // ck
    iq = iq_ref[0]
    iqs = jnp.concatenate([iq[:, h * IDX_DIM:(h + 1) * IDX_DIM] for h in range(IDX_HEADS)], axis=0)
    w = ikw_ref[0][:, IDX_DIM:IDX_DIM + IDX_HEADS] * (IDX_HEADS ** -0.5 * IDX_DIM ** -0.5)
    tpos = q0 + lax.broadcasted_iota(jnp.int32, (tq, 1), 0)
    lane_pos = lax.broadcasted_iota(jnp.int32, (1, ck), 1)

    def score_chunk(kc, carry):
        k0 = pl.multiple_of(kc * ck, ck)
        lg = _dot(iqs, ikt_ref[0, :, pl.ds(k0, ck)])
        sc = jnp.zeros((tq, ck), jnp.float32)
        for h in range(IDX_HEADS):
            sc = sc + jnp.maximum(lg[h * tq:(h + 1) * tq], 0.0) * w[:, h:h + 1]
        sc = jnp.where(k0 + lane_pos <= tpos, sc, -jnp.inf)
        bits = pltpu.bitcast(sc, jnp.int32)
        keys_ref[:, pl.ds(k0, ck)] = bits ^ ((bits >> 31) & 0x7FFFFFFF)
        return carry

    lax.fori_loop(0, nkc, score_chunk, 0)

    def count(pred):
        def body(kc, acc):
            k0 = pl.multiple_of(kc * ck, ck)
            ind = pred(keys_ref[:, pl.ds(k0, ck)], k0 + lane_pos).astype(jnp.int32)
            for c in range(ck // LANE):
                acc = acc + ind[:, c * LANE:(c + 1) * LANE]
            return acc
        acc = lax.fori_loop(0, nkc, body, jnp.zeros((tq, LANE), jnp.int32))
        return jnp.sum(acc, axis=1, keepdims=True)

    k_eff = jnp.minimum(topk, tpos + 1)
    int_min = jnp.int32(-2 ** 31)
    zero = jnp.zeros((tq, 1), jnp.int32)
    prefix = jnp.where(count(lambda k, sp: k >= zero) >= k_eff, zero, int_min)

    def bit_step(it, prefix):
        cand = prefix + jnp.left_shift(jnp.int32(1), 30 - it)
        return jnp.where(count(lambda k, sp: k >= cand) >= k_eff, cand, prefix)

    tau = lax.fori_loop(0, 31, bit_step, prefix)
    cnt_gt = count(lambda k, sp: k > tau)
    cnt_eq = count(lambda k, sp: k == tau)
    need = k_eff - cnt_gt
    overflow = cnt_eq > need
    p_ref[...] = jnp.full((tq, 1), s_len, jnp.int32)

    @pl.when(jnp.max(overflow.astype(jnp.int32)) > 0)
    def _():
        def pos_step(it, pos):
            cand = pos + jnp.left_shift(jnp.int32(1), (s_len.bit_length() - 1) - it)
            below = count(lambda k, sp: (k == tau) & (sp < cand))
            return jnp.where(below < need, cand, pos)
        pos = lax.fori_loop(0, s_len.bit_length(), pos_step, zero)
        p_ref[...] = jnp.where(overflow, pos, s_len)

    last_eq = p_ref[...]

    def write_chunk(kc, carry):
        k0 = pl.multiple_of(kc * ck, ck)

        @pl.when(kc < nkc)
        def _():
            k = keys_ref[:, pl.ds(k0, ck)]
            sel = (k > tau) | ((k == tau) & (k0 + lane_pos <= last_eq))
            mask_ref[0, :, pl.ds(k0, ck)] = jnp.where(sel, 0.0, NEG).astype(mask_ref.dtype)

        @pl.when(kc >= nkc)
        def _():
            mask_ref[0, :, pl.ds(k0, ck)] = jnp.full((tq, ck), NEG, mask_ref.dtype)
        return carry

    lax.fori_loop(0, s_len // ck, write_chunk, 0)


def _indexer_mask(iq, ikt, ikw, topk):
    b, s, ni = iq.shape
    tq, ck = 128, 512
    return pl.pallas_call(
        functools.partial(_indexer_kernel, tq=tq, ck=ck, topk=topk, s_len=s),
        grid=(b, s // tq),
        in_specs=[
            pl.BlockSpec((1, tq, ni), lambda bb, i: (bb, i, 0)),
            pl.BlockSpec((1, IDX_DIM, s), lambda bb, i: (bb, 0, 0)),
            pl.BlockSpec((1, tq, LANE), lambda bb, i: (bb, i, 0)),
        ],
        out_specs=pl.BlockSpec((1, tq, s), lambda bb, i: (bb, i, 0)),
        out_shape=jax.ShapeDtypeStruct((b, s, s), jnp.bfloat16),
        scratch_shapes=[pltpu.VMEM((tq, s), jnp.int32), pltpu.VMEM((tq, 1), jnp.int32)],
        compiler_params=_cparams("parallel", "parallel"),
        name="indexer_topk",
    )(iq, ikt, ikw)


def _flash_update(s_cols, m_ref, l_ref, rows):
    m_old = m_ref[rows, :]
    m_cur = functools.reduce(jnp.maximum, s_cols)
    m_new = jnp.maximum(m_old, jnp.max(m_cur, axis=-1, keepdims=True))
    alpha = jnp.exp2(m_old - m_new)
    ps = [jnp.exp2(sc - m_new) for sc in s_cols]
    l_ref[rows, :] = alpha * l_ref[rows, :] + functools.reduce(jnp.add, ps)
    m_ref[rows, :] = m_new
    return jnp.concatenate(ps, axis=1).astype(MXU_DTYPE), alpha


def _tile_is_near(i, j, tq, tk):
    return (j + 1) * (tk // LANE) - i * (tq // LANE) >= 0


def _dsa_attn_kernel(itab, jtab, q_ref, wuk_ref, ckvt_ref, ckv_ref, mask_ref, nb_ref, wuv_ref,
                     o_ref, qlat_ref, m_ref, l_ref, acc_ref, *, tq, tk, hc):
    pidx = pl.program_id(1)
    i = itab[pidx]
    j = jtab[pidx]
    jlast = (i * tq + tq - 1) // tk
    heads = A_HEADS
    nsub = tk // LANE
    nchunk = heads // hc
    crows = hc * tq

    @pl.when(j == 0)
    def _():
        for h in range(heads):
            qh = q_ref[0, :, h * A_QK_DIM:(h + 1) * A_QK_DIM]
            ql = _dot(qh, wuk_ref[h]) * (A_QK_DIM ** -0.5 * LOG2E)
            qlat_ref[h * tq:(h + 1) * tq, :] = ql.astype(qlat_ref.dtype)
        m_ref[...] = jnp.full(m_ref.shape, M_INIT, jnp.float32)
        l_ref[...] = jnp.zeros(l_ref.shape, jnp.float32)
        acc_ref[...] = jnp.zeros(acc_ref.shape, jnp.float32)

    def step(with_bias):
        ckvt = ckvt_ref[0]
        ckv = ckv_ref[0]
        maskf = mask_ref[0].astype(jnp.float32)
        mcols = [maskf[:, c * LANE:(c + 1) * LANE] for c in range(nsub)]
        bidx = [_bias_index(i, j, c, tq, tk) for c in range(nsub)] if with_bias else None
        s_next = _dot(qlat_ref[0:crows, :], ckvt)
        for ci in range(nchunk):
            s_all = s_next
            if ci + 1 < nchunk:
                s_next = _dot(qlat_ref[(ci + 1) * crows:(ci + 2) * crows, :], ckvt)
            ps, alphas = [], []
            for hh in range(hc):
                h = ci * hc + hh
                cols = []
                for c in range(nsub):
                    add = mcols[c] + nb_ref[bidx[c], h] if with_bias else mcols[c]
                    cols.append(s_all[hh * tq:(hh + 1) * tq, c * LANE:(c + 1) * LANE] + add)
                p, alpha = _flash_update(cols, m_ref, l_ref, slice(h * tq, (h + 1) * tq))
                ps.append(p)
                alphas.append(alpha)
            pv = _dot(jnp.concatenate(ps, axis=0), ckv)
            a = jnp.concatenate(alphas, axis=0)
            rows = slice(ci * crows, (ci + 1) * crows)
            for half in range(A_KV_RANK // LANE):
                cs = slice(half * LANE, (half + 1) * LANE)
                acc_ref[rows, cs] = a * acc_ref[rows, cs] + pv[:, cs]

    near = _tile_is_near(i, j, tq, tk)
    pl.when(near)(lambda: step(True))
    pl.when(jnp.logical_not(near))(lambda: step(False))

    @pl.when(j == jlast)
    def _():
        for h in range(heads):
            rows = slice(h * tq, (h + 1) * tq)
            l = jnp.sum(l_ref[rows, :], axis=-1, keepdims=True)
            oh = _dot((acc_ref[rows, :] / l).astype(MXU_DTYPE), wuv_ref[h])
            o_ref[0, :, h * A_V_DIM:(h + 1) * A_V_DIM] = oh.astype(o_ref.dtype)


def _dsa_attention(q, ckv, ckvt, mask, nb, w_uk, w_uv):
    b, s, nq = q.shape
    tq, tk, hc = 128, min(1024, s), 4
    itab, jtab = _causal_pairs(s, tq, tk)
    rows = A_HEADS * tq
    once = pl.Buffered(1)
    grid_spec = pltpu.PrefetchScalarGridSpec(
        num_scalar_prefetch=2,
        grid=(b, int(itab.shape[0])),
        in_specs=[
            pl.BlockSpec((1, tq, nq), lambda bb, p, it, jt: (bb, it[p], 0)),
            pl.BlockSpec(w_uk.shape, lambda bb, p, it, jt: (0, 0, 0), pipeline_mode=once),
            pl.BlockSpec((1, A_KV_RANK, tk), lambda bb, p, it, jt: (bb, 0, jt[p])),
            pl.BlockSpec((1, tk, A_KV_RANK), lambda bb, p, it, jt: (bb, jt[p], 0)),
            pl.BlockSpec((1, tq, tk), lambda bb, p, it, jt: (bb, it[p], jt[p])),
            pl.BlockSpec(nb.shape, lambda bb, p, it, jt: (0, 0, 0, 0), pipeline_mode=once),
            pl.BlockSpec(w_uv.shape, lambda bb, p, it, jt: (0, 0, 0), pipeline_mode=once),
        ],
        out_specs=pl.BlockSpec((1, tq, A_HEADS * A_V_DIM), lambda bb, p, it, jt: (bb, it[p], 0)),
        scratch_shapes=[
            pltpu.VMEM((rows, A_KV_RANK), MXU_DTYPE),
            pltpu.VMEM((rows, LANE), jnp.float32),
            pltpu.VMEM((rows, LANE), jnp.float32),
            pltpu.VMEM((rows, A_KV_RANK), jnp.float32),
        ],
    )
    return pl.pallas_call(
        functools.partial(_dsa_attn_kernel, tq=tq, tk=tk, hc=hc),
        grid_spec=grid_spec,
        out_shape=jax.ShapeDtypeStruct((b, s, A_HEADS * A_V_DIM), MXU_DTYPE),
        compiler_params=_cparams("parallel", "arbitrary"),
        name="dsa_attention",
    )(itab, jtab, q, w_uk, ckvt, ckv, mask, nb, w_uv)


def _oproj_ln_kernel(o_ref, x_ref, g_ref, w_ref, lng_ref, lnb_ref, out_ref, *, alpha):
    y = _dot(o_ref[0], w_ref[...])
    z = alpha * x_ref[0] + (1.0 + g_ref[...]) * y
    out_ref[0] = _layer_norm_rows(z, lng_ref[...], lnb_ref[...])


def _oproj_ln(o, x, mod, layer, w_o, ln_g, ln_b, alpha):
    b, s, d = x.shape
    tm = 512
    n_in = o.shape[-1]
    row = lambda bb, t: (bb, t, 0)
    return pl.pallas_call(
        functools.partial(_oproj_ln_kernel, alpha=alpha),
        grid=(b, s // tm),
        in_specs=[
            pl.BlockSpec((1, tm, n_in), row), pl.BlockSpec((1, tm, d), row),
            _mod_spec(layer, 2, d), _const_spec((n_in, d)),
            _const_spec((1, d)), _const_spec((1, d)),
        ],
        out_specs=pl.BlockSpec((1, tm, d), row),
        out_shape=jax.ShapeDtypeStruct((b, s, d), jnp.float32),
        compiler_params=_cparams("parallel", "parallel"),
        name="oproj_ln",
    )(o, x, mod, w_o.astype(MXU_DTYPE), ln_g.reshape(1, d), ln_b.reshape(1, d))


def _proj_diff_kernel(x_ref, sc_ref, sh_ref, wq_ref, wk_ref, wv_ref, q_ref, k_ref, v_ref):
    h = (x_ref[0] * (1.0 + sc_ref[...]) + sh_ref[...]).astype(MXU_DTYPE)
    q_ref[0] = (_dot(h, wq_ref[...]) * (B_HEAD_DIM ** -0.5 * LOG2E)).astype(q_ref.dtype)
    k_ref[0] = _dot(h, wk_ref[...]).astype(k_ref.dtype)
    v_ref[0] = _dot(h, wv_ref[...]).astype(v_ref.dtype)


def _proj_diff(x, mod, layer, w_in):
    b, s, d = x.shape
    tm = 512
    n = w_in.shape[1] // 3
    ws = [w_in[:, k * n:(k + 1) * n].astype(MXU_DTYPE) for k in range(3)]
    row = lambda bb, t: (bb, t, 0)
    return pl.pallas_call(
        _proj_diff_kernel,
        grid=(b, s // tm),
        in_specs=[pl.BlockSpec((1, tm, d), row), _mod_spec(layer, 1, d), _mod_spec(layer, 0, d)]
        + [_const_spec((d, n))] * 3,
        out_specs=[pl.BlockSpec((1, tm, n), row)] * 3,
        out_shape=[jax.ShapeDtypeStruct((b, s, n), MXU_DTYPE)] * 3,
        compiler_params=_cparams("parallel", "parallel"),
        name="proj_diff",
    )(x, mod, mod, *ws)


def _diff_attn_kernel(itab, jtab, q_ref, kt_ref, v_ref, nb_ref, lam_ref, g_ref, o_ref,
                      m_ref, l_ref, acc_ref, *, tq, tk, lam_init):
    pidx = pl.program_id(1)
    i = itab[pidx]
    j = jtab[pidx]
    jlast = (i * tq + tq - 1) // tk
    dv = 2 * B_HEAD_DIM

    @pl.when(j == 0)
    def _():
        m_ref[...] = jnp.full(m_ref.shape, M_INIT, jnp.float32)
        l_ref[...] = jnp.zeros(l_ref.shape, jnp.float32)
        acc_ref[...] = jnp.zeros(acc_ref.shape, jnp.float32)

    nsub = tk // LANE

    def step(with_bias):
        bidx = [_bias_index(i, j, c, tq, tk) for c in range(nsub)] if with_bias else None

        def logits(h):
            return [_dot(q_ref[0, 2 * h + mm], kt_ref[0, 2 * h + mm]) for mm in range(2)]

        s_next = logits(0)
        for h in range(B_HEADS):
            s_pair = s_next
            if h + 1 < B_HEADS:
                s_next = logits(h + 1)
            ps, alphas = [], []
            for mm in range(2):
                col = 2 * h + mm
                cols = [s_pair[mm][:, c * LANE:(c + 1) * LANE] for c in range(nsub)]
                if with_bias:
                    cols = [cols[c] + nb_ref[bidx[c], col] for c in range(nsub)]
                p, alpha = _flash_update(cols, m_ref, l_ref, slice(col * tq, (col + 1) * tq))
                ps.append(p)
                alphas.append(alpha)
            pv = _dot(jnp.concatenate(ps, axis=0), v_ref[0, h])
            rows = slice(2 * h * tq, (2 * h + 2) * tq)
            acc_ref[rows, :] = jnp.concatenate(alphas, axis=0) * acc_ref[rows, :] + pv

    near = _tile_is_near(i, j, tq, tk)
    pl.when(near)(lambda: step(True))
    pl.when(jnp.logical_not(near))(lambda: step(False))

    @pl.when(j == jlast)
    def _():
        lam = lam_ref[...]
        lam_full = (jnp.exp(jnp.sum(lam[0:1] * lam[1:2], axis=-1, keepdims=True))
                    - jnp.exp(jnp.sum(lam[2:3] * lam[3:4], axis=-1, keepdims=True)) + lam_init)
        for h in range(B_HEADS):
            r1 = slice(2 * h * tq, (2 * h + 1) * tq)
            r2 = slice((2 * h + 1) * tq, (2 * h + 2) * tq)
            l1 = jnp.sum(l_ref[r1, :], axis=-1, keepdims=True)
            l2 = jnp.sum(l_ref[r2, :], axis=-1, keepdims=True)
            o = acc_ref[r1, :] / l1 - lam_full * (acc_ref[r2, :] / l2)
            o = o * lax.rsqrt(jnp.mean(o * o, axis=-1, keepdims=True) + LN_EPS) * g_ref[...]
            o_ref[0, :, h * dv:(h + 1) * dv] = (o * (1.0 - lam_init)).astype(o_ref.dtype)


def _diff_attention(qm, ktm, vh, nb, lam, subln_g, lam_init):
    b, maps, s, dh = qm.shape
    tq, tk = 256, min(1024, s)
    dv = 2 * B_HEAD_DIM
    assert dv == LANE
    itab, jtab = _causal_pairs(s, tq, tk)
    once = pl.Buffered(1)
    grid_spec = pltpu.PrefetchScalarGridSpec(
        num_scalar_prefetch=2,
        grid=(b, int(itab.shape[0])),
        in_specs=[
            pl.BlockSpec((1, maps, tq, dh), lambda bb, p, it, jt: (bb, 0, it[p], 0)),
            pl.BlockSpec((1, maps, dh, tk), lambda bb, p, it, jt: (bb, 0, 0, jt[p])),
            pl.BlockSpec((1, B_HEADS, tk, dv), lambda bb, p, it, jt: (bb, 0, jt[p], 0)),
            pl.BlockSpec(nb.shape, lambda bb, p, it, jt: (0, 0, 0, 0), pipeline_mode=once),
            pl.BlockSpec(lam.shape, lambda bb, p, it, jt: (0, 0)),
            pl.BlockSpec((1, dv), lambda bb, p, it, jt: (0, 0)),
        ],
        out_specs=pl.BlockSpec((1, tq, B_HEADS * dv), lambda bb, p, it, jt: (bb, it[p], 0)),
        scratch_shapes=[
            pltpu.VMEM((maps * tq, LANE), jnp.float32),
            pltpu.VMEM((maps * tq, LANE), jnp.float32),
            pltpu.VMEM((maps * tq, dv), jnp.float32),
        ],
    )
    return pl.pallas_call(
        functools.partial(_diff_attn_kernel, tq=tq, tk=tk, lam_init=lam_init),
        grid_spec=grid_spec,
        out_shape=jax.ShapeDtypeStruct((b, s, B_HEADS * dv), MXU_DTYPE),
        compiler_params=_cparams("parallel", "arbitrary"),
        name="diff_attention",
    )(itab, jtab, qm, ktm, vh, nb, lam, subln_g.reshape(1, dv))


def _mlp_kernel(x_ref, sc_ref, sh_ref, g_ref, w1_ref, w2_ref, lng_ref, lnb_ref, out_ref, *, tf, alpha):
    x = x_ref[0]
    h = (x * (1.0 + sc_ref[...]) + sh_ref[...]).astype(MXU_DTYPE)
    y = jnp.zeros(x.shape, jnp.float32)
    for c in range(w1_ref.shape[1] // tf):
        a = jnp.maximum(_dot(h, w1_ref[:, c * tf:(c + 1) * tf]), 0.0)
        y = y + _dot((a * a).astype(MXU_DTYPE), w2_ref[c * tf:(c + 1) * tf, :])
    z = alpha * x + (1.0 + g_ref[...]) * y
    out_ref[0] = _layer_norm_rows(z, lng_ref[...], lnb_ref[...])


def _mlp(x, mod, layer, w1, w2, ln_g, ln_b, alpha):
    b, s, d = x.shape
    dff = w1.shape[1]
    tm, tf = 512, 1024
    row = lambda bb, t: (bb, t, 0)
    once = pl.Buffered(1)
    return pl.pallas_call(
        functools.partial(_mlp_kernel, tf=tf, alpha=alpha),
        grid=(b, s // tm),
        in_specs=[
            pl.BlockSpec((1, tm, d), row),
            _mod_spec(layer, 4, d), _mod_spec(layer, 3, d), _mod_spec(layer, 5, d),
            pl.BlockSpec((d, dff), lambda bb, t: (0, 0), pipeline_mode=once),
            pl.BlockSpec((dff, d), lambda bb, t: (0, 0), pipeline_mode=once),
            _const_spec((1, d)), _const_spec((1, d)),
        ],
        out_specs=pl.BlockSpec((1, tm, d), row),
        out_shape=jax.ShapeDtypeStruct((b, s, d), jnp.float32),
        compiler_params=_cparams("parallel", "parallel"),
        name="sqrelu_mlp",
    )(x, mod, mod, mod, w1.astype(MXU_DTYPE), w2.astype(MXU_DTYPE),
      ln_g.reshape(1, d), ln_b.reshape(1, d))


def kernel(x, c, rel_bias, ada_w, ada_b, ln_g, ln_b, a_w_in, a_kv_norm, a_w_uk, a_w_uv, a_w_o,
           b_w_in, b_lambda, b_subln, b_w_o, mlp_w1, mlp_w2):
    depth = ada_w.shape[0]
    b, s, d = x.shape
    alpha = (2 * depth) ** 0.25
    topk = min(TOPK_MAX, s // 4)
    mod = _ada_mod(c, ada_w, ada_b)
    nb_a = _bias_tiles(rel_bias, 128)
    nb_b = _bias_tiles(rel_bias, 256)
    for i in range(depth):
        jm = i // N_MIXERS
        if i % N_MIXERS == 0:
            q, ckv, iq, ikw = _proj_dsa(x, mod, i, a_w_in[jm], a_kv_norm[jm])
            ikt = jnp.swapaxes(ikw[..., :IDX_DIM].astype(MXU_DTYPE), 1, 2)
            mask = _indexer_mask(iq, ikt, ikw, topk)
            o = _dsa_attention(q, ckv, jnp.swapaxes(ckv, 1, 2), mask, nb_a,
                               a_w_uk[jm].astype(MXU_DTYPE), a_w_uv[jm].astype(MXU_DTYPE))
            w_o = a_w_o[jm]
        else:
            lam_init = 0.8 - 0.6 * math.exp(-0.3 * i)
            qf, kf, vf = _proj_diff(x, mod, i, b_w_in[jm])
            qm = jnp.transpose(qf.reshape(b, s, B_MAPS, B_HEAD_DIM), (0, 2, 1, 3))
            ktm = jnp.transpose(kf.reshape(b, s, B_MAPS, B_HEAD_DIM), (0, 2, 3, 1))
            vh = jnp.transpose(vf.reshape(b, s, B_HEADS, 2 * B_HEAD_DIM), (0, 2, 1, 3))
            o = _diff_attention(qm, ktm, vh, nb_b, b_lambda[jm], b_subln[jm], lam_init)
            w_o = b_w_o[jm]
        x = _oproj_ln(o, x, mod, i, w_o, ln_g[i, 0], ln_b[i, 0], alpha)
        x = _mlp(x, mod, i, mlp_w1[i], mlp_w2[i], ln_g[i, 1], ln_b[i, 1], alpha)
    return x
```

```python
import functools
import math

import jax
import jax.numpy as jnp
from jax import lax
from jax.experimental import pallas as pl
from jax.experimental.pallas import tpu as pltpu

N_MIXERS = 2
A_HEADS = 16
A_QK_DIM = 64
A_V_DIM = 64
A_KV_RANK = 256
IDX_HEADS = 8
IDX_DIM = 64
TOPK_MAX = 256
B_HEADS = 8
B_HEAD_DIM = 64
B_MAPS = 2 * B_HEADS
REL_BUCKETS = 32
REL_MAX_DIST = 128
LN_EPS = 1e-5
NEG = -1e30
M_INIT = 0.5 * NEG
LOG2E = math.log2(math.e)
SEARCH_ROUND = ("interpolate", "value_mid", "interpolate", "key_mid")
SEARCH_INTERP_ROUNDS = 6

LANE = 128
SUB = 8
VMEM_LIMIT = 56 * 1024 * 1024

MXU_DTYPE = jnp.bfloat16


def _dot(a, b):
    return jnp.dot(a, b, preferred_element_type=jnp.float32)


def _cparams(*sem):
    return pltpu.CompilerParams(dimension_semantics=sem, vmem_limit_bytes=VMEM_LIMIT)


def _const_spec(shape):
    nd = len(shape)
    return pl.BlockSpec(shape, lambda *_: (0,) * nd)


def _layer_norm_rows(z, g, b):
    mu = jnp.mean(z, axis=-1, keepdims=True)
    zc = z - mu
    var = jnp.mean(zc * zc, axis=-1, keepdims=True)
    return zc * lax.rsqrt(var + LN_EPS) * g + b


def _ada_kernel(ct_ref, w_ref, b_ref, o_ref, *, nb):
    ct = ct_ref[...]
    s = ct * jax.nn.sigmoid(ct)
    w = w_ref[0]
    rows = [jnp.sum(w * s[:, b:b + 1], axis=0, keepdims=True) for b in range(nb)]
    o_ref[0] = jnp.concatenate(rows, axis=0) + b_ref[0]


def _ada_mod(c, ada_w, ada_b):
    depth, d, n = ada_w.shape
    nb = c.shape[0]
    tn = 768
    out = pl.pallas_call(
        functools.partial(_ada_kernel, nb=nb),
        grid=(depth, n // tn),
        in_specs=[
            pl.BlockSpec((d, nb), lambda l, j: (0, 0)),
            pl.BlockSpec((1, d, tn), lambda l, j: (l, 0, j)),
            pl.BlockSpec((1, 1, tn), lambda l, j: (l, 0, j)),
        ],
        out_specs=pl.BlockSpec((1, nb, tn), lambda l, j: (l, 0, j)),
        out_shape=jax.ShapeDtypeStruct((depth, nb, n), jnp.float32),
        compiler_params=_cparams("arbitrary", "arbitrary"),
        name="ada_mod",
    )(c.T, ada_w, ada_b.reshape(depth, 1, n))
    return out.reshape(depth, nb, 6, 1, d)


def _mod_spec(layer, k, d):
    return pl.BlockSpec((None, None, None, 1, d), lambda b, *_: (layer, b, k, 0, 0))


def _rel_bucket(dist):
    n = jnp.maximum(dist, 0)
    max_exact = REL_BUCKETS // 2
    nf = jnp.maximum(n, 1).astype(jnp.float32)
    large = max_exact + (jnp.log(nf / max_exact) / math.log(REL_MAX_DIST / max_exact)
                         * (REL_BUCKETS - max_exact)).astype(jnp.int32)
    large = jnp.minimum(large, REL_BUCKETS - 1)
    return jnp.where(n < max_exact, n, large)


def _bias_tile_kernel(tab_ref, bk_ref, o_ref, *, heads):
    bk = bk_ref[0]
    for h in range(heads):
        far = tab_ref[REL_BUCKETS - 1, h]
        acc = jnp.zeros(bk.shape, jnp.float32)
        for b in range(REL_BUCKETS - 1):
            acc = jnp.where(bk == b, (tab_ref[b, h] - far) * LOG2E, acc)
        o_ref[0, h] = jnp.where(bk < 0, NEG, acc)


def _bias_tiles(rel_bias, tq):
    heads = rel_bias.shape[1]
    r_sub = tq // LANE
    rels = jnp.arange(-2, r_sub + 1, dtype=jnp.int32)
    ql = jnp.arange(tq, dtype=jnp.int32)[None, :, None]
    sl = jnp.arange(LANE, dtype=jnp.int32)[None, None, :]
    dist = ql - rels[:, None, None] * LANE - sl
    bk = jnp.where(dist < 0, -1, _rel_bucket(dist))
    nk = r_sub + 3
    return pl.pallas_call(
        functools.partial(_bias_tile_kernel, heads=heads),
        grid=(nk,),
        in_specs=[
            pl.BlockSpec(memory_space=pltpu.SMEM),
            pl.BlockSpec((1, tq, LANE), lambda k: (k, 0, 0)),
        ],
        out_specs=pl.BlockSpec((1, heads, tq, LANE), lambda k: (k, 0, 0, 0)),
        out_shape=jax.ShapeDtypeStruct((nk, heads, tq, LANE), jnp.float32),
        compiler_params=_cparams("arbitrary"),
        name="bias_tiles",
    )(rel_bias, bk)


def _bias_index(i, j, c, tq, tk):
    rel = j * (tk // LANE) + c - i * (tq // LANE)
    return jnp.clip(rel + 2, 0, tq // LANE + 2)


def _causal_pairs(s, tq, tk):
    ii, jj = [], []
    for i in range(s // tq):
        for j in range((i * tq + tq - 1) // tk + 1):
            ii.append(i)
            jj.append(j)
    return jnp.asarray(ii, jnp.int32), jnp.asarray(jj, jnp.int32)


def _proj_dsa_kernel(x_ref, sc_ref, sh_ref, wq_ref, wc_ref, wi_ref, wk_ref, kvg_ref,
                     q_ref, ckv_ref, iq_ref, ikw_ref):
    h = (x_ref[0] * (1.0 + sc_ref[...]) + sh_ref[...]).astype(MXU_DTYPE)
    q_ref[0] = _dot(h, wq_ref[...]).astype(q_ref.dtype)
    ckv = _dot(h, wc_ref[...])
    ckv = ckv * lax.rsqrt(jnp.mean(ckv * ckv, axis=-1, keepdims=True) + LN_EPS) * kvg_ref[...]
    ckv_ref[0] = ckv.astype(ckv_ref.dtype)
    iq_ref[0] = _dot(h, wi_ref[...]).astype(iq_ref.dtype)
    ikw_ref[0] = _dot(h, wk_ref[...])


def _proj_dsa(x, mod, layer, w_in, kv_norm):
    b, s, d = x.shape
    tm = 512
    nq = A_HEADS * A_QK_DIM
    ni = IDX_HEADS * IDX_DIM
    o1, o2, o3 = nq, nq + A_KV_RANK, nq + A_KV_RANK + ni
    wq = w_in[:, :o1].astype(MXU_DTYPE)
    wc = w_in[:, o1:o2].astype(MXU_DTYPE)
    wi = w_in[:, o2:o3].astype(MXU_DTYPE)
    wk = jnp.pad(w_in[:, o3:], ((0, 0), (0, LANE - (w_in.shape[1] - o3)))).astype(MXU_DTYPE)
    row = lambda bb, t: (bb, t, 0)
    return pl.pallas_call(
        _proj_dsa_kernel,
        grid=(b, s // tm),
        in_specs=[
            pl.BlockSpec((1, tm, d), row),
            _mod_spec(layer, 1, d), _mod_spec(layer, 0, d),
            _const_spec((d, nq)), _const_spec((d, A_KV_RANK)), _const_spec((d, ni)),
            _const_spec((d, LANE)), _const_spec((1, A_KV_RANK)),
        ],
        out_specs=[
            pl.BlockSpec((1, tm, nq), row), pl.BlockSpec((1, tm, A_KV_RANK), row),
            pl.BlockSpec((1, tm, ni), row), pl.BlockSpec((1, tm, LANE), row),
        ],
        out_shape=[
            jax.ShapeDtypeStruct((b, s, nq), MXU_DTYPE),
            jax.ShapeDtypeStruct((b, s, A_KV_RANK), MXU_DTYPE),
            jax.ShapeDtypeStruct((b, s, ni), MXU_DTYPE),
            jax.ShapeDtypeStruct((b, s, LANE), jnp.float32),
        ],
        compiler_params=_cparams("parallel", "parallel"),
        name="proj_dsa",
    )(x, mod, mod, wq, wc, wi, wk, kv_norm.reshape(1, A_KV_RANK))


def _float_key(v):
    bits = pltpu.bitcast(v, jnp.int32)
    return bits ^ ((bits >> 31) & 0x7FFFFFFF)


def _key_float(k):
    return pltpu.bitcast(k ^ ((k >> 31) & 0x7FFFFFFF), jnp.float32)


def _indexer_kernel(ik_ref, iqt_ref, iwt_ref, mask_ref, keys_ref, p_ref, *, tq, ck, topk, s_len):
    i = pl.program_id(1)
    q0 = i * tq
    nkc = (q0 + tq + ck - 1) // ck
    w = iwt_ref[0] * (IDX_HEADS ** -0.5 * IDX_DIM ** -0.5)
    tpos = q0 + lax.broadcasted_iota(jnp.int32, (1, tq), 1)
    srow = lax.broadcasted_iota(jnp.int32, (ck, 1), 0)
    int_max = jnp.int32(2 ** 31 - 1)
    int_min = jnp.int32(-2 ** 31)

    def fold(x, op):
        slabs = [x[r * SUB:(r + 1) * SUB] for r in range(ck // SUB)]
        while len(slabs) > 1:
            slabs = [op(a, b) for a, b in zip(slabs[0::2], slabs[1::2])]
        return slabs[0]

    def score_chunk(kc, carry):
        kmax, kmin = carry
        k0 = pl.multiple_of(kc * ck, ck)
        ik = ik_ref[0, pl.ds(k0, ck), :]
        sc = jnp.zeros((ck, tq), jnp.float32)
        for h in range(IDX_HEADS):
            lg = _dot(ik, iqt_ref[0, h * IDX_DIM:(h + 1) * IDX_DIM, :])
            sc = sc + jnp.maximum(lg, 0.0) * w[h:h + 1, :]
        causal = k0 + srow <= tpos
        key = _float_key(jnp.where(causal, sc, -jnp.inf))
        keys_ref[pl.ds(k0, ck), :] = key
        return (jnp.maximum(kmax, fold(key, jnp.maximum)),
                jnp.minimum(kmin, fold(jnp.where(causal, key, int_max), jnp.minimum)))

    kmax, kmin = lax.fori_loop(0, nkc, score_chunk, (jnp.full((SUB, tq), int_min, jnp.int32),
                                                      jnp.full((SUB, tq), int_max, jnp.int32)))
    kmax = jnp.max(kmax, axis=0, keepdims=True)
    kmin = jnp.min(kmin, axis=0, keepdims=True)

    def count(pred):
        def body(kc, acc):
            k0 = pl.multiple_of(kc * ck, ck)
            ind = pred(keys_ref[pl.ds(k0, ck), :], k0 + srow).astype(jnp.float32)
            return acc + fold(ind, jnp.add)
        acc = lax.fori_loop(0, nkc, body, jnp.zeros((SUB, tq), jnp.float32))
        return jnp.sum(acc, axis=0, keepdims=True).astype(jnp.int32)

    k_eff = jnp.minimum(topk, tpos + 1)
    log_keff = jnp.log(k_eff.astype(jnp.float32))

    def unsettled(lo, hi, cnt_lo):
        return (cnt_lo != k_eff) & (hi > lo + 1)

    def search_pass(st, kind, bisect_only):
        lo, hi, cnt_lo, cnt_hi = st
        key_mid = (lo >> 1) + (hi >> 1) + (lo & hi & 1)
        if kind == "key_mid":
            cand = key_mid
        else:
            vlo, vhi = _key_float(lo), _key_float(hi)
            if kind == "interpolate":
                a = jnp.log(cnt_lo.astype(jnp.float32) + 0.5)
                b = jnp.log(cnt_hi.astype(jnp.float32) + 0.5)
                cand = _float_key(vlo + (vhi - vlo) * ((a - log_keff) / (a - b)))
            else:
                cand = _float_key(0.5 * vlo + 0.5 * vhi)
            cand = jnp.where(bisect_only, key_mid, cand)
        cand = jnp.clip(cand, lo + 1, hi - 1)
        cnt = count(lambda k, sp: k >= cand)
        upd = unsettled(lo, hi, cnt_lo)
        up = upd & (cnt >= k_eff)
        dn = upd & (cnt < k_eff)
        return (jnp.where(up, cand, lo), jnp.where(dn, cand, hi),
                jnp.where(up, cnt, cnt_lo), jnp.where(dn, cnt, cnt_hi))

    def search_cond(st):
        return jnp.max(unsettled(st[1], st[2], st[3]).astype(jnp.int32)) > 0

    def search_body(st):
        rounds, st = st[0], st[1:]
        bisect_only = rounds >= SEARCH_INTERP_ROUNDS
        for kind in SEARCH_ROUND:
            st = search_pass(st, kind, bisect_only)
        return (rounds + 1,) + st

    _, tau, hi, cnt_ge, cnt_gt = lax.while_loop(
        search_cond, search_body,
        (jnp.int32(0), kmin, kmax + 1, tpos + 1, jnp.zeros((1, tq), jnp.int32)))
    need = k_eff - cnt_gt
    overflow = cnt_ge > k_eff
    p_ref[...] = jnp.full((1, tq), s_len, jnp.int32)

    @pl.when(jnp.max(overflow.astype(jnp.int32)) > 0)
    def _():
        def pos_step(it, pos):
            cand = pos + jnp.left_shift(jnp.int32(1), (s_len.bit_length() - 1) - it)
            below = count(lambda k, sp: (k == tau) & (sp < cand))
            return jnp.where(below < need, cand, pos)
        pos = lax.fori_loop(0, s_len.bit_length(), pos_step, jnp.zeros((1, tq), jnp.int32))
        p_ref[...] = jnp.where(overflow, pos, s_len)

    last_eq = p_ref[...]

    def write_chunk(kc, carry):
        k0 = pl.multiple_of(kc * ck, ck)

        @pl.when(kc < nkc)
        def _():
            k = keys_ref[pl.ds(k0, ck), :]
            sel = (k > tau) | ((k == tau) & (k0 + srow <= last_eq))
            mask_ref[0, :, pl.ds(k0, ck)] = jnp.where(sel, 0.0, NEG).T.astype(mask_ref.dtype)

        @pl.when(kc >= nkc)
        def _():
            mask_ref[0, :, pl.ds(k0, ck)] = jnp.full((tq, ck), NEG, mask_ref.dtype)
        return carry

    lax.fori_loop(0, s_len // ck, write_chunk, 0)


def _indexer_mask(ik, iqt, iwt, topk):
    b, s, di = ik.shape
    ni = iqt.shape[1]
    tq, ck = 256, 512
    return pl.pallas_call(
        functools.partial(_indexer_kernel, tq=tq, ck=ck, topk=topk, s_len=s),
        grid=(b, s // tq),
        in_specs=[
            pl.BlockSpec((1, s, di), lambda bb, i: (bb, 0, 0)),
            pl.BlockSpec((1, ni, tq), lambda bb, i: (bb, 0, i)),
            pl.BlockSpec((1, IDX_HEADS, tq), lambda bb, i: (bb, 0, i)),
        ],
        out_specs=pl.BlockSpec((1, tq, s), lambda bb, i: (bb, i, 0)),
        out_shape=jax.ShapeDtypeStruct((b, s, s), jnp.bfloat16),
        scratch_shapes=[pltpu.VMEM((s, tq), jnp.int32), pltpu.VMEM((1, tq), jnp.int32)],
        compiler_params=_cparams("parallel", "parallel"),
        name="indexer_topk",
    )(ik, iqt, iwt)


def _flash_update(s_cols, m_ref, l_ref, rows):
    m_old = m_ref[rows, :]
    m_cur = functools.reduce(jnp.maximum, s_cols)
    m_new = jnp.maximum(m_old, jnp.max(m_cur, axis=-1, keepdims=True))
    alpha = jnp.exp2(m_old - m_new)
    ps = [jnp.exp2(sc - m_new) for sc in s_cols]
    l_ref[rows, :] = alpha * l_ref[rows, :] + functools.reduce(jnp.add, ps)
    m_ref[rows, :] = m_new
    return jnp.concatenate(ps, axis=1).astype(MXU_DTYPE), alpha


def _tile_is_near(i, j, tq, tk):
    return (j + 1) * (tk // LANE) - i * (tq // LANE) >= 0


def _dsa_attn_kernel(itab, jtab, q_ref, wuk_ref, ckvt_ref, ckv_ref, mask_ref, nb_ref, wuv_ref,
                     o_ref, qlat_ref, m_ref, l_ref, acc_ref, *, tq, tk, hc):
    pidx = pl.program_id(1)
    i = itab[pidx]
    j = jtab[pidx]
    jlast = (i * tq + tq - 1) // tk
    heads = A_HEADS
    nsub = tk // LANE
    nchunk = heads // hc
    crows = hc * tq

    @pl.when(j == 0)
    def _():
        for h in range(heads):
            qh = q_ref[0, :, h * A_QK_DIM:(h + 1) * A_QK_DIM]
            ql = _dot(qh, wuk_ref[h]) * (A_QK_DIM ** -0.5 * LOG2E)
            qlat_ref[h * tq:(h + 1) * tq, :] = ql.astype(qlat_ref.dtype)
        m_ref[...] = jnp.full(m_ref.shape, M_INIT, jnp.float32)
        l_ref[...] = jnp.zeros(l_ref.shape, jnp.float32)
        acc_ref[...] = jnp.zeros(acc_ref.shape, jnp.float32)

    def step(with_bias):
        ckvt = ckvt_ref[0]
        ckv = ckv_ref[0]
        maskf = mask_ref[0].astype(jnp.float32)
        mcols = [maskf[:, c * LANE:(c + 1) * LANE] for c in range(nsub)]
        bidx = [_bias_index(i, j, c, tq, tk) for c in range(nsub)] if with_bias else None
        s_next = _dot(qlat_ref[0:crows, :], ckvt)
        for ci in range(nchunk):
            s_all = s_next
            if ci + 1 < nchunk:
                s_next = _dot(qlat_ref[(ci + 1) * crows:(ci + 2) * crows, :], ckvt)
            ps, alphas = [], []
            for hh in range(hc):
                h = ci * hc + hh
                cols = []
                for c in range(nsub):
                    add = mcols[c] + nb_ref[bidx[c], h] if with_bias else mcols[c]
                    cols.append(s_all[hh * tq:(hh + 1) * tq, c * LANE:(c + 1) * LANE] + add)
                p, alpha = _flash_update(cols, m_ref, l_ref, slice(h * tq, (h + 1) * tq))
                ps.append(p)
                alphas.append(alpha)
            pv = _dot(jnp.concatenate(ps, axis=0), ckv)
            a = jnp.concatenate(alphas, axis=0)
            rows = slice(ci * crows, (ci + 1) * crows)
            for half in range(A_KV_RANK // LANE):
                cs = slice(half * LANE, (half + 1) * LANE)
                acc_ref[rows, cs] = a * acc_ref[rows, cs] + pv[:, cs]

    near = _tile_is_near(i, j, tq, tk)
    pl.when(near)(lambda: step(True))
    pl.when(jnp.logical_not(near))(lambda: step(False))

    @pl.when(j == jlast)
    def _():
        for h in range(heads):
            rows = slice(h * tq, (h + 1) * tq)
            l = jnp.sum(l_ref[rows, :], axis=-1, keepdims=True)
            oh = _dot((acc_ref[rows, :] / l).astype(MXU_DTYPE), wuv_ref[h])
            o_ref[0, :, h * A_V_DIM:(h + 1) * A_V_DIM] = oh.astype(o_ref.dtype)


def _dsa_attention(q, ckv, ckvt, mask, nb, w_uk, w_uv):
    b, s, nq = q.shape
    tq, tk, hc = 128, min(1024, s), 4
    itab, jtab = _causal_pairs(s, tq, tk)
    rows = A_HEADS * tq
    once = pl.Buffered(1)
    grid_spec = pltpu.PrefetchScalarGridSpec(
        num_scalar_prefetch=2,
        grid=(b, int(itab.shape[0])),
        in_specs=[
            pl.BlockSpec((1, tq, nq), lambda bb, p, it, jt: (bb, it[p], 0)),
            pl.BlockSpec(w_uk.shape, lambda bb, p, it, jt: (0, 0, 0), pipeline_mode=once),
            pl.BlockSpec((1, A_KV_RANK, tk), lambda bb, p, it, jt: (bb, 0, jt[p])),
            pl.BlockSpec((1, tk, A_KV_RANK), lambda bb, p, it, jt: (bb, jt[p], 0)),
            pl.BlockSpec((1, tq, tk), lambda bb, p, it, jt: (bb, it[p], jt[p])),
            pl.BlockSpec(nb.shape, lambda bb, p, it, jt: (0, 0, 0, 0), pipeline_mode=once),
            pl.BlockSpec(w_uv.shape, lambda bb, p, it, jt: (0, 0, 0), pipeline_mode=once),
        ],
        out_specs=pl.BlockSpec((1, tq, A_HEADS * A_V_DIM), lambda bb, p, it, jt: (bb, it[p], 0)),
        scratch_shapes=[
            pltpu.VMEM((rows, A_KV_RANK), MXU_DTYPE),
            pltpu.VMEM((rows, LANE), jnp.float32),
            pltpu.VMEM((rows, LANE), jnp.float32),
            pltpu.VMEM((rows, A_KV_RANK), jnp.float32),
        ],
    )
    return pl.pallas_call(
        functools.partial(_dsa_attn_kernel, tq=tq, tk=tk, hc=hc),
        grid_spec=grid_spec,
        out_shape=jax.ShapeDtypeStruct((b, s, A_HEADS * A_V_DIM), MXU_DTYPE),
        compiler_params=_cparams("parallel", "arbitrary"),
        name="dsa_attention",
    )(itab, jtab, q, w_uk, ckvt, ckv, mask, nb, w_uv)


def _oproj_ln_kernel(o_ref, x_ref, g_ref, w_ref, lng_ref, lnb_ref, out_ref, *, alpha):
    y = _dot(o_ref[0], w_ref[...])
    z = alpha * x_ref[0] + (1.0 + g_ref[...]) * y
    out_ref[0] = _layer_norm_rows(z, lng_ref[...], lnb_ref[...])


def _oproj_ln(o, x, mod, layer, w_o, ln_g, ln_b, alpha):
    b, s, d = x.shape
    tm = 512
    n_in = o.shape[-1]
    row = lambda bb, t: (bb, t, 0)
    return pl.pallas_call(
        functools.partial(_oproj_ln_kernel, alpha=alpha),
        grid=(b, s // tm),
        in_specs=[
            pl.BlockSpec((1, tm, n_in), row), pl.BlockSpec((1, tm, d), row),
            _mod_spec(layer, 2, d), _const_spec((n_in, d)),
            _const_spec((1, d)), _const_spec((1, d)),
        ],
        out_specs=pl.BlockSpec((1, tm, d), row),
        out_shape=jax.ShapeDtypeStruct((b, s, d), jnp.float32),
        compiler_params=_cparams("parallel", "parallel"),
        name="oproj_ln",
    )(o, x, mod, w_o.astype(MXU_DTYPE), ln_g.reshape(1, d), ln_b.reshape(1, d))


def _proj_diff_kernel(x_ref, sc_ref, sh_ref, wq_ref, wk_ref, wv_ref, q_ref, k_ref, v_ref):
    h = (x_ref[0] * (1.0 + sc_ref[...]) + sh_ref[...]).astype(MXU_DTYPE)
    q_ref[0] = (_dot(h, wq_ref[...]) * (B_HEAD_DIM ** -0.5 * LOG2E)).astype(q_ref.dtype)
    k_ref[0] = _dot(h, wk_ref[...]).astype(k_ref.dtype)
    v_ref[0] = _dot(h, wv_ref[...]).astype(v_ref.dtype)


def _proj_diff(x, mod, layer, w_in):
    b, s, d = x.shape
    tm = 512
    n = w_in.shape[1] // 3
    ws = [w_in[:, k * n:(k + 1) * n].astype(MXU_DTYPE) for k in range(3)]
    row = lambda bb, t: (bb, t, 0)
    return pl.pallas_call(
        _proj_diff_kernel,
        grid=(b, s // tm),
        in_specs=[pl.BlockSpec((1, tm, d), row), _mod_spec(layer, 1, d), _mod_spec(layer, 0, d)]
        + [_const_spec((d, n))] * 3,
        out_specs=[pl.BlockSpec((1, tm, n), row)] * 3,
        out_shape=[jax.ShapeDtypeStruct((b, s, n), MXU_DTYPE)] * 3,
        compiler_params=_cparams("parallel", "parallel"),
        name="proj_diff",
    )(x, mod, mod, *ws)


def _diff_attn_kernel(itab, jtab, q_ref, kt_ref, v_ref, nb_ref, lam_ref, g_ref, o_ref,
                      m_ref, l_ref, acc_ref, *, tq, tk, lam_init):
    pidx = pl.program_id(1)
    i = itab[pidx]
    j = jtab[pidx]
    jlast = (i * tq + tq - 1) // tk
    dv = 2 * B_HEAD_DIM

    @pl.when(j == 0)
    def _():
        m_ref[...] = jnp.full(m_ref.shape, M_INIT, jnp.float32)
        l_ref[...] = jnp.zeros(l_ref.shape, jnp.float32)
        acc_ref[...] = jnp.zeros(acc_ref.shape, jnp.float32)

    nsub = tk // LANE

    def step(with_bias):
        bidx = [_bias_index(i, j, c, tq, tk) for c in range(nsub)] if with_bias else None

        def logits(h):
            return [_dot(q_ref[0, 2 * h + mm], kt_ref[0, 2 * h + mm]) for mm in range(2)]

        s_next = logits(0)
        for h in range(B_HEADS):
            s_pair = s_next
            if h + 1 < B_HEADS:
                s_next = logits(h + 1)
            ps, alphas = [], []
            for mm in range(2):
                col = 2 * h + mm
                cols = [s_pair[mm][:, c * LANE:(c + 1) * LANE] for c in range(nsub)]
                if with_bias:
                    cols = [cols[c] + nb_ref[bidx[c], col] for c in range(nsub)]
                p, alpha = _flash_update(cols, m_ref, l_ref, slice(col * tq, (col + 1) * tq))
                ps.append(p)
                alphas.append(alpha)
            pv = _dot(jnp.concatenate(ps, axis=0), v_ref[0, h])
            rows = slice(2 * h * tq, (2 * h + 2) * tq)
            acc_ref[rows, :] = jnp.concatenate(alphas, axis=0) * acc_ref[rows, :] + pv

    near = _tile_is_near(i, j, tq, tk)
    pl.when(near)(lambda: step(True))
    pl.when(jnp.logical_not(near))(lambda: step(False))

    @pl.when(j == jlast)
    def _():
        lam = lam_ref[...]
        lam_full = (jnp.exp(jnp.sum(lam[0:1] * lam[1:2], axis=-1, keepdims=True))
                    - jnp.exp(jnp.sum(lam[2:3] * lam[3:4], axis=-1, keepdims=True)) + lam_init)
        for h in range(B_HEADS):
            r1 = slice(2 * h * tq, (2 * h + 1) * tq)
            r2 = slice((2 * h + 1) * tq, (2 * h + 2) * tq)
            l1 = jnp.sum(l_ref[r1, :], axis=-1, keepdims=True)
            l2 = jnp.sum(l_ref[r2, :], axis=-1, keepdims=True)
            o = acc_ref[r1, :] / l1 - lam_full * (acc_ref[r2, :] / l2)
            o = o * lax.rsqrt(jnp.mean(o * o, axis=-1, keepdims=True) + LN_EPS) * g_ref[...]
            o_ref[0, :, h * dv:(h + 1) * dv] = (o * (1.0 - lam_init)).astype(o_ref.dtype)


def _diff_attention(qm, ktm, vh, nb, lam, subln_g, lam_init):
    b, maps, s, dh = qm.shape
    tq, tk = 256, min(1024, s)
    dv = 2 * B_HEAD_DIM
    assert dv == LANE
    itab, jtab = _causal_pairs(s, tq, tk)
    once = pl.Buffered(1)
    grid_spec = pltpu.PrefetchScalarGridSpec(
        num_scalar_prefetch=2,
        grid=(b, int(itab.shape[0])),
        in_specs=[
            pl.BlockSpec((1, maps, tq, dh), lambda bb, p, it, jt: (bb, 0, it[p], 0)),
            pl.BlockSpec((1, maps, dh, tk), lambda bb, p, it, jt: (bb, 0, 0, jt[p])),
            pl.BlockSpec((1, B_HEADS, tk, dv), lambda bb, p, it, jt: (bb, 0, jt[p], 0)),
            pl.BlockSpec(nb.shape, lambda bb, p, it, jt: (0, 0, 0, 0), pipeline_mode=once),
            pl.BlockSpec(lam.shape, lambda bb, p, it, jt: (0, 0)),
            pl.BlockSpec((1, dv), lambda bb, p, it, jt: (0, 0)),
        ],
        out_specs=pl.BlockSpec((1, tq, B_HEADS * dv), lambda bb, p, it, jt: (bb, it[p], 0)),
        scratch_shapes=[
            pltpu.VMEM((maps * tq, LANE), jnp.float32),
            pltpu.VMEM((maps * tq, LANE), jnp.float32),
            pltpu.VMEM((maps * tq, dv), jnp.float32),
        ],
    )
    return pl.pallas_call(
        functools.partial(_diff_attn_kernel, tq=tq, tk=tk, lam_init=lam_init),
        grid_spec=grid_spec,
        out_shape=jax.ShapeDtypeStruct((b, s, B_HEADS * dv), MXU_DTYPE),
        compiler_params=_cparams("parallel", "arbitrary"),
        name="diff_attention",
    )(itab, jtab, qm, ktm, vh, nb, lam, subln_g.reshape(1, dv))


def _mlp_kernel(x_ref, sc_ref, sh_ref, g_ref, w1_ref, w2_ref, lng_ref, lnb_ref, out_ref, *, tf, alpha):
    x = x_ref[0]
    h = (x * (1.0 + sc_ref[...]) + sh_ref[...]).astype(MXU_DTYPE)
    y = jnp.zeros(x.shape, jnp.float32)
    for c in range(w1_ref.shape[1] // tf):
        a = jnp.maximum(_dot(h, w1_ref[:, c * tf:(c + 1) * tf]), 0.0)
        y = y + _dot((a * a).astype(MXU_DTYPE), w2_ref[c * tf:(c + 1) * tf, :])
    z = alpha * x + (1.0 + g_ref[...]) * y
    out_ref[0] = _layer_norm_rows(z, lng_ref[...], lnb_ref[...])


def _mlp(x, mod, layer, w1, w2, ln_g, ln_b, alpha):
    b, s, d = x.shape
    dff = w1.shape[1]
    tm, tf = 512, 1024
    row = lambda bb, t: (bb, t, 0)
    once = pl.Buffered(1)
    return pl.pallas_call(
        functools.partial(_mlp_kernel, tf=tf, alpha=alpha),
        grid=(b, s // tm),
        in_specs=[
            pl.BlockSpec((1, tm, d), row),
            _mod_spec(layer, 4, d), _mod_spec(layer, 3, d), _mod_spec(layer, 5, d),
            pl.BlockSpec((d, dff), lambda bb, t: (0, 0), pipeline_mode=once),
            pl.BlockSpec((dff, d), lambda bb, t: (0, 0), pipeline_mode=once),
            _const_spec((1, d)), _const_spec((1, d)),
        ],
        out_specs=pl.BlockSpec((1, tm, d), row),
        out_shape=jax.ShapeDtypeStruct((b, s, d), jnp.float32),
        compiler_params=_cparams("parallel", "parallel"),
        name="sqrelu_mlp",
    )(x, mod, mod, mod, w1.astype(MXU_DTYPE), w2.astype(MXU_DTYPE),
      ln_g.reshape(1, d), ln_b.reshape(1, d))


def kernel(x, c, rel_bias, ada_w, ada_b, ln_g, ln_b, a_w_in, a_kv_norm, a_w_uk, a_w_uv, a_w_o,
           b_w_in, b_lambda, b_subln, b_w_o, mlp_w1, mlp_w2):
    depth = ada_w.shape[0]
    b, s, d = x.shape
    alpha = (2 * depth) ** 0.25
    topk = min(TOPK_MAX, s // 4)
    mod = _ada_mod(c, ada_w, ada_b)
    nb_a = _bias_tiles(rel_bias, 128)
    nb_b = _bias_tiles(rel_bias, 256)
    for i in range(depth):
        jm = i // N_MIXERS
        if i % N_MIXERS == 0:
            q, ckv, iq, ikw = _proj_dsa(x, mod, i, a_w_in[jm], a_kv_norm[jm])
            ik = ikw[..., :IDX_DIM].astype(MXU_DTYPE)
            iwt = jnp.swapaxes(ikw[..., IDX_DIM:IDX_DIM + IDX_HEADS], 1, 2)
            mask = _indexer_mask(ik, jnp.swapaxes(iq, 1, 2), iwt, topk)
            o = _dsa_attention(q, ckv, jnp.swapaxes(ckv, 1, 2), mask, nb_a,
                               a_w_uk[jm].astype(MXU_DTYPE), a_w_uv[jm].astype(MXU_DTYPE))
            w_o = a_w_o[jm]
        else:
            lam_init = 0.8 - 0.6 * math.exp(-0.3 * i)
            qf, kf, vf = _proj_diff(x, mod, i, b_w_in[jm])
            qm = jnp.transpose(qf.reshape(b, s, B_MAPS, B_HEAD_DIM), (0, 2, 1, 3))
            ktm = jnp.transpose(kf.reshape(b, s, B_MAPS, B_HEAD_DIM), (0, 2, 3, 1))
            vh = jnp.transpose(vf.reshape(b, s, B_HEADS, 2 * B_HEAD_DIM), (0, 2, 1, 3))
            o = _diff_attention(qm, ktm, vh, nb_b, b_lambda[jm], b_subln[jm], lam_init)
            w_o = b_w_o[jm]
        x = _oproj_ln(o, x, mod, i, w_o, ln_g[i, 0], ln_b[i, 0], alpha)
        x = _mlp(x, mod, i, mlp_w1[i], mlp_w2[i], ln_g[i, 1], ln_b[i, 1], alpha)
    return x
```

```python
import functools
import math

import jax
import jax.numpy as jnp
from jax import lax
from jax.experimental import pallas as pl
from jax.experimental.pallas import tpu as pltpu

N_MIXERS = 2
A_HEADS = 16
A_QK_DIM = 64
A_V_DIM = 64
A_KV_RANK = 256
IDX_HEADS = 8
IDX_DIM = 64
TOPK_MAX = 256
B_HEADS = 8
B_HEAD_DIM = 64
B_MAPS = 2 * B_HEADS
REL_BUCKETS = 32
REL_MAX_DIST = 128
LN_EPS = 1e-5
NEG = -1e30
M_INIT = 0.5 * NEG
LOG2E = math.log2(math.e)
SEARCH_ROUND = ("interpolate", "value_mid", "interpolate", "key_mid")
SEARCH_OPENING = ("zero", "above_zero")
SEARCH_INTERP_ROUNDS = 6
FOLD_CHAINS = 8

LANE = 128
SUB = 8
VMEM_LIMIT = 56 * 1024 * 1024

MXU_DTYPE = jnp.bfloat16


def _dot(a, b):
    return jnp.dot(a, b, preferred_element_type=jnp.float32)


def _cparams(*sem):
    return pltpu.CompilerParams(dimension_semantics=sem, vmem_limit_bytes=VMEM_LIMIT)


def _const_spec(shape):
    nd = len(shape)
    return pl.BlockSpec(shape, lambda *_: (0,) * nd)


def _layer_norm_rows(z, g, b):
    mu = jnp.mean(z, axis=-1, keepdims=True)
    zc = z - mu
    var = jnp.mean(zc * zc, axis=-1, keepdims=True)
    return zc * lax.rsqrt(var + LN_EPS) * g + b


def _ada_kernel(ct_ref, w_ref, b_ref, o_ref, *, nb):
    ct = ct_ref[...]
    s = ct * jax.nn.sigmoid(ct)
    w = w_ref[0]
    rows = [jnp.sum(w * s[:, b:b + 1], axis=0, keepdims=True) for b in range(nb)]
    o_ref[0] = jnp.concatenate(rows, axis=0) + b_ref[0]


def _ada_mod(c, ada_w, ada_b):
    depth, d, n = ada_w.shape
    nb = c.shape[0]
    tn = 768
    out = pl.pallas_call(
        functools.partial(_ada_kernel, nb=nb),
        grid=(depth, n // tn),
        in_specs=[
            pl.BlockSpec((d, nb), lambda l, j: (0, 0)),
            pl.BlockSpec((1, d, tn), lambda l, j: (l, 0, j)),
            pl.BlockSpec((1, 1, tn), lambda l, j: (l, 0, j)),
        ],
        out_specs=pl.BlockSpec((1, nb, tn), lambda l, j: (l, 0, j)),
        out_shape=jax.ShapeDtypeStruct((depth, nb, n), jnp.float32),
        compiler_params=_cparams("arbitrary", "arbitrary"),
        name="ada_mod",
    )(c.T, ada_w, ada_b.reshape(depth, 1, n))
    return out.reshape(depth, nb, 6, 1, d)


def _mod_spec(layer, k, d):
    return pl.BlockSpec((None, None, None, 1, d), lambda b, *_: (layer, b, k, 0, 0))


def _rel_bucket(dist):
    n = jnp.maximum(dist, 0)
    max_exact = REL_BUCKETS // 2
    nf = jnp.maximum(n, 1).astype(jnp.float32)
    large = max_exact + (jnp.log(nf / max_exact) / math.log(REL_MAX_DIST / max_exact)
                         * (REL_BUCKETS - max_exact)).astype(jnp.int32)
    large = jnp.minimum(large, REL_BUCKETS - 1)
    return jnp.where(n < max_exact, n, large)


def _bias_tile_kernel(tab_ref, bk_ref, o_ref, *, heads):
    bk = bk_ref[0]
    for h in range(heads):
        far = tab_ref[REL_BUCKETS - 1, h]
        acc = jnp.zeros(bk.shape, jnp.float32)
        for b in range(REL_BUCKETS - 1):
            acc = jnp.where(bk == b, (tab_ref[b, h] - far) * LOG2E, acc)
        o_ref[0, h] = jnp.where(bk < 0, NEG, acc)


def _bias_tiles(rel_bias, tq):
    heads = rel_bias.shape[1]
    r_sub = tq // LANE
    rels = jnp.arange(-2, r_sub + 1, dtype=jnp.int32)
    ql = jnp.arange(tq, dtype=jnp.int32)[None, :, None]
    sl = jnp.arange(LANE, dtype=jnp.int32)[None, None, :]
    dist = ql - rels[:, None, None] * LANE - sl
    bk = jnp.where(dist < 0, -1, _rel_bucket(dist))
    nk = r_sub + 3
    return pl.pallas_call(
        functools.partial(_bias_tile_kernel, heads=heads),
        grid=(nk,),
        in_specs=[
            pl.BlockSpec(memory_space=pltpu.SMEM),
            pl.BlockSpec((1, tq, LANE), lambda k: (k, 0, 0)),
        ],
        out_specs=pl.BlockSpec((1, heads, tq, LANE), lambda k: (k, 0, 0, 0)),
        out_shape=jax.ShapeDtypeStruct((nk, heads, tq, LANE), jnp.float32),
        compiler_params=_cparams("arbitrary"),
        name="bias_tiles",
    )(rel_bias, bk)


def _bias_index(i, j, c, tq, tk):
    rel = j * (tk // LANE) + c - i * (tq // LANE)
    return jnp.clip(rel + 2, 0, tq // LANE + 2)


def _causal_pairs(s, tq, tk):
    ii, jj = [], []
    for i in range(s // tq):
        for j in range((i * tq + tq - 1) // tk + 1):
            ii.append(i)
            jj.append(j)
    return jnp.asarray(ii, jnp.int32), jnp.asarray(jj, jnp.int32)


def _proj_dsa_kernel(x_ref, sc_ref, sh_ref, wq_ref, wc_ref, wi_ref, wk_ref, kvg_ref,
                     q_ref, ckv_ref, iq_ref, ikw_ref):
    h = (x_ref[0] * (1.0 + sc_ref[...]) + sh_ref[...]).astype(MXU_DTYPE)
    q_ref[0] = _dot(h, wq_ref[...]).astype(q_ref.dtype)
    ckv = _dot(h, wc_ref[...])
    ckv = ckv * lax.rsqrt(jnp.mean(ckv * ckv, axis=-1, keepdims=True) + LN_EPS) * kvg_ref[...]
    ckv_ref[0] = ckv.astype(ckv_ref.dtype)
    iq_ref[0] = _dot(h, wi_ref[...]).astype(iq_ref.dtype)
    ikw_ref[0] = _dot(h, wk_ref[...])


def _proj_dsa(x, mod, layer, w_in, kv_norm):
    b, s, d = x.shape
    tm = 512
    nq = A_HEADS * A_QK_DIM
    ni = IDX_HEADS * IDX_DIM
    o1, o2, o3 = nq, nq + A_KV_RANK, nq + A_KV_RANK + ni
    wq = w_in[:, :o1].astype(MXU_DTYPE)
    wc = w_in[:, o1:o2].astype(MXU_DTYPE)
    wi = w_in[:, o2:o3].astype(MXU_DTYPE)
    wk = jnp.pad(w_in[:, o3:], ((0, 0), (0, LANE - (w_in.shape[1] - o3)))).astype(MXU_DTYPE)
    row = lambda bb, t: (bb, t, 0)
    return pl.pallas_call(
        _proj_dsa_kernel,
        grid=(b, s // tm),
        in_specs=[
            pl.BlockSpec((1, tm, d), row),
            _mod_spec(layer, 1, d), _mod_spec(layer, 0, d),
            _const_spec((d, nq)), _const_spec((d, A_KV_RANK)), _const_spec((d, ni)),
            _const_spec((d, LANE)), _const_spec((1, A_KV_RANK)),
        ],
        out_specs=[
            pl.BlockSpec((1, tm, nq), row), pl.BlockSpec((1, tm, A_KV_RANK), row),
            pl.BlockSpec((1, tm, ni), row), pl.BlockSpec((1, tm, LANE), row),
        ],
        out_shape=[
            jax.ShapeDtypeStruct((b, s, nq), MXU_DTYPE),
            jax.ShapeDtypeStruct((b, s, A_KV_RANK), MXU_DTYPE),
            jax.ShapeDtypeStruct((b, s, ni), MXU_DTYPE),
            jax.ShapeDtypeStruct((b, s, LANE), jnp.float32),
        ],
        compiler_params=_cparams("parallel", "parallel"),
        name="proj_dsa",
    )(x, mod, mod, wq, wc, wi, wk, kv_norm.reshape(1, A_KV_RANK))


def _float_key(v):
    bits = pltpu.bitcast(v, jnp.int32)
    return bits ^ ((bits >> 31) & 0x7FFFFFFF)


def _key_float(k):
    return pltpu.bitcast(k ^ ((k >> 31) & 0x7FFFFFFF), jnp.float32)


def _indexer_kernel(ik_ref, iqt_ref, iwt_ref, mask_ref, keys_ref, p_ref, *, tq, ck, topk, s_len):
    i = pl.program_id(1)
    q0 = i * tq
    nkc = (q0 + tq + ck - 1) // ck
    w = iwt_ref[0] * (IDX_HEADS ** -0.5 * IDX_DIM ** -0.5)
    tpos = q0 + lax.broadcasted_iota(jnp.int32, (1, tq), 1)
    srow = lax.broadcasted_iota(jnp.int32, (ck, 1), 0)
    int_max = jnp.int32(2 ** 31 - 1)
    int_min = jnp.int32(-2 ** 31)

    def fold(x, op):
        slabs = [x[r * SUB:(r + 1) * SUB] for r in range(ck // SUB)]
        lanes = FOLD_CHAINS
        chains = slabs[:lanes]
        for r, slab in enumerate(slabs[lanes:]):
            chains[r % lanes] = op(chains[r % lanes], slab)
        slabs = chains
        while len(slabs) > 1:
            slabs = [op(a, b) for a, b in zip(slabs[0::2], slabs[1::2])]
        return slabs[0]

    def score_chunk(kc, carry):
        kmax, kmin = carry
        k0 = pl.multiple_of(kc * ck, ck)
        ik = ik_ref[0, pl.ds(k0, ck), :]
        sc = jnp.zeros((ck, tq), jnp.float32)
        for h in range(IDX_HEADS):
            lg = _dot(ik, iqt_ref[0, h * IDX_DIM:(h + 1) * IDX_DIM, :])
            sc = sc + jnp.maximum(lg, 0.0) * w[h:h + 1, :]
        causal = k0 + srow <= tpos
        key = _float_key(jnp.where(causal, sc, -jnp.inf))
        keys_ref[pl.ds(k0, ck), :] = key
        return (jnp.maximum(kmax, fold(key, jnp.maximum)),
                jnp.minimum(kmin, fold(jnp.where(causal, key, int_max), jnp.minimum)))

    kmax, kmin = lax.fori_loop(0, nkc, score_chunk, (jnp.full((SUB, tq), int_min, jnp.int32),
                                                      jnp.full((SUB, tq), int_max, jnp.int32)))
    kmax = jnp.max(kmax, axis=0, keepdims=True)
    kmin = jnp.min(kmin, axis=0, keepdims=True)

    def count(pred):
        def body(kc, accs):
            k0 = pl.multiple_of(kc * ck, ck)
            accs = list(accs)
            chunk = keys_ref.at[pl.ds(k0, ck), :]
            for r in range(ck // SUB):
                k = chunk[r * SUB:(r + 1) * SUB, :]
                hit = pred(k, k0 + srow[r * SUB:(r + 1) * SUB]).astype(jnp.float32)
                accs[r % FOLD_CHAINS] = accs[r % FOLD_CHAINS] + hit
            return tuple(accs)
        accs = lax.fori_loop(0, nkc, body, (jnp.zeros((SUB, tq), jnp.float32),) * FOLD_CHAINS)
        acc = functools.reduce(jnp.add, accs)
        return jnp.sum(acc, axis=0, keepdims=True).astype(jnp.int32)

    k_eff = jnp.minimum(topk, tpos + 1)
    log_keff = jnp.log(k_eff.astype(jnp.float32))

    def unsettled(lo, hi, cnt_lo):
        return (cnt_lo != k_eff) & (hi > lo + 1)

    def search_pass(st, kind, bisect_only):
        lo, hi, cnt_lo, cnt_hi = st
        key_mid = (lo >> 1) + (hi >> 1) + (lo & hi & 1)
        if kind == "key_mid":
            cand = key_mid
        elif kind in ("zero", "above_zero"):
            cand = jnp.full((1, tq), 0 if kind == "zero" else 1, jnp.int32)
        else:
            vlo, vhi = _key_float(lo), _key_float(hi)
            if kind == "interpolate":
                a = jnp.log(cnt_lo.astype(jnp.float32) + 0.5)
                b = jnp.log(cnt_hi.astype(jnp.float32) + 0.5)
                cand = _float_key(vlo + (vhi - vlo) * ((a - log_keff) / (a - b)))
            else:
                cand = _float_key(0.5 * vlo + 0.5 * vhi)
            cand = jnp.where(bisect_only, key_mid, cand)
        cand = jnp.clip(cand, lo + 1, hi - 1)
        cnt = count(lambda k, sp: k >= cand)
        upd = unsettled(lo, hi, cnt_lo)
        up = upd & (cnt >= k_eff)
        dn = upd & (cnt < k_eff)
        return (jnp.where(up, cand, lo), jnp.where(dn, cand, hi),
                jnp.where(up, cnt, cnt_lo), jnp.where(dn, cnt, cnt_hi))

    def search_cond(st):
        return jnp.max(unsettled(st[1], st[2], st[3]).astype(jnp.int32)) > 0

    def search_body(st):
        rounds, st = st[0], st[1:]
        bisect_only = rounds >= SEARCH_INTERP_ROUNDS
        for kind in SEARCH_ROUND:
            st = search_pass(st, kind, bisect_only)
        return (rounds + 1,) + st

    st = (kmin, kmax + 1, tpos + 1, jnp.zeros((1, tq), jnp.int32))
    for kind in SEARCH_OPENING:
        st = search_pass(st, kind, False)
    _, tau, hi, cnt_ge, cnt_gt = lax.while_loop(search_cond, search_body, (jnp.int32(0),) + st)
    need = k_eff - cnt_gt
    overflow = cnt_ge > k_eff
    p_ref[...] = jnp.full((1, tq), s_len, jnp.int32)

    @pl.when(jnp.max(overflow.astype(jnp.int32)) > 0)
    def _():
        def pos_step(it, pos):
            cand = pos + jnp.left_shift(jnp.int32(1), (s_len.bit_length() - 1) - it)
            below = count(lambda k, sp: (k == tau) & (sp < cand))
            return jnp.where(below < need, cand, pos)
        pos = lax.fori_loop(0, s_len.bit_length(), pos_step, jnp.zeros((1, tq), jnp.int32))
        p_ref[...] = jnp.where(overflow, pos, s_len)

    last_eq = p_ref[...]

    def write_chunk(kc, carry):
        k0 = pl.multiple_of(kc * ck, ck)

        @pl.when(kc < nkc)
        def _():
            k = keys_ref[pl.ds(k0, ck), :]
            sel = (k > tau) | ((k == tau) & (k0 + srow <= last_eq))
            mask_ref[0, :, pl.ds(k0, ck)] = jnp.where(sel, 0.0, NEG).T.astype(mask_ref.dtype)

        @pl.when(kc >= nkc)
        def _():
            mask_ref[0, :, pl.ds(k0, ck)] = jnp.full((tq, ck), NEG, mask_ref.dtype)
        return carry

    lax.fori_loop(0, s_len // ck, write_chunk, 0)


def _indexer_mask(ik, iqt, iwt, topk):
    b, s, di = ik.shape
    ni = iqt.shape[1]
    tq, ck = 256, 512
    return pl.pallas_call(
        functools.partial(_indexer_kernel, tq=tq, ck=ck, topk=topk, s_len=s),
        grid=(b, s // tq),
        in_specs=[
            pl.BlockSpec((1, s, di), lambda bb, i: (bb, 0, 0)),
            pl.BlockSpec((1, ni, tq), lambda bb, i: (bb, 0, i)),
            pl.BlockSpec((1, IDX_HEADS, tq), lambda bb, i: (bb, 0, i)),
        ],
        out_specs=pl.BlockSpec((1, tq, s), lambda bb, i: (bb, i, 0)),
        out_shape=jax.ShapeDtypeStruct((b, s, s), jnp.bfloat16),
        scratch_shapes=[pltpu.VMEM((s, tq), jnp.int32), pltpu.VMEM((1, tq), jnp.int32)],
        compiler_params=_cparams("parallel", "parallel"),
        name="indexer_topk",
    )(ik, iqt, iwt)


def _flash_update(s_cols, m_ref, l_ref, rows):
    m_old = m_ref[rows, :]
    m_cur = functools.reduce(jnp.maximum, s_cols)
    m_new = jnp.maximum(m_old, jnp.max(m_cur, axis=-1, keepdims=True))
    alpha = jnp.exp2(m_old - m_new)
    ps = [jnp.exp2(sc - m_new) for sc in s_cols]
    l_ref[rows, :] = alpha * l_ref[rows, :] + functools.reduce(jnp.add, ps)
    m_ref[rows, :] = m_new
    return jnp.concatenate(ps, axis=1).astype(MXU_DTYPE), alpha


def _tile_is_near(i, j, tq, tk):
    return (j + 1) * (tk // LANE) - i * (tq // LANE) >= 0


def _dsa_attn_kernel(itab, jtab, q_ref, wuk_ref, ckvt_ref, ckv_ref, mask_ref, nb_ref, wuv_ref,
                     o_ref, qlat_ref, m_ref, l_ref, acc_ref, *, tq, tk, hc):
    pidx = pl.program_id(1)
    i = itab[pidx]
    j = jtab[pidx]
    jlast = (i * tq + tq - 1) // tk
    heads = A_HEADS
    nsub = tk // LANE
    nchunk = heads // hc
    crows = hc * tq

    @pl.when(j == 0)
    def _():
        for h in range(heads):
            qh = q_ref[0, :, h * A_QK_DIM:(h + 1) * A_QK_DIM]
            ql = _dot(qh, wuk_ref[h]) * (A_QK_DIM ** -0.5 * LOG2E)
            qlat_ref[h * tq:(h + 1) * tq, :] = ql.astype(qlat_ref.dtype)
        m_ref[...] = jnp.full(m_ref.shape, M_INIT, jnp.float32)
        l_ref[...] = jnp.zeros(l_ref.shape, jnp.float32)
        acc_ref[...] = jnp.zeros(acc_ref.shape, jnp.float32)

    def step(with_bias):
        ckvt = ckvt_ref[0]
        ckv = ckv_ref[0]
        maskf = mask_ref[0].astype(jnp.float32)
        mcols = [maskf[:, c * LANE:(c + 1) * LANE] for c in range(nsub)]
        bidx = [_bias_index(i, j, c, tq, tk) for c in range(nsub)] if with_bias else None
        s_next = _dot(qlat_ref[0:crows, :], ckvt)
        for ci in range(nchunk):
            s_all = s_next
            if ci + 1 < nchunk:
                s_next = _dot(qlat_ref[(ci + 1) * crows:(ci + 2) * crows, :], ckvt)
            ps, alphas = [], []
            for hh in range(hc):
                h = ci * hc + hh
                cols = []
                for c in range(nsub):
                    add = mcols[c] + nb_ref[bidx[c], h] if with_bias else mcols[c]
                    cols.append(s_all[hh * tq:(hh + 1) * tq, c * LANE:(c + 1) * LANE] + add)
                p, alpha = _flash_update(cols, m_ref, l_ref, slice(h * tq, (h + 1) * tq))
                ps.append(p)
                alphas.append(alpha)
            pv = _dot(jnp.concatenate(ps, axis=0), ckv)
            a = jnp.concatenate(alphas, axis=0)
            rows = slice(ci * crows, (ci + 1) * crows)
            for half in range(A_KV_RANK // LANE):
                cs = slice(half * LANE, (half + 1) * LANE)
                acc_ref[rows, cs] = a * acc_ref[rows, cs] + pv[:, cs]

    near = _tile_is_near(i, j, tq, tk)
    pl.when(near)(lambda: step(True))
    pl.when(jnp.logical_not(near))(lambda: step(False))

    @pl.when(j == jlast)
    def _():
        for h in range(heads):
            rows = slice(h * tq, (h + 1) * tq)
            l = jnp.sum(l_ref[rows, :], axis=-1, keepdims=True)
            oh = _dot((acc_ref[rows, :] / l).astype(MXU_DTYPE), wuv_ref[h])
            o_ref[0, :, h * A_V_DIM:(h + 1) * A_V_DIM] = oh.astype(o_ref.dtype)


def _dsa_attention(q, ckv, ckvt, mask, nb, w_uk, w_uv):
    b, s, nq = q.shape
    tq, tk, hc = 128, min(1024, s), 4
    itab, jtab = _causal_pairs(s, tq, tk)
    rows = A_HEADS * tq
    once = pl.Buffered(1)
    grid_spec = pltpu.PrefetchScalarGridSpec(
        num_scalar_prefetch=2,
        grid=(b, int(itab.shape[0])),
        in_specs=[
            pl.BlockSpec((1, tq, nq), lambda bb, p, it, jt: (bb, it[p], 0)),
            pl.BlockSpec(w_uk.shape, lambda bb, p, it, jt: (0, 0, 0), pipeline_mode=once),
            pl.BlockSpec((1, A_KV_RANK, tk), lambda bb, p, it, jt: (bb, 0, jt[p])),
            pl.BlockSpec((1, tk, A_KV_RANK), lambda bb, p, it, jt: (bb, jt[p], 0)),
            pl.BlockSpec((1, tq, tk), lambda bb, p, it, jt: (bb, it[p], jt[p])),
            pl.BlockSpec(nb.shape, lambda bb, p, it, jt: (0, 0, 0, 0), pipeline_mode=once),
            pl.BlockSpec(w_uv.shape, lambda bb, p, it, jt: (0, 0, 0), pipeline_mode=once),
        ],
        out_specs=pl.BlockSpec((1, tq, A_HEADS * A_V_DIM), lambda bb, p, it, jt: (bb, it[p], 0)),
        scratch_shapes=[
            pltpu.VMEM((rows, A_KV_RANK), MXU_DTYPE),
            pltpu.VMEM((rows, LANE), jnp.float32),
            pltpu.VMEM((rows, LANE), jnp.float32),
            pltpu.VMEM((rows, A_KV_RANK), jnp.float32),
        ],
    )
    return pl.pallas_call(
        functools.partial(_dsa_attn_kernel, tq=tq, tk=tk, hc=hc),
        grid_spec=grid_spec,
        out_shape=jax.ShapeDtypeStruct((b, s, A_HEADS * A_V_DIM), MXU_DTYPE),
        compiler_params=_cparams("parallel", "arbitrary"),
        name="dsa_attention",
    )(itab, jtab, q, w_uk, ckvt, ckv, mask, nb, w_uv)


def _oproj_ln_kernel(o_ref, x_ref, g_ref, w_ref, lng_ref, lnb_ref, out_ref, *, alpha):
    y = _dot(o_ref[0], w_ref[...])
    z = alpha * x_ref[0] + (1.0 + g_ref[...]) * y
    out_ref[0] = _layer_norm_rows(z, lng_ref[...], lnb_ref[...])


def _oproj_ln(o, x, mod, layer, w_o, ln_g, ln_b, alpha):
    b, s, d = x.shape
    tm = 512
    n_in = o.shape[-1]
    row = lambda bb, t: (bb, t, 0)
    return pl.pallas_call(
        functools.partial(_oproj_ln_kernel, alpha=alpha),
        grid=(b, s // tm),
        in_specs=[
            pl.BlockSpec((1, tm, n_in), row), pl.BlockSpec((1, tm, d), row),
            _mod_spec(layer, 2, d), _const_spec((n_in, d)),
            _const_spec((1, d)), _const_spec((1, d)),
        ],
        out_specs=pl.BlockSpec((1, tm, d), row),
        out_shape=jax.ShapeDtypeStruct((b, s, d), jnp.float32),
        compiler_params=_cparams("parallel", "parallel"),
        name="oproj_ln",
    )(o, x, mod, w_o.astype(MXU_DTYPE), ln_g.reshape(1, d), ln_b.reshape(1, d))


def _proj_diff_kernel(x_ref, sc_ref, sh_ref, wq_ref, wk_ref, wv_ref, q_ref, k_ref, v_ref):
    h = (x_ref[0] * (1.0 + sc_ref[...]) + sh_ref[...]).astype(MXU_DTYPE)
    q_ref[0] = (_dot(h, wq_ref[...]) * (B_HEAD_DIM ** -0.5 * LOG2E)).astype(q_ref.dtype)
    k_ref[0] = _dot(h, wk_ref[...]).astype(k_ref.dtype)
    v_ref[0] = _dot(h, wv_ref[...]).astype(v_ref.dtype)


def _proj_diff(x, mod, layer, w_in):
    b, s, d = x.shape
    tm = 512
    n = w_in.shape[1] // 3
    ws = [w_in[:, k * n:(k + 1) * n].astype(MXU_DTYPE) for k in range(3)]
    row = lambda bb, t: (bb, t, 0)
    return pl.pallas_call(
        _proj_diff_kernel,
        grid=(b, s // tm),
        in_specs=[pl.BlockSpec((1, tm, d), row), _mod_spec(layer, 1, d), _mod_spec(layer, 0, d)]
        + [_const_spec((d, n))] * 3,
        out_specs=[pl.BlockSpec((1, tm, n), row)] * 3,
        out_shape=[jax.ShapeDtypeStruct((b, s, n), MXU_DTYPE)] * 3,
        compiler_params=_cparams("parallel", "parallel"),
        name="proj_diff",
    )(x, mod, mod, *ws)


def _diff_attn_kernel(itab, jtab, q_ref, kt_ref, v_ref, nb_ref, lam_ref, g_ref, o_ref,
                      m_ref, l_ref, acc_ref, *, tq, tk, lam_init):
    pidx = pl.program_id(1)
    i = itab[pidx]
    j = jtab[pidx]
    jlast = (i * tq + tq - 1) // tk
    dv = 2 * B_HEAD_DIM

    @pl.when(j == 0)
    def _():
        m_ref[...] = jnp.full(m_ref.shape, M_INIT, jnp.float32)
        l_ref[...] = jnp.zeros(l_ref.shape, jnp.float32)
        acc_ref[...] = jnp.zeros(acc_ref.shape, jnp.float32)

    nsub = tk // LANE

    def step(with_bias):
        bidx = [_bias_index(i, j, c, tq, tk) for c in range(nsub)] if with_bias else None

        def logits(h):
            return [_dot(q_ref[0, 2 * h + mm], kt_ref[0, 2 * h + mm]) for mm in range(2)]

        s_next = logits(0)
        for h in range(B_HEADS):
            s_pair = s_next
            if h + 1 < B_HEADS:
                s_next = logits(h + 1)
            ps, alphas = [], []
            for mm in range(2):
                col = 2 * h + mm
                cols = [s_pair[mm][:, c * LANE:(c + 1) * LANE] for c in range(nsub)]
                if with_bias:
                    cols = [cols[c] + nb_ref[bidx[c], col] for c in range(nsub)]
                p, alpha = _flash_update(cols, m_ref, l_ref, slice(col * tq, (col + 1) * tq))
                ps.append(p)
                alphas.append(alpha)
            pv = _dot(jnp.concatenate(ps, axis=0), v_ref[0, h])
            rows = slice(2 * h * tq, (2 * h + 2) * tq)
            acc_ref[rows, :] = jnp.concatenate(alphas, axis=0) * acc_ref[rows, :] + pv

    near = _tile_is_near(i, j, tq, tk)
    pl.when(near)(lambda: step(True))
    pl.when(jnp.logical_not(near))(lambda: step(False))

    @pl.when(j == jlast)
    def _():
        lam = lam_ref[...]
        lam_full = (jnp.exp(jnp.sum(lam[0:1] * lam[1:2], axis=-1, keepdims=True))
                    - jnp.exp(jnp.sum(lam[2:3] * lam[3:4], axis=-1, keepdims=True)) + lam_init)
        for h in range(B_HEADS):
            r1 = slice(2 * h * tq, (2 * h + 1) * tq)
            r2 = slice((2 * h + 1) * tq, (2 * h + 2) * tq)
            l1 = jnp.sum(l_ref[r1, :], axis=-1, keepdims=True)
            l2 = jnp.sum(l_ref[r2, :], axis=-1, keepdims=True)
            o = acc_ref[r1, :] / l1 - lam_full * (acc_ref[r2, :] / l2)
            o = o * lax.rsqrt(jnp.mean(o * o, axis=-1, keepdims=True) + LN_EPS) * g_ref[...]
            o_ref[0, :, h * dv:(h + 1) * dv] = (o * (1.0 - lam_init)).astype(o_ref.dtype)


def _diff_attention(qm, ktm, vh, nb, lam, subln_g, lam_init):
    b, maps, s, dh = qm.shape
    tq, tk = 256, min(1024, s)
    dv = 2 * B_HEAD_DIM
    assert dv == LANE
    itab, jtab = _causal_pairs(s, tq, tk)
    once = pl.Buffered(1)
    grid_spec = pltpu.PrefetchScalarGridSpec(
        num_scalar_prefetch=2,
        grid=(b, int(itab.shape[0])),
        in_specs=[
            pl.BlockSpec((1, maps, tq, dh), lambda bb, p, it, jt: (bb, 0, it[p], 0)),
            pl.BlockSpec((1, maps, dh, tk), lambda bb, p, it, jt: (bb, 0, 0, jt[p])),
            pl.BlockSpec((1, B_HEADS, tk, dv), lambda bb, p, it, jt: (bb, 0, jt[p], 0)),
            pl.BlockSpec(nb.shape, lambda bb, p, it, jt: (0, 0, 0, 0), pipeline_mode=once),
            pl.BlockSpec(lam.shape, lambda bb, p, it, jt: (0, 0)),
            pl.BlockSpec((1, dv), lambda bb, p, it, jt: (0, 0)),
        ],
        out_specs=pl.BlockSpec((1, tq, B_HEADS * dv), lambda bb, p, it, jt: (bb, it[p], 0)),
        scratch_shapes=[
            pltpu.VMEM((maps * tq, LANE), jnp.float32),
            pltpu.VMEM((maps * tq, LANE), jnp.float32),
            pltpu.VMEM((maps * tq, dv), jnp.float32),
        ],
    )
    return pl.pallas_call(
        functools.partial(_diff_attn_kernel, tq=tq, tk=tk, lam_init=lam_init),
        grid_spec=grid_spec,
        out_shape=jax.ShapeDtypeStruct((b, s, B_HEADS * dv), MXU_DTYPE),
        compiler_params=_cparams("parallel", "arbitrary"),
        name="diff_attention",
    )(itab, jtab, qm, ktm, vh, nb, lam, subln_g.reshape(1, dv))


def _mlp_kernel(x_ref, sc_ref, sh_ref, g_ref, w1_ref, w2_ref, lng_ref, lnb_ref, out_ref, *, tf, alpha):
    x = x_ref[0]
    h = (x * (1.0 + sc_ref[...]) + sh_ref[...]).astype(MXU_DTYPE)
    y = jnp.zeros(x.shape, jnp.float32)
    for c in range(w1_ref.shape[1] // tf):
        a = jnp.maximum(_dot(h, w1_ref[:, c * tf:(c + 1) * tf]), 0.0)
        y = y + _dot((a * a).astype(MXU_DTYPE), w2_ref[c * tf:(c + 1) * tf, :])
    z = alpha * x + (1.0 + g_ref[...]) * y
    out_ref[0] = _layer_norm_rows(z, lng_ref[...], lnb_ref[...])


def _mlp(x, mod, layer, w1, w2, ln_g, ln_b, alpha):
    b, s, d = x.shape
    dff = w1.shape[1]
    tm, tf = 512, 1024
    row = lambda bb, t: (bb, t, 0)
    once = pl.Buffered(1)
    return pl.pallas_call(
        functools.partial(_mlp_kernel, tf=tf, alpha=alpha),
        grid=(b, s // tm),
        in_specs=[
            pl.BlockSpec((1, tm, d), row),
            _mod_spec(layer, 4, d), _mod_spec(layer, 3, d), _mod_spec(layer, 5, d),
            pl.BlockSpec((d, dff), lambda bb, t: (0, 0), pipeline_mode=once),
            pl.BlockSpec((dff, d), lambda bb, t: (0, 0), pipeline_mode=once),
            _const_spec((1, d)), _const_spec((1, d)),
        ],
        out_specs=pl.BlockSpec((1, tm, d), row),
        out_shape=jax.ShapeDtypeStruct((b, s, d), jnp.float32),
        compiler_params=_cparams("parallel", "parallel"),
        name="sqrelu_mlp",
    )(x, mod, mod, mod, w1.astype(MXU_DTYPE), w2.astype(MXU_DTYPE),
      ln_g.reshape(1, d), ln_b.reshape(1, d))


def kernel(x, c, rel_bias, ada_w, ada_b, ln_g, ln_b, a_w_in, a_kv_norm, a_w_uk, a_w_uv, a_w_o,
           b_w_in, b_lambda, b_subln, b_w_o, mlp_w1, mlp_w2):
    depth = ada_w.shape[0]
    b, s, d = x.shape
    alpha = (2 * depth) ** 0.25
    topk = min(TOPK_MAX, s // 4)
    mod = _ada_mod(c, ada_w, ada_b)
    nb_a = _bias_tiles(rel_bias, 128)
    nb_b = _bias_tiles(rel_bias, 256)
    for i in range(depth):
        jm = i // N_MIXERS
        if i % N_MIXERS == 0:
            q, ckv, iq, ikw = _proj_dsa(x, mod, i, a_w_in[jm], a_kv_norm[jm])
            ik = ikw[..., :IDX_DIM].astype(MXU_DTYPE)
            iwt = jnp.swapaxes(ikw[..., IDX_DIM:IDX_DIM + IDX_HEADS], 1, 2)
            mask = _indexer_mask(ik, jnp.swapaxes(iq, 1, 2), iwt, topk)
            o = _dsa_attention(q, ckv, jnp.swapaxes(ckv, 1, 2), mask, nb_a,
                               a_w_uk[jm].astype(MXU_DTYPE), a_w_uv[jm].astype(MXU_DTYPE))
            w_o = a_w_o[jm]
        else:
            lam_init = 0.8 - 0.6 * math.exp(-0.3 * i)
            qf, kf, vf = _proj_diff(x, mod, i, b_w_in[jm])
            qm = jnp.transpose(qf.reshape(b, s, B_MAPS, B_HEAD_DIM), (0, 2, 1, 3))
            ktm = jnp.transpose(kf.reshape(b, s, B_MAPS, B_HEAD_DIM), (0, 2, 3, 1))
            vh = jnp.transpose(vf.reshape(b, s, B_HEADS, 2 * B_HEAD_DIM), (0, 2, 1, 3))
            o = _diff_attention(qm, ktm, vh, nb_b, b_lambda[jm], b_subln[jm], lam_init)
            w_o = b_w_o[jm]
        x = _oproj_ln(o, x, mod, i, w_o, ln_g[i, 0], ln_b[i, 0], alpha)
        x = _mlp(x, mod, i, mlp_w1[i], mlp_w2[i], ln_g[i, 1], ln_b[i, 1], alpha)
    return x
```

```python
import functools
import math

import jax
import jax.numpy as jnp
from jax import lax
from jax.experimental import pallas as pl
from jax.experimental.pallas import tpu as pltpu

N_MIXERS = 2
A_HEADS = 16
A_QK_DIM = 64
A_V_DIM = 64
A_KV_RANK = 256
IDX_HEADS = 8
IDX_DIM = 64
TOPK_MAX = 256
B_HEADS = 8
B_HEAD_DIM = 64
B_MAPS = 2 * B_HEADS
REL_BUCKETS = 32
REL_MAX_DIST = 128
LN_EPS = 1e-5
NEG = -1e30
M_INIT = 0.5 * NEG
LOG2E = math.log2(math.e)
SEARCH_ROUND = ("interpolate", "value_mid", "interpolate", "key_mid")
SEARCH_OPENING = ("zero", "above_zero")
SEARCH_INTERP_ROUNDS = 6
FOLD_CHAINS = 8

LANE = 128
SUB = 8
VMEM_LIMIT = 56 * 1024 * 1024
DSA_TQ = 256
DIFF_TQ = 256

MXU_DTYPE = jnp.bfloat16


def _dot(a, b):
    return jnp.dot(a, b, preferred_element_type=jnp.float32)


def _cparams(*sem):
    return pltpu.CompilerParams(dimension_semantics=sem, vmem_limit_bytes=VMEM_LIMIT)


def _const_spec(shape):
    nd = len(shape)
    return pl.BlockSpec(shape, lambda *_: (0,) * nd)


def _layer_norm_rows(z, g, b):
    mu = jnp.mean(z, axis=-1, keepdims=True)
    zc = z - mu
    var = jnp.mean(zc * zc, axis=-1, keepdims=True)
    return zc * lax.rsqrt(var + LN_EPS) * g + b


def _ada_kernel(ct_ref, w_ref, b_ref, o_ref, *, nb):
    ct = ct_ref[...]
    s = ct * jax.nn.sigmoid(ct)
    w = w_ref[0]
    rows = [jnp.sum(w * s[:, b:b + 1], axis=0, keepdims=True) for b in range(nb)]
    o_ref[0] = jnp.concatenate(rows, axis=0) + b_ref[0]


def _ada_mod(c, ada_w, ada_b):
    depth, d, n = ada_w.shape
    nb = c.shape[0]
    tn = 768
    out = pl.pallas_call(
        functools.partial(_ada_kernel, nb=nb),
        grid=(depth, n // tn),
        in_specs=[
            pl.BlockSpec((d, nb), lambda l, j: (0, 0)),
            pl.BlockSpec((1, d, tn), lambda l, j: (l, 0, j)),
            pl.BlockSpec((1, 1, tn), lambda l, j: (l, 0, j)),
        ],
        out_specs=pl.BlockSpec((1, nb, tn), lambda l, j: (l, 0, j)),
        out_shape=jax.ShapeDtypeStruct((depth, nb, n), jnp.float32),
        compiler_params=_cparams("arbitrary", "arbitrary"),
        name="ada_mod",
    )(c.T, ada_w, ada_b.reshape(depth, 1, n))
    return out.reshape(depth, nb, 6, 1, d)


def _mod_spec(layer, k, d):
    return pl.BlockSpec((None, None, None, 1, d), lambda b, *_: (layer, b, k, 0, 0))


def _rel_bucket(dist):
    n = jnp.maximum(dist, 0)
    max_exact = REL_BUCKETS // 2
    nf = jnp.maximum(n, 1).astype(jnp.float32)
    large = max_exact + (jnp.log(nf / max_exact) / math.log(REL_MAX_DIST / max_exact)
                         * (REL_BUCKETS - max_exact)).astype(jnp.int32)
    large = jnp.minimum(large, REL_BUCKETS - 1)
    return jnp.where(n < max_exact, n, large)


def _bias_tile_kernel(tab_ref, bk_ref, o_ref, *, heads):
    bk = bk_ref[0]
    for h in range(heads):
        far = tab_ref[REL_BUCKETS - 1, h]
        acc = jnp.zeros(bk.shape, jnp.float32)
        for b in range(REL_BUCKETS - 1):
            acc = jnp.where(bk == b, (tab_ref[b, h] - far) * LOG2E, acc)
        o_ref[0, h] = jnp.where(bk < 0, NEG, acc)


def _bias_tiles(rel_bias, tq):
    heads = rel_bias.shape[1]
    r_sub = tq // LANE
    rels = jnp.arange(-2, r_sub + 1, dtype=jnp.int32)
    ql = jnp.arange(tq, dtype=jnp.int32)[None, :, None]
    sl = jnp.arange(LANE, dtype=jnp.int32)[None, None, :]
    dist = ql - rels[:, None, None] * LANE - sl
    bk = jnp.where(dist < 0, -1, _rel_bucket(dist))
    nk = r_sub + 3
    return pl.pallas_call(
        functools.partial(_bias_tile_kernel, heads=heads),
        grid=(nk,),
        in_specs=[
            pl.BlockSpec(memory_space=pltpu.SMEM),
            pl.BlockSpec((1, tq, LANE), lambda k: (k, 0, 0)),
        ],
        out_specs=pl.BlockSpec((1, heads, tq, LANE), lambda k: (k, 0, 0, 0)),
        out_shape=jax.ShapeDtypeStruct((nk, heads, tq, LANE), jnp.float32),
        compiler_params=_cparams("arbitrary"),
        name="bias_tiles",
    )(rel_bias, bk)


def _bias_index(i, j, c, tq, tk):
    rel = j * (tk // LANE) + c - i * (tq // LANE)
    return jnp.clip(rel + 2, 0, tq // LANE + 2)


def _causal_pairs(s, tq, tk):
    ii, jj = [], []
    for i in range(s // tq):
        for j in range((i * tq + tq - 1) // tk + 1):
            ii.append(i)
            jj.append(j)
    return jnp.asarray(ii, jnp.int32), jnp.asarray(jj, jnp.int32)


def _proj_dsa_kernel(x_ref, sc_ref, sh_ref, wq_ref, wc_ref, wi_ref, wk_ref, kvg_ref,
                     q_ref, ckv_ref, iq_ref, ikw_ref):
    h = (x_ref[0] * (1.0 + sc_ref[...]) + sh_ref[...]).astype(MXU_DTYPE)
    q_ref[0] = _dot(h, wq_ref[...]).astype(q_ref.dtype)
    ckv = _dot(h, wc_ref[...])
    ckv = ckv * lax.rsqrt(jnp.mean(ckv * ckv, axis=-1, keepdims=True) + LN_EPS) * kvg_ref[...]
    ckv_ref[0] = ckv.astype(ckv_ref.dtype)
    iq_ref[0] = _dot(h, wi_ref[...]).astype(iq_ref.dtype)
    ikw_ref[0] = _dot(h, wk_ref[...])


def _proj_dsa(x, mod, layer, w_in, kv_norm):
    b, s, d = x.shape
    tm = 512
    nq = A_HEADS * A_QK_DIM
    ni = IDX_HEADS * IDX_DIM
    o1, o2, o3 = nq, nq + A_KV_RANK, nq + A_KV_RANK + ni
    wq = w_in[:, :o1].astype(MXU_DTYPE)
    wc = w_in[:, o1:o2].astype(MXU_DTYPE)
    wi = w_in[:, o2:o3].astype(MXU_DTYPE)
    wk = jnp.pad(w_in[:, o3:], ((0, 0), (0, LANE - (w_in.shape[1] - o3)))).astype(MXU_DTYPE)
    row = lambda bb, t: (bb, t, 0)
    return pl.pallas_call(
        _proj_dsa_kernel,
        grid=(b, s // tm),
        in_specs=[
            pl.BlockSpec((1, tm, d), row),
            _mod_spec(layer, 1, d), _mod_spec(layer, 0, d),
            _const_spec((d, nq)), _const_spec((d, A_KV_RANK)), _const_spec((d, ni)),
            _const_spec((d, LANE)), _const_spec((1, A_KV_RANK)),
        ],
        out_specs=[
            pl.BlockSpec((1, tm, nq), row), pl.BlockSpec((1, tm, A_KV_RANK), row),
            pl.BlockSpec((1, tm, ni), row), pl.BlockSpec((1, tm, LANE), row),
        ],
        out_shape=[
            jax.ShapeDtypeStruct((b, s, nq), MXU_DTYPE),
            jax.ShapeDtypeStruct((b, s, A_KV_RANK), MXU_DTYPE),
            jax.ShapeDtypeStruct((b, s, ni), MXU_DTYPE),
            jax.ShapeDtypeStruct((b, s, LANE), jnp.float32),
        ],
        compiler_params=_cparams("parallel", "parallel"),
        name="proj_dsa",
    )(x, mod, mod, wq, wc, wi, wk, kv_norm.reshape(1, A_KV_RANK))


def _float_key(v):
    bits = pltpu.bitcast(v, jnp.int32)
    return bits ^ ((bits >> 31) & 0x7FFFFFFF)


def _key_float(k):
    return pltpu.bitcast(k ^ ((k >> 31) & 0x7FFFFFFF), jnp.float32)


def _indexer_kernel(ik_ref, iqt_ref, iwt_ref, mask_ref, keys_ref, p_ref, *, tq, ck, topk, s_len):
    i = pl.program_id(1)
    q0 = i * tq
    nkc = (q0 + tq + ck - 1) // ck
    w = iwt_ref[0] * (IDX_HEADS ** -0.5 * IDX_DIM ** -0.5)
    tpos = q0 + lax.broadcasted_iota(jnp.int32, (1, tq), 1)
    srow = lax.broadcasted_iota(jnp.int32, (ck, 1), 0)
    int_max = jnp.int32(2 ** 31 - 1)
    int_min = jnp.int32(-2 ** 31)

    def fold(x, op):
        slabs = [x[r * SUB:(r + 1) * SUB] for r in range(ck // SUB)]
        lanes = FOLD_CHAINS
        chains = slabs[:lanes]
        for r, slab in enumerate(slabs[lanes:]):
            chains[r % lanes] = op(chains[r % lanes], slab)
        slabs = chains
        while len(slabs) > 1:
            slabs = [op(a, b) for a, b in zip(slabs[0::2], slabs[1::2])]
        return slabs[0]

    def score_chunk(kc, carry):
        kmax, kmin = carry
        k0 = pl.multiple_of(kc * ck, ck)
        ik = ik_ref[0, pl.ds(k0, ck), :]
        sc = jnp.zeros((ck, tq), jnp.float32)
        for h in range(IDX_HEADS):
            lg = _dot(ik, iqt_ref[0, h * IDX_DIM:(h + 1) * IDX_DIM, :])
            sc = sc + jnp.maximum(lg, 0.0) * w[h:h + 1, :]
        causal = k0 + srow <= tpos
        key = _float_key(jnp.where(causal, sc, -jnp.inf))
        keys_ref[pl.ds(k0, ck), :] = key
        return (jnp.maximum(kmax, fold(key, jnp.maximum)),
                jnp.minimum(kmin, fold(jnp.where(causal, key, int_max), jnp.minimum)))

    kmax, kmin = lax.fori_loop(0, nkc, score_chunk, (jnp.full((SUB, tq), int_min, jnp.int32),
                                                      jnp.full((SUB, tq), int_max, jnp.int32)))
    kmax = jnp.max(kmax, axis=0, keepdims=True)
    kmin = jnp.min(kmin, axis=0, keepdims=True)

    def count(pred):
        def body(kc, accs):
            k0 = pl.multiple_of(kc * ck, ck)
            accs = list(accs)
            chunk = keys_ref.at[pl.ds(k0, ck), :]
            for r in range(ck // SUB):
                k = chunk[r * SUB:(r + 1) * SUB, :]
                hit = pred(k, k0 + srow[r * SUB:(r + 1) * SUB]).astype(jnp.float32)
                accs[r % FOLD_CHAINS] = accs[r % FOLD_CHAINS] + hit
            return tuple(accs)
        accs = lax.fori_loop(0, nkc, body, (jnp.zeros((SUB, tq), jnp.float32),) * FOLD_CHAINS)
        acc = functools.reduce(jnp.add, accs)
        return jnp.sum(acc, axis=0, keepdims=True).astype(jnp.int32)

    k_eff = jnp.minimum(topk, tpos + 1)
    log_keff = jnp.log(k_eff.astype(jnp.float32))

    def unsettled(lo, hi, cnt_lo):
        return (cnt_lo != k_eff) & (hi > lo + 1)

    def search_pass(st, kind, bisect_only):
        lo, hi, cnt_lo, cnt_hi = st
        key_mid = (lo >> 1) + (hi >> 1) + (lo & hi & 1)
        if kind == "key_mid":
            cand = key_mid
        elif kind in ("zero", "above_zero"):
            cand = jnp.full((1, tq), 0 if kind == "zero" else 1, jnp.int32)
        else:
            vlo, vhi = _key_float(lo), _key_float(hi)
            if kind == "interpolate":
                a = jnp.log(cnt_lo.astype(jnp.float32) + 0.5)
                b = jnp.log(cnt_hi.astype(jnp.float32) + 0.5)
                cand = _float_key(vlo + (vhi - vlo) * ((a - log_keff) / (a - b)))
            else:
                cand = _float_key(0.5 * vlo + 0.5 * vhi)
            cand = jnp.where(bisect_only, key_mid, cand)
        cand = jnp.clip(cand, lo + 1, hi - 1)
        cnt = count(lambda k, sp: k >= cand)
        upd = unsettled(lo, hi, cnt_lo)
        up = upd & (cnt >= k_eff)
        dn = upd & (cnt < k_eff)
        return (jnp.where(up, cand, lo), jnp.where(dn, cand, hi),
                jnp.where(up, cnt, cnt_lo), jnp.where(dn, cnt, cnt_hi))

    def search_cond(st):
        return jnp.max(unsettled(st[1], st[2], st[3]).astype(jnp.int32)) > 0

    def search_body(st):
        rounds, st = st[0], st[1:]
        bisect_only = rounds >= SEARCH_INTERP_ROUNDS
        for kind in SEARCH_ROUND:
            st = search_pass(st, kind, bisect_only)
        return (rounds + 1,) + st

    st = (kmin, kmax + 1, tpos + 1, jnp.zeros((1, tq), jnp.int32))
    for kind in SEARCH_OPENING:
        st = search_pass(st, kind, False)
    _, tau, hi, cnt_ge, cnt_gt = lax.while_loop(search_cond, search_body, (jnp.int32(0),) + st)
    need = k_eff - cnt_gt
    overflow = cnt_ge > k_eff
    p_ref[...] = jnp.full((1, tq), s_len, jnp.int32)

    @pl.when(jnp.max(overflow.astype(jnp.int32)) > 0)
    def _():
        def pos_step(it, pos):
            cand = pos + jnp.left_shift(jnp.int32(1), (s_len.bit_length() - 1) - it)
            below = count(lambda k, sp: (k == tau) & (sp < cand))
            return jnp.where(below < need, cand, pos)
        pos = lax.fori_loop(0, s_len.bit_length(), pos_step, jnp.zeros((1, tq), jnp.int32))
        p_ref[...] = jnp.where(overflow, pos, s_len)

    last_eq = p_ref[...]

    def write_chunk(kc, carry):
        k0 = pl.multiple_of(kc * ck, ck)

        @pl.when(kc < nkc)
        def _():
            k = keys_ref[pl.ds(k0, ck), :]
            sel = (k > tau) | ((k == tau) & (k0 + srow <= last_eq))
            mask_ref[0, :, pl.ds(k0, ck)] = jnp.where(sel, 0.0, NEG).T.astype(mask_ref.dtype)

        @pl.when(kc >= nkc)
        def _():
            mask_ref[0, :, pl.ds(k0, ck)] = jnp.full((tq, ck), NEG, mask_ref.dtype)
        return carry

    lax.fori_loop(0, s_len // ck, write_chunk, 0)


def _indexer_mask(ik, iqt, iwt, topk):
    b, s, di = ik.shape
    ni = iqt.shape[1]
    tq, ck = 256, 512
    return pl.pallas_call(
        functools.partial(_indexer_kernel, tq=tq, ck=ck, topk=topk, s_len=s),
        grid=(b, s // tq),
        in_specs=[
            pl.BlockSpec((1, s, di), lambda bb, i: (bb, 0, 0)),
            pl.BlockSpec((1, ni, tq), lambda bb, i: (bb, 0, i)),
            pl.BlockSpec((1, IDX_HEADS, tq), lambda bb, i: (bb, 0, i)),
        ],
        out_specs=pl.BlockSpec((1, tq, s), lambda bb, i: (bb, i, 0)),
        out_shape=jax.ShapeDtypeStruct((b, s, s), jnp.bfloat16),
        scratch_shapes=[pltpu.VMEM((s, tq), jnp.int32), pltpu.VMEM((1, tq), jnp.int32)],
        compiler_params=_cparams("parallel", "parallel"),
        name="indexer_topk",
    )(ik, iqt, iwt)


def _flash_update(s_cols, m_ref, l_ref, rows):
    m_old = m_ref[rows, :]
    m_cur = functools.reduce(jnp.maximum, s_cols)
    m_new = jnp.maximum(m_old, jnp.max(m_cur, axis=-1, keepdims=True))
    alpha = jnp.exp2(m_old - m_new)
    ps = [jnp.exp2(sc - m_new) for sc in s_cols]
    l_ref[rows, :] = alpha * l_ref[rows, :] + functools.reduce(jnp.add, ps)
    m_ref[rows, :] = m_new
    return jnp.concatenate(ps, axis=1).astype(MXU_DTYPE), alpha


def _tile_is_near(i, j, tq, tk):
    return (j + 1) * (tk // LANE) - i * (tq // LANE) >= 0


def _dsa_attn_kernel(itab, jtab, q_ref, wuk_ref, ckvt_ref, ckv_ref, mask_ref, nb_ref, wuv_ref,
                     o_ref, qlat_ref, m_ref, l_ref, acc_ref, *, tq, tk, hc):
    pidx = pl.program_id(1)
    i = itab[pidx]
    j = jtab[pidx]
    jlast = (i * tq + tq - 1) // tk
    heads = A_HEADS
    nsub = tk // LANE
    nchunk = heads // hc
    crows = hc * tq

    @pl.when(j == 0)
    def _():
        for h in range(heads):
            qh = q_ref[0, :, h * A_QK_DIM:(h + 1) * A_QK_DIM]
            ql = _dot(qh, wuk_ref[h]) * (A_QK_DIM ** -0.5 * LOG2E)
            qlat_ref[h * tq:(h + 1) * tq, :] = ql.astype(qlat_ref.dtype)
        m_ref[...] = jnp.full(m_ref.shape, M_INIT, jnp.float32)
        l_ref[...] = jnp.zeros(l_ref.shape, jnp.float32)
        acc_ref[...] = jnp.zeros(acc_ref.shape, jnp.float32)

    def step(with_bias):
        ckvt = ckvt_ref[0]
        ckv = ckv_ref[0]
        maskf = mask_ref[0].astype(jnp.float32)
        mcols = [maskf[:, c * LANE:(c + 1) * LANE] for c in range(nsub)]
        bidx = [_bias_index(i, j, c, tq, tk) for c in range(nsub)] if with_bias else None
        s_next = _dot(qlat_ref[0:crows, :], ckvt)
        for ci in range(nchunk):
            s_all = s_next
            if ci + 1 < nchunk:
                s_next = _dot(qlat_ref[(ci + 1) * crows:(ci + 2) * crows, :], ckvt)
            ps, alphas = [], []
            for hh in range(hc):
                h = ci * hc + hh
                cols = []
                for c in range(nsub):
                    add = mcols[c] + nb_ref[bidx[c], h] if with_bias else mcols[c]
                    cols.append(s_all[hh * tq:(hh + 1) * tq, c * LANE:(c + 1) * LANE] + add)
                p, alpha = _flash_update(cols, m_ref, l_ref, slice(h * tq, (h + 1) * tq))
                ps.append(p)
                alphas.append(alpha)
            pv = _dot(jnp.concatenate(ps, axis=0), ckv)
            a = jnp.concatenate(alphas, axis=0)
            rows = slice(ci * crows, (ci + 1) * crows)
            for half in range(A_KV_RANK // LANE):
                cs = slice(half * LANE, (half + 1) * LANE)
                acc_ref[rows, cs] = a * acc_ref[rows, cs] + pv[:, cs]

    near = _tile_is_near(i, j, tq, tk)
    pl.when(near)(lambda: step(True))
    pl.when(jnp.logical_not(near))(lambda: step(False))

    @pl.when(j == jlast)
    def _():
        for h in range(heads):
            rows = slice(h * tq, (h + 1) * tq)
            l = jnp.sum(l_ref[rows, :], axis=-1, keepdims=True)
            oh = _dot((acc_ref[rows, :] / l).astype(MXU_DTYPE), wuv_ref[h])
            o_ref[0, :, h * A_V_DIM:(h + 1) * A_V_DIM] = oh.astype(o_ref.dtype)


def _dsa_attention(q, ckv, ckvt, mask, nb, w_uk, w_uv):
    b, s, nq = q.shape
    tq, tk, hc = DSA_TQ, min(1024, s), 2
    itab, jtab = _causal_pairs(s, tq, tk)
    rows = A_HEADS * tq
    once = pl.Buffered(1)
    grid_spec = pltpu.PrefetchScalarGridSpec(
        num_scalar_prefetch=2,
        grid=(b, int(itab.shape[0])),
        in_specs=[
            pl.BlockSpec((1, tq, nq), lambda bb, p, it, jt: (bb, it[p], 0)),
            pl.BlockSpec(w_uk.shape, lambda bb, p, it, jt: (0, 0, 0), pipeline_mode=once),
            pl.BlockSpec((1, A_KV_RANK, tk), lambda bb, p, it, jt: (bb, 0, jt[p])),
            pl.BlockSpec((1, tk, A_KV_RANK), lambda bb, p, it, jt: (bb, jt[p], 0)),
            pl.BlockSpec((1, tq, tk), lambda bb, p, it, jt: (bb, it[p], jt[p])),
            pl.BlockSpec(nb.shape, lambda bb, p, it, jt: (0, 0, 0, 0), pipeline_mode=once),
            pl.BlockSpec(w_uv.shape, lambda bb, p, it, jt: (0, 0, 0), pipeline_mode=once),
        ],
        out_specs=pl.BlockSpec((1, tq, A_HEADS * A_V_DIM), lambda bb, p, it, jt: (bb, it[p], 0)),
        scratch_shapes=[
            pltpu.VMEM((rows, A_KV_RANK), MXU_DTYPE),
            pltpu.VMEM((rows, LANE), jnp.float32),
            pltpu.VMEM((rows, LANE), jnp.float32),
            pltpu.VMEM((rows, A_KV_RANK), jnp.float32),
        ],
    )
    return pl.pallas_call(
        functools.partial(_dsa_attn_kernel, tq=tq, tk=tk, hc=hc),
        grid_spec=grid_spec,
        out_shape=jax.ShapeDtypeStruct((b, s, A_HEADS * A_V_DIM), MXU_DTYPE),
        compiler_params=_cparams("parallel", "arbitrary"),
        name="dsa_attention",
    )(itab, jtab, q, w_uk, ckvt, ckv, mask, nb, w_uv)


def _proj_diff_kernel(x_ref, sc_ref, sh_ref, wq_ref, wk_ref, wv_ref, q_ref, k_ref, v_ref):
    h = (x_ref[0] * (1.0 + sc_ref[...]) + sh_ref[...]).astype(MXU_DTYPE)
    q_ref[0] = (_dot(h, wq_ref[...]) * (B_HEAD_DIM ** -0.5 * LOG2E)).astype(q_ref.dtype)
    k_ref[0] = _dot(h, wk_ref[...]).astype(k_ref.dtype)
    v_ref[0] = _dot(h, wv_ref[...]).astype(v_ref.dtype)


def _proj_diff(x, mod, layer, w_in):
    b, s, d = x.shape
    tm = 512
    n = w_in.shape[1] // 3
    ws = [w_in[:, k * n:(k + 1) * n].astype(MXU_DTYPE) for k in range(3)]
    row = lambda bb, t: (bb, t, 0)
    return pl.pallas_call(
        _proj_diff_kernel,
        grid=(b, s // tm),
        in_specs=[pl.BlockSpec((1, tm, d), row), _mod_spec(layer, 1, d), _mod_spec(layer, 0, d)]
        + [_const_spec((d, n))] * 3,
        out_specs=[pl.BlockSpec((1, tm, n), row)] * 3,
        out_shape=[jax.ShapeDtypeStruct((b, s, n), MXU_DTYPE)] * 3,
        compiler_params=_cparams("parallel", "parallel"),
        name="proj_diff",
    )(x, mod, mod, *ws)


def _diff_attn_kernel(itab, jtab, q_ref, kt_ref, v_ref, nb_ref, lam_ref, g_ref, o_ref,
                      m_ref, l_ref, acc_ref, *, tq, tk, lam_init):
    pidx = pl.program_id(1)
    i = itab[pidx]
    j = jtab[pidx]
    jlast = (i * tq + tq - 1) // tk
    dv = 2 * B_HEAD_DIM

    @pl.when(j == 0)
    def _():
        m_ref[...] = jnp.full(m_ref.shape, M_INIT, jnp.float32)
        l_ref[...] = jnp.zeros(l_ref.shape, jnp.float32)
        acc_ref[...] = jnp.zeros(acc_ref.shape, jnp.float32)

    nsub = tk // LANE

    def step(with_bias):
        bidx = [_bias_index(i, j, c, tq, tk) for c in range(nsub)] if with_bias else None

        def logits(h):
            return [_dot(q_ref[0, 2 * h + mm], kt_ref[0, 2 * h + mm]) for mm in range(2)]

        s_next = logits(0)
        for h in range(B_HEADS):
            s_pair = s_next
            if h + 1 < B_HEADS:
                s_next = logits(h + 1)
            ps, alphas = [], []
            for mm in range(2):
                col = 2 * h + mm
                cols = [s_pair[mm][:, c * LANE:(c + 1) * LANE] for c in range(nsub)]
                if with_bias:
                    cols = [cols[c] + nb_ref[bidx[c], col] for c in range(nsub)]
                p, alpha = _flash_update(cols, m_ref, l_ref, slice(col * tq, (col + 1) * tq))
                ps.append(p)
                alphas.append(alpha)
            pv = _dot(jnp.concatenate(ps, axis=0), v_ref[0, h])
            rows = slice(2 * h * tq, (2 * h + 2) * tq)
            acc_ref[rows, :] = jnp.concatenate(alphas, axis=0) * acc_ref[rows, :] + pv

    near = _tile_is_near(i, j, tq, tk)
    pl.when(near)(lambda: step(True))
    pl.when(jnp.logical_not(near))(lambda: step(False))

    @pl.when(j == jlast)
    def _():
        lam = lam_ref[...]
        lam_full = (jnp.exp(jnp.sum(lam[0:1] * lam[1:2], axis=-1, keepdims=True))
                    - jnp.exp(jnp.sum(lam[2:3] * lam[3:4], axis=-1, keepdims=True)) + lam_init)
        for h in range(B_HEADS):
            r1 = slice(2 * h * tq, (2 * h + 1) * tq)
            r2 = slice((2 * h + 1) * tq, (2 * h + 2) * tq)
            l1 = jnp.sum(l_ref[r1, :], axis=-1, keepdims=True)
            l2 = jnp.sum(l_ref[r2, :], axis=-1, keepdims=True)
            o = acc_ref[r1, :] / l1 - lam_full * (acc_ref[r2, :] / l2)
            o = o * lax.rsqrt(jnp.mean(o * o, axis=-1, keepdims=True) + LN_EPS) * g_ref[...]
            o_ref[0, :, h * dv:(h + 1) * dv] = (o * (1.0 - lam_init)).astype(o_ref.dtype)


def _diff_attention(qm, ktm, vh, nb, lam, subln_g, lam_init):
    b, maps, s, dh = qm.shape
    tq, tk = DIFF_TQ, min(1024, s)
    dv = 2 * B_HEAD_DIM
    assert dv == LANE
    itab, jtab = _causal_pairs(s, tq, tk)
    once = pl.Buffered(1)
    grid_spec = pltpu.PrefetchScalarGridSpec(
        num_scalar_prefetch=2,
        grid=(b, int(itab.shape[0])),
        in_specs=[
            pl.BlockSpec((1, maps, tq, dh), lambda bb, p, it, jt: (bb, 0, it[p], 0)),
            pl.BlockSpec((1, maps, dh, tk), lambda bb, p, it, jt: (bb, 0, 0, jt[p])),
            pl.BlockSpec((1, B_HEADS, tk, dv), lambda bb, p, it, jt: (bb, 0, jt[p], 0)),
            pl.BlockSpec(nb.shape, lambda bb, p, it, jt: (0, 0, 0, 0), pipeline_mode=once),
            pl.BlockSpec(lam.shape, lambda bb, p, it, jt: (0, 0)),
            pl.BlockSpec((1, dv), lambda bb, p, it, jt: (0, 0)),
        ],
        out_specs=pl.BlockSpec((1, tq, B_HEADS * dv), lambda bb, p, it, jt: (bb, it[p], 0)),
        scratch_shapes=[
            pltpu.VMEM((maps * tq, LANE), jnp.float32),
            pltpu.VMEM((maps * tq, LANE), jnp.float32),
            pltpu.VMEM((maps * tq, dv), jnp.float32),
        ],
    )
    return pl.pallas_call(
        functools.partial(_diff_attn_kernel, tq=tq, tk=tk, lam_init=lam_init),
        grid_spec=grid_spec,
        out_shape=jax.ShapeDtypeStruct((b, s, B_HEADS * dv), MXU_DTYPE),
        compiler_params=_cparams("parallel", "arbitrary"),
        name="diff_attention",
    )(itab, jtab, qm, ktm, vh, nb, lam, subln_g.reshape(1, dv))


def _post_mixer_kernel(o_ref, x_ref, gt_ref, sc_ref, sh_ref, gc_ref, wo_ref, w1_ref, w2_ref,
                       lng_ref, lnb_ref, out_ref, *, tf, alpha):
    y = _dot(o_ref[0], wo_ref[...])
    x = _layer_norm_rows(alpha * x_ref[0] + (1.0 + gt_ref[...]) * y, lng_ref[0:1], lnb_ref[0:1])
    h = (x * (1.0 + sc_ref[...]) + sh_ref[...]).astype(MXU_DTYPE)
    y = jnp.zeros(x.shape, jnp.float32)
    for c in range(w1_ref.shape[1] // tf):
        a = jnp.maximum(_dot(h, w1_ref[:, c * tf:(c + 1) * tf]), 0.0)
        y = y + _dot((a * a).astype(MXU_DTYPE), w2_ref[c * tf:(c + 1) * tf, :])
    z = alpha * x + (1.0 + gc_ref[...]) * y
    out_ref[0] = _layer_norm_rows(z, lng_ref[1:2], lnb_ref[1:2])


def _post_mixer(o, x, mod, layer, w_o, w1, w2, ln_g, ln_b, alpha):
    b, s, d = x.shape
    dff = w1.shape[1]
    n_in = o.shape[-1]
    tm, tf = 512, 1024
    row = lambda bb, t: (bb, t, 0)
    once = pl.Buffered(1)
    return pl.pallas_call(
        functools.partial(_post_mixer_kernel, tf=tf, alpha=alpha),
        grid=(b, s // tm),
        in_specs=[
            pl.BlockSpec((1, tm, n_in), row), pl.BlockSpec((1, tm, d), row),
            _mod_spec(layer, 2, d), _mod_spec(layer, 4, d), _mod_spec(layer, 3, d),
            _mod_spec(layer, 5, d),
            pl.BlockSpec((n_in, d), lambda bb, t: (0, 0), pipeline_mode=once),
            pl.BlockSpec((d, dff), lambda bb, t: (0, 0), pipeline_mode=once),
            pl.BlockSpec((dff, d), lambda bb, t: (0, 0), pipeline_mode=once),
            _const_spec((2, d)), _const_spec((2, d)),
        ],
        out_specs=pl.BlockSpec((1, tm, d), row),
        out_shape=jax.ShapeDtypeStruct((b, s, d), jnp.float32),
        compiler_params=_cparams("parallel", "parallel"),
        name="post_mixer",
    )(o, x, mod, mod, mod, mod, w_o.astype(MXU_DTYPE), w1.astype(MXU_DTYPE),
      w2.astype(MXU_DTYPE), ln_g, ln_b)


def kernel(x, c, rel_bias, ada_w, ada_b, ln_g, ln_b, a_w_in, a_kv_norm, a_w_uk, a_w_uv, a_w_o,
           b_w_in, b_lambda, b_subln, b_w_o, mlp_w1, mlp_w2):
    depth = ada_w.shape[0]
    b, s, d = x.shape
    alpha = (2 * depth) ** 0.25
    topk = min(TOPK_MAX, s // 4)
    mod = _ada_mod(c, ada_w, ada_b)
    nb = {tq: _bias_tiles(rel_bias, tq) for tq in {DSA_TQ, DIFF_TQ}}
    nb_a, nb_b = nb[DSA_TQ], nb[DIFF_TQ]
    for i in range(depth):
        jm = i // N_MIXERS
        if i % N_MIXERS == 0:
            q, ckv, iq, ikw = _proj_dsa(x, mod, i, a_w_in[jm], a_kv_norm[jm])
            ik = ikw[..., :IDX_DIM].astype(MXU_DTYPE)
            iwt = jnp.swapaxes(ikw[..., IDX_DIM:IDX_DIM + IDX_HEADS], 1, 2)
            mask = _indexer_mask(ik, jnp.swapaxes(iq, 1, 2), iwt, topk)
            o = _dsa_attention(q, ckv, jnp.swapaxes(ckv, 1, 2), mask, nb_a,
                               a_w_uk[jm].astype(MXU_DTYPE), a_w_uv[jm].astype(MXU_DTYPE))
            w_o = a_w_o[jm]
        else:
            lam_init = 0.8 - 0.6 * math.exp(-0.3 * i)
            qf, kf, vf = _proj_diff(x, mod, i, b_w_in[jm])
            qm = jnp.transpose(qf.reshape(b, s, B_MAPS, B_HEAD_DIM), (0, 2, 1, 3))
            ktm = jnp.transpose(kf.reshape(b, s, B_MAPS, B_HEAD_DIM), (0, 2, 3, 1))
            vh = jnp.transpose(vf.reshape(b, s, B_HEADS, 2 * B_HEAD_DIM), (0, 2, 1, 3))
            o = _diff_attention(qm, ktm, vh, nb_b, b_lambda[jm], b_subln[jm], lam_init)
            w_o = b_w_o[jm]
        x = _post_mixer(o, x, mod, i, w_o, mlp_w1[i], mlp_w2[i], ln_g[i], ln_b[i], alpha)
    return x
```

```python
import functools
import math

import jax
import jax.numpy as jnp
from jax import lax
from jax.experimental import pallas as pl
from jax.experimental.pallas import tpu as pltpu

N_MIXERS = 2
A_HEADS = 16
A_QK_DIM = 64
A_V_DIM = 64
A_KV_RANK = 256
IDX_HEADS = 8
IDX_DIM = 64
TOPK_MAX = 256
B_HEADS = 8
B_HEAD_DIM = 64
B_MAPS = 2 * B_HEADS
REL_BUCKETS = 32
REL_MAX_DIST = 128
LN_EPS = 1e-5
NEG = -1e30
M_INIT = 0.5 * NEG
LOG2E = math.log2(math.e)
SEARCH_ROUND = ("interpolate", "value_mid", "interpolate", "key_mid")
SEARCH_OPENING = ("zero", "above_zero")
SEARCH_INTERP_ROUNDS = 6
FOLD_CHAINS = 8

LANE = 128
SUB = 8
VMEM_LIMIT = 56 * 1024 * 1024
DSA_TQ = 256
DIFF_TQ = 256

MXU_DTYPE = jnp.bfloat16


def _dot(a, b):
    return jnp.dot(a, b, preferred_element_type=jnp.float32)


def _cparams(*sem):
    return pltpu.CompilerParams(dimension_semantics=sem, vmem_limit_bytes=VMEM_LIMIT)


def _const_spec(shape):
    nd = len(shape)
    return pl.BlockSpec(shape, lambda *_: (0,) * nd)


def _layer_norm_rows(z, g, b):
    mu = jnp.mean(z, axis=-1, keepdims=True)
    zc = z - mu
    var = jnp.mean(zc * zc, axis=-1, keepdims=True)
    return zc * lax.rsqrt(var + LN_EPS) * g + b


def _ada_kernel(ct_ref, w_ref, b_ref, o_ref, *, nb):
    ct = ct_ref[...]
    s = ct * jax.nn.sigmoid(ct)
    w = w_ref[0]
    rows = [jnp.sum(w * s[:, b:b + 1], axis=0, keepdims=True) for b in range(nb)]
    o_ref[0] = jnp.concatenate(rows, axis=0) + b_ref[0]


def _ada_mod(c, ada_w, ada_b):
    depth, d, n = ada_w.shape
    nb = c.shape[0]
    tn = 768
    out = pl.pallas_call(
        functools.partial(_ada_kernel, nb=nb),
        grid=(depth, n // tn),
        in_specs=[
            pl.BlockSpec((d, nb), lambda l, j: (0, 0)),
            pl.BlockSpec((1, d, tn), lambda l, j: (l, 0, j)),
            pl.BlockSpec((1, 1, tn), lambda l, j: (l, 0, j)),
        ],
        out_specs=pl.BlockSpec((1, nb, tn), lambda l, j: (l, 0, j)),
        out_shape=jax.ShapeDtypeStruct((depth, nb, n), jnp.float32),
        compiler_params=_cparams("arbitrary", "arbitrary"),
        name="ada_mod",
    )(c.T, ada_w, ada_b.reshape(depth, 1, n))
    return out.reshape(depth, nb, 6, 1, d)


def _mod_spec(layer, k, d):
    return pl.BlockSpec((None, None, None, 1, d), lambda b, *_: (layer, b, k, 0, 0))


def _rel_bucket(dist):
    n = jnp.maximum(dist, 0)
    max_exact = REL_BUCKETS // 2
    nf = jnp.maximum(n, 1).astype(jnp.float32)
    large = max_exact + (jnp.log(nf / max_exact) / math.log(REL_MAX_DIST / max_exact)
                         * (REL_BUCKETS - max_exact)).astype(jnp.int32)
    large = jnp.minimum(large, REL_BUCKETS - 1)
    return jnp.where(n < max_exact, n, large)


def _bias_tile_kernel(tab_ref, bk_ref, o_ref, *, heads):
    bk = bk_ref[0]
    for h in range(heads):
        far = tab_ref[REL_BUCKETS - 1, h]
        acc = jnp.zeros(bk.shape, jnp.float32)
        for b in range(REL_BUCKETS - 1):
            acc = jnp.where(bk == b, (tab_ref[b, h] - far) * LOG2E, acc)
        o_ref[0, h] = jnp.where(bk < 0, NEG, acc)


def _bias_tiles(rel_bias, tq):
    heads = rel_bias.shape[1]
    r_sub = tq // LANE
    rels = jnp.arange(-2, r_sub + 1, dtype=jnp.int32)
    ql = jnp.arange(tq, dtype=jnp.int32)[None, :, None]
    sl = jnp.arange(LANE, dtype=jnp.int32)[None, None, :]
    dist = ql - rels[:, None, None] * LANE - sl
    bk = jnp.where(dist < 0, -1, _rel_bucket(dist))
    nk = r_sub + 3
    return pl.pallas_call(
        functools.partial(_bias_tile_kernel, heads=heads),
        grid=(nk,),
        in_specs=[
            pl.BlockSpec(memory_space=pltpu.SMEM),
            pl.BlockSpec((1, tq, LANE), lambda k: (k, 0, 0)),
        ],
        out_specs=pl.BlockSpec((1, heads, tq, LANE), lambda k: (k, 0, 0, 0)),
        out_shape=jax.ShapeDtypeStruct((nk, heads, tq, LANE), jnp.float32),
        compiler_params=_cparams("arbitrary"),
        name="bias_tiles",
    )(rel_bias, bk)


def _bias_index(i, j, c, tq, tk):
    rel = j * (tk // LANE) + c - i * (tq // LANE)
    return jnp.clip(rel + 2, 0, tq // LANE + 2)


def _causal_pairs(s, tq, tk):
    ii, jj = [], []
    for i in range(s // tq):
        for j in range((i * tq + tq - 1) // tk + 1):
            ii.append(i)
            jj.append(j)
    return jnp.asarray(ii, jnp.int32), jnp.asarray(jj, jnp.int32)


def _proj_dsa_kernel(x_ref, sc_ref, sh_ref, wq_ref, wc_ref, wi_ref, wk_ref, kvg_ref,
                     q_ref, ckv_ref, iq_ref, ikw_ref):
    h = (x_ref[0] * (1.0 + sc_ref[...]) + sh_ref[...]).astype(MXU_DTYPE)
    q_ref[0] = _dot(h, wq_ref[...]).astype(q_ref.dtype)
    ckv = _dot(h, wc_ref[...])
    ckv = ckv * lax.rsqrt(jnp.mean(ckv * ckv, axis=-1, keepdims=True) + LN_EPS) * kvg_ref[...]
    ckv_ref[0] = ckv.astype(ckv_ref.dtype)
    iq_ref[0] = _dot(h, wi_ref[...]).astype(iq_ref.dtype)
    ikw_ref[0] = _dot(h, wk_ref[...])


def _proj_dsa(x, mod, layer, w_in, kv_norm):
    b, s, d = x.shape
    tm = 512
    nq = A_HEADS * A_QK_DIM
    ni = IDX_HEADS * IDX_DIM
    o1, o2, o3 = nq, nq + A_KV_RANK, nq + A_KV_RANK + ni
    wq = w_in[:, :o1].astype(MXU_DTYPE)
    wc = w_in[:, o1:o2].astype(MXU_DTYPE)
    wi = w_in[:, o2:o3].astype(MXU_DTYPE)
    wk = jnp.pad(w_in[:, o3:], ((0, 0), (0, LANE - (w_in.shape[1] - o3)))).astype(MXU_DTYPE)
    row = lambda bb, t: (bb, t, 0)
    return pl.pallas_call(
        _proj_dsa_kernel,
        grid=(b, s // tm),
        in_specs=[
            pl.BlockSpec((1, tm, d), row),
            _mod_spec(layer, 1, d), _mod_spec(layer, 0, d),
            _const_spec((d, nq)), _const_spec((d, A_KV_RANK)), _const_spec((d, ni)),
            _const_spec((d, LANE)), _const_spec((1, A_KV_RANK)),
        ],
        out_specs=[
            pl.BlockSpec((1, tm, nq), row), pl.BlockSpec((1, tm, A_KV_RANK), row),
            pl.BlockSpec((1, tm, ni), row), pl.BlockSpec((1, tm, LANE), row),
        ],
        out_shape=[
            jax.ShapeDtypeStruct((b, s, nq), MXU_DTYPE),
            jax.ShapeDtypeStruct((b, s, A_KV_RANK), MXU_DTYPE),
            jax.ShapeDtypeStruct((b, s, ni), MXU_DTYPE),
            jax.ShapeDtypeStruct((b, s, LANE), jnp.float32),
        ],
        compiler_params=_cparams("parallel", "parallel"),
        name="proj_dsa",
    )(x, mod, mod, wq, wc, wi, wk, kv_norm.reshape(1, A_KV_RANK))


def _float_key(v):
    bits = pltpu.bitcast(v, jnp.int32)
    return bits ^ ((bits >> 31) & 0x7FFFFFFF)


def _key_float(k):
    return pltpu.bitcast(k ^ ((k >> 31) & 0x7FFFFFFF), jnp.float32)


def _indexer_kernel(ik_ref, iqt_ref, iwt_ref, mask_ref, keys_ref, p_ref, *, tq, ck, topk, s_len):
    i = pl.program_id(1)
    q0 = i * tq
    nkc = (q0 + tq + ck - 1) // ck
    w = iwt_ref[0] * (IDX_HEADS ** -0.5 * IDX_DIM ** -0.5)
    tpos = q0 + lax.broadcasted_iota(jnp.int32, (1, tq), 1)
    srow = lax.broadcasted_iota(jnp.int32, (ck, 1), 0)
    int_max = jnp.int32(2 ** 31 - 1)
    int_min = jnp.int32(-2 ** 31)

    def fold(x, op):
        slabs = [x[r * SUB:(r + 1) * SUB] for r in range(ck // SUB)]
        lanes = FOLD_CHAINS
        chains = slabs[:lanes]
        for r, slab in enumerate(slabs[lanes:]):
            chains[r % lanes] = op(chains[r % lanes], slab)
        slabs = chains
        while len(slabs) > 1:
            slabs = [op(a, b) for a, b in zip(slabs[0::2], slabs[1::2])]
        return slabs[0]

    def score_chunk(kc, carry):
        kmax, kmin = carry
        k0 = pl.multiple_of(kc * ck, ck)
        ik = ik_ref[0, pl.ds(k0, ck), :]
        sc = jnp.zeros((ck, tq), jnp.float32)
        for h in range(IDX_HEADS):
            lg = _dot(ik, iqt_ref[0, h * IDX_DIM:(h + 1) * IDX_DIM, :])
            sc = sc + jnp.maximum(lg, 0.0) * w[h:h + 1, :]
        causal = k0 + srow <= tpos
        key = _float_key(jnp.where(causal, sc, -jnp.inf))
        keys_ref[pl.ds(k0, ck), :] = key
        return (jnp.maximum(kmax, fold(key, jnp.maximum)),
                jnp.minimum(kmin, fold(jnp.where(causal, key, int_max), jnp.minimum)))

    kmax, kmin = lax.fori_loop(0, nkc, score_chunk, (jnp.full((SUB, tq), int_min, jnp.int32),
                                                      jnp.full((SUB, tq), int_max, jnp.int32)))
    kmax = jnp.max(kmax, axis=0, keepdims=True)
    kmin = jnp.min(kmin, axis=0, keepdims=True)

    def count(pred):
        def body(kc, accs):
            k0 = pl.multiple_of(kc * ck, ck)
            accs = list(accs)
            chunk = keys_ref.at[pl.ds(k0, ck), :]
            for r in range(ck // SUB):
                k = chunk[r * SUB:(r + 1) * SUB, :]
                hit = pred(k, k0 + srow[r * SUB:(r + 1) * SUB]).astype(jnp.float32)
                accs[r % FOLD_CHAINS] = accs[r % FOLD_CHAINS] + hit
            return tuple(accs)
        accs = lax.fori_loop(0, nkc, body, (jnp.zeros((SUB, tq), jnp.float32),) * FOLD_CHAINS)
        acc = functools.reduce(jnp.add, accs)
        return jnp.sum(acc, axis=0, keepdims=True).astype(jnp.int32)

    k_eff = jnp.minimum(topk, tpos + 1)
    log_keff = jnp.log(k_eff.astype(jnp.float32))

    def unsettled(lo, hi, cnt_lo):
        return (cnt_lo != k_eff) & (hi > lo + 1)

    def search_pass(st, kind, bisect_only):
        lo, hi, cnt_lo, cnt_hi = st
        key_mid = (lo >> 1) + (hi >> 1) + (lo & hi & 1)
        if kind == "key_mid":
            cand = key_mid
        elif kind in ("zero", "above_zero"):
            cand = jnp.full((1, tq), 0 if kind == "zero" else 1, jnp.int32)
        else:
            vlo, vhi = _key_float(lo), _key_float(hi)
            if kind == "interpolate":
                a = jnp.log(cnt_lo.astype(jnp.float32) + 0.5)
                b = jnp.log(cnt_hi.astype(jnp.float32) + 0.5)
                cand = _float_key(vlo + (vhi - vlo) * ((a - log_keff) / (a - b)))
            else:
                cand = _float_key(0.5 * vlo + 0.5 * vhi)
            cand = jnp.where(bisect_only, key_mid, cand)
        cand = jnp.clip(cand, lo + 1, hi - 1)
        cnt = count(lambda k, sp: k >= cand)
        upd = unsettled(lo, hi, cnt_lo)
        up = upd & (cnt >= k_eff)
        dn = upd & (cnt < k_eff)
        return (jnp.where(up, cand, lo), jnp.where(dn, cand, hi),
                jnp.where(up, cnt, cnt_lo), jnp.where(dn, cnt, cnt_hi))

    def search_cond(st):
        return jnp.max(unsettled(st[1], st[2], st[3]).astype(jnp.int32)) > 0

    def search_body(st):
        rounds, st = st[0], st[1:]
        bisect_only = rounds >= SEARCH_INTERP_ROUNDS
        for kind in SEARCH_ROUND:
            st = search_pass(st, kind, bisect_only)
        return (rounds + 1,) + st

    st = (kmin, kmax + 1, tpos + 1, jnp.zeros((1, tq), jnp.int32))
    for kind in SEARCH_OPENING:
        st = search_pass(st, kind, False)
    _, tau, hi, cnt_ge, cnt_gt = lax.while_loop(search_cond, search_body, (jnp.int32(0),) + st)
    need = k_eff - cnt_gt
    overflow = cnt_ge > k_eff
    p_ref[...] = jnp.full((1, tq), s_len, jnp.int32)

    @pl.when(jnp.max(overflow.astype(jnp.int32)) > 0)
    def _():
        def pos_step(it, pos):
            cand = pos + jnp.left_shift(jnp.int32(1), (s_len.bit_length() - 1) - it)
            below = count(lambda k, sp: (k == tau) & (sp < cand))
            return jnp.where(below < need, cand, pos)
        pos = lax.fori_loop(0, s_len.bit_length(), pos_step, jnp.zeros((1, tq), jnp.int32))
        p_ref[...] = jnp.where(overflow, pos, s_len)

    last_eq = p_ref[...]

    def write_chunk(kc, carry):
        k0 = pl.multiple_of(kc * ck, ck)

        @pl.when(kc < nkc)
        def _():
            k = keys_ref[pl.ds(k0, ck), :]
            sel = (k > tau) | ((k == tau) & (k0 + srow <= last_eq))
            mask_ref[0, :, pl.ds(k0, ck)] = jnp.where(sel, 0.0, NEG).T.astype(mask_ref.dtype)

        @pl.when(kc >= nkc)
        def _():
            mask_ref[0, :, pl.ds(k0, ck)] = jnp.full((tq, ck), NEG, mask_ref.dtype)
        return carry

    lax.fori_loop(0, s_len // ck, write_chunk, 0)


def _indexer_mask(ik, iqt, iwt, topk):
    b, s, di = ik.shape
    ni = iqt.shape[1]
    tq, ck = 256, 512
    return pl.pallas_call(
        functools.partial(_indexer_kernel, tq=tq, ck=ck, topk=topk, s_len=s),
        grid=(b, s // tq),
        in_specs=[
            pl.BlockSpec((1, s, di), lambda bb, i: (bb, 0, 0)),
            pl.BlockSpec((1, ni, tq), lambda bb, i: (bb, 0, i)),
            pl.BlockSpec((1, IDX_HEADS, tq), lambda bb, i: (bb, 0, i)),
        ],
        out_specs=pl.BlockSpec((1, tq, s), lambda bb, i: (bb, i, 0)),
        out_shape=jax.ShapeDtypeStruct((b, s, s), jnp.bfloat16),
        scratch_shapes=[pltpu.VMEM((s, tq), jnp.int32), pltpu.VMEM((1, tq), jnp.int32)],
        compiler_params=_cparams("parallel", "parallel"),
        name="indexer_topk",
    )(ik, iqt, iwt)


def _flash_update(s_cols, m_ref, l_ref, rows):
    m_old = m_ref[rows, :]
    m_cur = functools.reduce(jnp.maximum, s_cols)
    m_new = jnp.maximum(m_old, jnp.max(m_cur, axis=-1, keepdims=True))
    alpha = jnp.exp2(m_old - m_new)
    ps = [jnp.exp2(sc - m_new) for sc in s_cols]
    l_ref[rows, :] = alpha * l_ref[rows, :] + functools.reduce(jnp.add, ps)
    m_ref[rows, :] = m_new
    return jnp.concatenate(ps, axis=1).astype(MXU_DTYPE), alpha


def _tile_is_near(i, j, tq, tk):
    return (j + 1) * (tk // LANE) - i * (tq // LANE) >= 0


def _dispatch_tile(step, i, j, tq, tk):
    near = _tile_is_near(i, j, tq, tk)
    in_first_half = (i + 1) * tq - j * tk <= tk // 2
    pl.when(near & in_first_half)(lambda: step(True, tk // 2))
    pl.when(near & jnp.logical_not(in_first_half))(lambda: step(True, tk))
    pl.when(jnp.logical_not(near))(lambda: step(False, tk))


def _dsa_attn_kernel(itab, jtab, q_ref, wuk_ref, ckvt_ref, ckv_ref, mask_ref, nb_ref, wuv_ref,
                     o_ref, qlat_ref, m_ref, l_ref, acc_ref, *, tq, tk, hc):
    pidx = pl.program_id(1)
    i = itab[pidx]
    j = jtab[pidx]
    jlast = (i * tq + tq - 1) // tk
    heads = A_HEADS
    nchunk = heads // hc
    crows = hc * tq

    @pl.when(j == 0)
    def _():
        for h in range(heads):
            qh = q_ref[0, :, h * A_QK_DIM:(h + 1) * A_QK_DIM]
            ql = _dot(qh, wuk_ref[h]) * (A_QK_DIM ** -0.5 * LOG2E)
            qlat_ref[h * tq:(h + 1) * tq, :] = ql.astype(qlat_ref.dtype)
        m_ref[...] = jnp.full(m_ref.shape, M_INIT, jnp.float32)
        l_ref[...] = jnp.zeros(l_ref.shape, jnp.float32)
        acc_ref[...] = jnp.zeros(acc_ref.shape, jnp.float32)

    def step(with_bias, width):
        nsub = width // LANE
        ckvt = ckvt_ref[0, :, :width]
        ckv = ckv_ref[0, :width, :]
        maskf = mask_ref[0, :, :width].astype(jnp.float32)
        mcols = [maskf[:, c * LANE:(c + 1) * LANE] for c in range(nsub)]
        bidx = [_bias_index(i, j, c, tq, tk) for c in range(nsub)] if with_bias else None
        s_next = _dot(qlat_ref[0:crows, :], ckvt)
        for ci in range(nchunk):
            s_all = s_next
            if ci + 1 < nchunk:
                s_next = _dot(qlat_ref[(ci + 1) * crows:(ci + 2) * crows, :], ckvt)
            ps, alphas = [], []
            for hh in range(hc):
                h = ci * hc + hh
                cols = []
                for c in range(nsub):
                    add = mcols[c] + nb_ref[bidx[c], h] if with_bias else mcols[c]
                    cols.append(s_all[hh * tq:(hh + 1) * tq, c * LANE:(c + 1) * LANE] + add)
                p, alpha = _flash_update(cols, m_ref, l_ref, slice(h * tq, (h + 1) * tq))
                ps.append(p)
                alphas.append(alpha)
            pv = _dot(jnp.concatenate(ps, axis=0), ckv)
            a = jnp.concatenate(alphas, axis=0)
            rows = slice(ci * crows, (ci + 1) * crows)
            for half in range(A_KV_RANK // LANE):
                cs = slice(half * LANE, (half + 1) * LANE)
                acc_ref[rows, cs] = a * acc_ref[rows, cs] + pv[:, cs]

    _dispatch_tile(step, i, j, tq, tk)

    @pl.when(j == jlast)
    def _():
        for h in range(heads):
            rows = slice(h * tq, (h + 1) * tq)
            l = jnp.sum(l_ref[rows, :], axis=-1, keepdims=True)
            oh = _dot((acc_ref[rows, :] / l).astype(MXU_DTYPE), wuv_ref[h])
            o_ref[0, :, h * A_V_DIM:(h + 1) * A_V_DIM] = oh.astype(o_ref.dtype)


def _dsa_attention(q, ckv, ckvt, mask, nb, w_uk, w_uv):
    b, s, nq = q.shape
    tq, tk, hc = DSA_TQ, min(1024, s), 2
    itab, jtab = _causal_pairs(s, tq, tk)
    rows = A_HEADS * tq
    once = pl.Buffered(1)
    grid_spec = pltpu.PrefetchScalarGridSpec(
        num_scalar_prefetch=2,
        grid=(b, int(itab.shape[0])),
        in_specs=[
            pl.BlockSpec((1, tq, nq), lambda bb, p, it, jt: (bb, it[p], 0)),
            pl.BlockSpec(w_uk.shape, lambda bb, p, it, jt: (0, 0, 0), pipeline_mode=once),
            pl.BlockSpec((1, A_KV_RANK, tk), lambda bb, p, it, jt: (bb, 0, jt[p])),
            pl.BlockSpec((1, tk, A_KV_RANK), lambda bb, p, it, jt: (bb, jt[p], 0)),
            pl.BlockSpec((1, tq, tk), lambda bb, p, it, jt: (bb, it[p], jt[p])),
            pl.BlockSpec(nb.shape, lambda bb, p, it, jt: (0, 0, 0, 0), pipeline_mode=once),
            pl.BlockSpec(w_uv.shape, lambda bb, p, it, jt: (0, 0, 0), pipeline_mode=once),
        ],
        out_specs=pl.BlockSpec((1, tq, A_HEADS * A_V_DIM), lambda bb, p, it, jt: (bb, it[p], 0)),
        scratch_shapes=[
            pltpu.VMEM((rows, A_KV_RANK), MXU_DTYPE),
            pltpu.VMEM((rows, LANE), jnp.float32),
            pltpu.VMEM((rows, LANE), jnp.float32),
            pltpu.VMEM((rows, A_KV_RANK), jnp.float32),
        ],
    )
    return pl.pallas_call(
        functools.partial(_dsa_attn_kernel, tq=tq, tk=tk, hc=hc),
        grid_spec=grid_spec,
        out_shape=jax.ShapeDtypeStruct((b, s, A_HEADS * A_V_DIM), MXU_DTYPE),
        compiler_params=_cparams("parallel", "arbitrary"),
        name="dsa_attention",
    )(itab, jtab, q, w_uk, ckvt, ckv, mask, nb, w_uv)


def _proj_diff_kernel(x_ref, sc_ref, sh_ref, wq_ref, wk_ref, wv_ref, q_ref, k_ref, v_ref):
    h = (x_ref[0] * (1.0 + sc_ref[...]) + sh_ref[...]).astype(MXU_DTYPE)
    q_ref[0] = (_dot(h, wq_ref[...]) * (B_HEAD_DIM ** -0.5 * LOG2E)).astype(q_ref.dtype)
    k_ref[0] = _dot(h, wk_ref[...]).astype(k_ref.dtype)
    v_ref[0] = _dot(h, wv_ref[...]).astype(v_ref.dtype)


def _proj_diff(x, mod, layer, w_in):
    b, s, d = x.shape
    tm = 512
    n = w_in.shape[1] // 3
    ws = [w_in[:, k * n:(k + 1) * n].astype(MXU_DTYPE) for k in range(3)]
    row = lambda bb, t: (bb, t, 0)
    return pl.pallas_call(
        _proj_diff_kernel,
        grid=(b, s // tm),
        in_specs=[pl.BlockSpec((1, tm, d), row), _mod_spec(layer, 1, d), _mod_spec(layer, 0, d)]
        + [_const_spec((d, n))] * 3,
        out_specs=[pl.BlockSpec((1, tm, n), row)] * 3,
        out_shape=[jax.ShapeDtypeStruct((b, s, n), MXU_DTYPE)] * 3,
        compiler_params=_cparams("parallel", "parallel"),
        name="proj_diff",
    )(x, mod, mod, *ws)


def _diff_attn_kernel(itab, jtab, q_ref, kt_ref, v_ref, nb_ref, lam_ref, g_ref, o_ref,
                      m_ref, l_ref, acc_ref, *, tq, tk, lam_init):
    pidx = pl.program_id(1)
    i = itab[pidx]
    j = jtab[pidx]
    jlast = (i * tq + tq - 1) // tk
    dv = 2 * B_HEAD_DIM

    @pl.when(j == 0)
    def _():
        m_ref[...] = jnp.full(m_ref.shape, M_INIT, jnp.float32)
        l_ref[...] = jnp.zeros(l_ref.shape, jnp.float32)
        acc_ref[...] = jnp.zeros(acc_ref.shape, jnp.float32)

    def step(with_bias, width):
        nsub = width // LANE
        bidx = [_bias_index(i, j, c, tq, tk) for c in range(nsub)] if with_bias else None

        def logits(h):
            return [_dot(q_ref[0, 2 * h + mm], kt_ref[0, 2 * h + mm, :, :width]) for mm in range(2)]

        s_next = logits(0)
        for h in range(B_HEADS):
            s_pair = s_next
            if h + 1 < B_HEADS:
                s_next = logits(h + 1)
            ps, alphas = [], []
            for mm in range(2):
                col = 2 * h + mm
                cols = [s_pair[mm][:, c * LANE:(c + 1) * LANE] for c in range(nsub)]
                if with_bias:
                    cols = [cols[c] + nb_ref[bidx[c], col] for c in range(nsub)]
                p, alpha = _flash_update(cols, m_ref, l_ref, slice(col * tq, (col + 1) * tq))
                ps.append(p)
                alphas.append(alpha)
            pv = _dot(jnp.concatenate(ps, axis=0), v_ref[0, h, :width, :])
            rows = slice(2 * h * tq, (2 * h + 2) * tq)
            acc_ref[rows, :] = jnp.concatenate(alphas, axis=0) * acc_ref[rows, :] + pv

    _dispatch_tile(step, i, j, tq, tk)

    @pl.when(j == jlast)
    def _():
        lam = lam_ref[...]
        lam_full = (jnp.exp(jnp.sum(lam[0:1] * lam[1:2], axis=-1, keepdims=True))
                    - jnp.exp(jnp.sum(lam[2:3] * lam[3:4], axis=-1, keepdims=True)) + lam_init)
        for h in range(B_HEADS):
            r1 = slice(2 * h * tq, (2 * h + 1) * tq)
            r2 = slice((2 * h + 1) * tq, (2 * h + 2) * tq)
            l1 = jnp.sum(l_ref[r1, :], axis=-1, keepdims=True)
            l2 = jnp.sum(l_ref[r2, :], axis=-1, keepdims=True)
            o = acc_ref[r1, :] / l1 - lam_full * (acc_ref[r2, :] / l2)
            o = o * lax.rsqrt(jnp.mean(o * o, axis=-1, keepdims=True) + LN_EPS) * g_ref[...]
            o_ref[0, :, h * dv:(h + 1) * dv] = (o * (1.0 - lam_init)).astype(o_ref.dtype)


def _diff_attention(qm, ktm, vh, nb, lam, subln_g, lam_init):
    b, maps, s, dh = qm.shape
    tq, tk = DIFF_TQ, min(1024, s)
    dv = 2 * B_HEAD_DIM
    assert dv == LANE
    itab, jtab = _causal_pairs(s, tq, tk)
    once = pl.Buffered(1)
    grid_spec = pltpu.PrefetchScalarGridSpec(
        num_scalar_prefetch=2,
        grid=(b, int(itab.shape[0])),
        in_specs=[
            pl.BlockSpec((1, maps, tq, dh), lambda bb, p, it, jt: (bb, 0, it[p], 0)),
            pl.BlockSpec((1, maps, dh, tk), lambda bb, p, it, jt: (bb, 0, 0, jt[p])),
            pl.BlockSpec((1, B_HEADS, tk, dv), lambda bb, p, it, jt: (bb, 0, jt[p], 0)),
            pl.BlockSpec(nb.shape, lambda bb, p, it, jt: (0, 0, 0, 0), pipeline_mode=once),
            pl.BlockSpec(lam.shape, lambda bb, p, it, jt: (0, 0)),
            pl.BlockSpec((1, dv), lambda bb, p, it, jt: (0, 0)),
        ],
        out_specs=pl.BlockSpec((1, tq, B_HEADS * dv), lambda bb, p, it, jt: (bb, it[p], 0)),
        scratch_shapes=[
            pltpu.VMEM((maps * tq, LANE), jnp.float32),
            pltpu.VMEM((maps * tq, LANE), jnp.float32),
            pltpu.VMEM((maps * tq, dv), jnp.float32),
        ],
    )
    return pl.pallas_call(
        functools.partial(_diff_attn_kernel, tq=tq, tk=tk, lam_init=lam_init),
        grid_spec=grid_spec,
        out_shape=jax.ShapeDtypeStruct((b, s, B_HEADS * dv), MXU_DTYPE),
        compiler_params=_cparams("parallel", "arbitrary"),
        name="diff_attention",
    )(itab, jtab, qm, ktm, vh, nb, lam, subln_g.reshape(1, dv))


def _post_mixer_kernel(o_ref, x_ref, gt_ref, sc_ref, sh_ref, gc_ref, wo_ref, w1_ref, w2_ref,
                       lng_ref, lnb_ref, out_ref, *, tf, alpha):
    y = _dot(o_ref[0], wo_ref[...])
    x = _layer_norm_rows(alpha * x_ref[0] + (1.0 + gt_ref[...]) * y, lng_ref[0:1], lnb_ref[0:1])
    h = (x * (1.0 + sc_ref[...]) + sh_ref[...]).astype(MXU_DTYPE)
    y = jnp.zeros(x.shape, jnp.float32)
    for c in range(w1_ref.shape[1] // tf):
        a = jnp.maximum(_dot(h, w1_ref[:, c * tf:(c + 1) * tf]), 0.0)
        y = y + _dot((a * a).astype(MXU_DTYPE), w2_ref[c * tf:(c + 1) * tf, :])
    z = alpha * x + (1.0 + gc_ref[...]) * y
    out_ref[0] = _layer_norm_rows(z, lng_ref[1:2], lnb_ref[1:2])


def _post_mixer(o, x, mod, layer, w_o, w1, w2, ln_g, ln_b, alpha):
    b, s, d = x.shape
    dff = w1.shape[1]
    n_in = o.shape[-1]
    tm, tf = 512, 1024
    row = lambda bb, t: (bb, t, 0)
    once = pl.Buffered(1)
    return pl.pallas_call(
        functools.partial(_post_mixer_kernel, tf=tf, alpha=alpha),
        grid=(b, s // tm),
        in_specs=[
            pl.BlockSpec((1, tm, n_in), row), pl.BlockSpec((1, tm, d), row),
            _mod_spec(layer, 2, d), _mod_spec(layer, 4, d), _mod_spec(layer, 3, d),
            _mod_spec(layer, 5, d),
            pl.BlockSpec((n_in, d), lambda bb, t: (0, 0), pipeline_mode=once),
            pl.BlockSpec((d, dff), lambda bb, t: (0, 0), pipeline_mode=once),
            pl.BlockSpec((dff, d), lambda bb, t: (0, 0), pipeline_mode=once),
            _const_spec((2, d)), _const_spec((2, d)),
        ],
        out_specs=pl.BlockSpec((1, tm, d), row),
        out_shape=jax.ShapeDtypeStruct((b, s, d), jnp.float32),
        compiler_params=_cparams("parallel", "parallel"),
        name="post_mixer",
    )(o, x, mod, mod, mod, mod, w_o.astype(MXU_DTYPE), w1.astype(MXU_DTYPE),
      w2.astype(MXU_DTYPE), ln_g, ln_b)


def kernel(x, c, rel_bias, ada_w, ada_b, ln_g, ln_b, a_w_in, a_kv_norm, a_w_uk, a_w_uv, a_w_o,
           b_w_in, b_lambda, b_subln, b_w_o, mlp_w1, mlp_w2):
    depth = ada_w.shape[0]
    b, s, d = x.shape
    alpha = (2 * depth) ** 0.25
    topk = min(TOPK_MAX, s // 4)
    mod = _ada_mod(c, ada_w, ada_b)
    nb = {tq: _bias_tiles(rel_bias, tq) for tq in {DSA_TQ, DIFF_TQ}}
    nb_a, nb_b = nb[DSA_TQ], nb[DIFF_TQ]
    for i in range(depth):
        jm = i // N_MIXERS
        if i % N_MIXERS == 0:
            q, ckv, iq, ikw = _proj_dsa(x, mod, i, a_w_in[jm], a_kv_norm[jm])
            ik = ikw[..., :IDX_DIM].astype(MXU_DTYPE)
            iwt = jnp.swapaxes(ikw[..., IDX_DIM:IDX_DIM + IDX_HEADS], 1, 2)
            mask = _indexer_mask(ik, jnp.swapaxes(iq, 1, 2), iwt, topk)
            o = _dsa_attention(q, ckv, jnp.swapaxes(ckv, 1, 2), mask, nb_a,
                               a_w_uk[jm].astype(MXU_DTYPE), a_w_uv[jm].astype(MXU_DTYPE))
            w_o = a_w_o[jm]
        else:
            lam_init = 0.8 - 0.6 * math.exp(-0.3 * i)
            qf, kf, vf = _proj_diff(x, mod, i, b_w_in[jm])
            qm = jnp.transpose(qf.reshape(b, s, B_MAPS, B_HEAD_DIM), (0, 2, 1, 3))
            ktm = jnp.transpose(kf.reshape(b, s, B_MAPS, B_HEAD_DIM), (0, 2, 3, 1))
            vh = jnp.transpose(vf.reshape(b, s, B_HEADS, 2 * B_HEAD_DIM), (0, 2, 1, 3))
            o = _diff_attention(qm, ktm, vh, nb_b, b_lambda[jm], b_subln[jm], lam_init)
            w_o = b_w_o[jm]
        x = _post_mixer(o, x, mod, i, w_o, mlp_w1[i], mlp_w2[i], ln_g[i], ln_b[i], alpha)
    return x
```

```python
import functools
import math

import jax
import jax.numpy as jnp
from jax import lax
from jax.experimental import pallas as pl
from jax.experimental.pallas import tpu as pltpu

N_MIXERS = 2
A_HEADS = 16
A_QK_DIM = 64
A_V_DIM = 64
A_KV_RANK = 256
IDX_HEADS = 8
IDX_DIM = 64
TOPK_MAX = 256
B_HEADS = 8
B_HEAD_DIM = 64
B_MAPS = 2 * B_HEADS
REL_BUCKETS = 32
REL_MAX_DIST = 128
LN_EPS = 1e-5
NEG = -1e30
M_INIT = 0.5 * NEG
LOG2E = math.log2(math.e)
SEARCH_ROUND = ("interpolate", "value_mid", "interpolate", "key_mid")
SEARCH_OPENING = ("zero", "above_zero")
SEARCH_INTERP_ROUNDS = 6
FOLD_CHAINS = 8

LANE = 128
SUB = 8
VMEM_LIMIT = 56 * 1024 * 1024
DSA_TQ = 256
DIFF_TQ = 256
DIAG_PARTS = 4

MXU_DTYPE = jnp.bfloat16


def _dot(a, b):
    return jnp.dot(a, b, preferred_element_type=jnp.float32)


def _cparams(*sem):
    return pltpu.CompilerParams(dimension_semantics=sem, vmem_limit_bytes=VMEM_LIMIT)


def _const_spec(shape):
    nd = len(shape)
    return pl.BlockSpec(shape, lambda *_: (0,) * nd)


def _layer_norm_rows(z, g, b):
    mu = jnp.mean(z, axis=-1, keepdims=True)
    zc = z - mu
    var = jnp.mean(zc * zc, axis=-1, keepdims=True)
    return zc * lax.rsqrt(var + LN_EPS) * g + b


def _ada_kernel(ct_ref, w_ref, b_ref, o_ref, *, nb):
    ct = ct_ref[...]
    s = ct * jax.nn.sigmoid(ct)
    w = w_ref[0]
    rows = [jnp.sum(w * s[:, b:b + 1], axis=0, keepdims=True) for b in range(nb)]
    o_ref[0] = jnp.concatenate(rows, axis=0) + b_ref[0]


def _ada_mod(c, ada_w, ada_b):
    depth, d, n = ada_w.shape
    nb = c.shape[0]
    tn = 768
    out = pl.pallas_call(
        functools.partial(_ada_kernel, nb=nb),
        grid=(depth, n // tn),
        in_specs=[
            pl.BlockSpec((d, nb), lambda l, j: (0, 0)),
            pl.BlockSpec((1, d, tn), lambda l, j: (l, 0, j)),
            pl.BlockSpec((1, 1, tn), lambda l, j: (l, 0, j)),
        ],
        out_specs=pl.BlockSpec((1, nb, tn), lambda l, j: (l, 0, j)),
        out_shape=jax.ShapeDtypeStruct((depth, nb, n), jnp.float32),
        compiler_params=_cparams("arbitrary", "arbitrary"),
        name="ada_mod",
    )(c.T, ada_w, ada_b.reshape(depth, 1, n))
    return out.reshape(depth, nb, 6, 1, d)


def _mod_spec(layer, k, d):
    return pl.BlockSpec((None, None, None, 1, d), lambda b, *_: (layer, b, k, 0, 0))


def _rel_bucket(dist):
    n = jnp.maximum(dist, 0)
    max_exact = REL_BUCKETS // 2
    nf = jnp.maximum(n, 1).astype(jnp.float32)
    large = max_exact + (jnp.log(nf / max_exact) / math.log(REL_MAX_DIST / max_exact)
                         * (REL_BUCKETS - max_exact)).astype(jnp.int32)
    large = jnp.minimum(large, REL_BUCKETS - 1)
    return jnp.where(n < max_exact, n, large)


def _bias_tile_kernel(tab_ref, bk_ref, o_ref, *, heads):
    bk = bk_ref[0]
    for h in range(heads):
        far = tab_ref[REL_BUCKETS - 1, h]
        acc = jnp.zeros(bk.shape, jnp.float32)
        for b in range(REL_BUCKETS - 1):
            acc = jnp.where(bk == b, (tab_ref[b, h] - far) * LOG2E, acc)
        o_ref[0, h] = jnp.where(bk < 0, NEG, acc)


def _bias_tiles(rel_bias, tq):
    heads = rel_bias.shape[1]
    r_sub = tq // LANE
    rels = jnp.arange(-2, r_sub + 1, dtype=jnp.int32)
    ql = jnp.arange(tq, dtype=jnp.int32)[None, :, None]
    sl = jnp.arange(LANE, dtype=jnp.int32)[None, None, :]
    dist = ql - rels[:, None, None] * LANE - sl
    bk = jnp.where(dist < 0, -1, _rel_bucket(dist))
    nk = r_sub + 3
    return pl.pallas_call(
        functools.partial(_bias_tile_kernel, heads=heads),
        grid=(nk,),
        in_specs=[
            pl.BlockSpec(memory_space=pltpu.SMEM),
            pl.BlockSpec((1, tq, LANE), lambda k: (k, 0, 0)),
        ],
        out_specs=pl.BlockSpec((1, heads, tq, LANE), lambda k: (k, 0, 0, 0)),
        out_shape=jax.ShapeDtypeStruct((nk, heads, tq, LANE), jnp.float32),
        compiler_params=_cparams("arbitrary"),
        name="bias_tiles",
    )(rel_bias, bk)


def _bias_index(i, j, c, tq, tk):
    rel = j * (tk // LANE) + c - i * (tq // LANE)
    return jnp.clip(rel + 2, 0, tq // LANE + 2)


def _causal_pairs(s, tq, tk):
    ii, jj = [], []
    for i in range(s // tq):
        for j in range((i * tq + tq - 1) // tk + 1):
            ii.append(i)
            jj.append(j)
    return jnp.asarray(ii, jnp.int32), jnp.asarray(jj, jnp.int32)


def _proj_dsa_kernel(x_ref, sc_ref, sh_ref, wq_ref, wc_ref, wi_ref, wk_ref, kvg_ref,
                     q_ref, ckv_ref, iq_ref, ikw_ref):
    h = (x_ref[0] * (1.0 + sc_ref[...]) + sh_ref[...]).astype(MXU_DTYPE)
    q_ref[0] = _dot(h, wq_ref[...]).astype(q_ref.dtype)
    ckv = _dot(h, wc_ref[...])
    ckv = ckv * lax.rsqrt(jnp.mean(ckv * ckv, axis=-1, keepdims=True) + LN_EPS) * kvg_ref[...]
    ckv_ref[0] = ckv.astype(ckv_ref.dtype)
    iq_ref[0] = _dot(h, wi_ref[...]).astype(iq_ref.dtype)
    ikw_ref[0] = _dot(h, wk_ref[...])


def _proj_dsa(x, mod, layer, w_in, kv_norm):
    b, s, d = x.shape
    tm = 512
    nq = A_HEADS * A_QK_DIM
    ni = IDX_HEADS * IDX_DIM
    o1, o2, o3 = nq, nq + A_KV_RANK, nq + A_KV_RANK + ni
    wq = w_in[:, :o1].astype(MXU_DTYPE)
    wc = w_in[:, o1:o2].astype(MXU_DTYPE)
    wi = w_in[:, o2:o3].astype(MXU_DTYPE)
    wk = jnp.pad(w_in[:, o3:], ((0, 0), (0, LANE - (w_in.shape[1] - o3)))).astype(MXU_DTYPE)
    row = lambda bb, t: (bb, t, 0)
    return pl.pallas_call(
        _proj_dsa_kernel,
        grid=(b, s // tm),
        in_specs=[
            pl.BlockSpec((1, tm, d), row),
            _mod_spec(layer, 1, d), _mod_spec(layer, 0, d),
            _const_spec((d, nq)), _const_spec((d, A_KV_RANK)), _const_spec((d, ni)),
            _const_spec((d, LANE)), _const_spec((1, A_KV_RANK)),
        ],
        out_specs=[
            pl.BlockSpec((1, tm, nq), row), pl.BlockSpec((1, tm, A_KV_RANK), row),
            pl.BlockSpec((1, tm, ni), row), pl.BlockSpec((1, tm, LANE), row),
        ],
        out_shape=[
            jax.ShapeDtypeStruct((b, s, nq), MXU_DTYPE),
            jax.ShapeDtypeStruct((b, s, A_KV_RANK), MXU_DTYPE),
            jax.ShapeDtypeStruct((b, s, ni), MXU_DTYPE),
            jax.ShapeDtypeStruct((b, s, LANE), jnp.float32),
        ],
        compiler_params=_cparams("parallel", "parallel"),
        name="proj_dsa",
    )(x, mod, mod, wq, wc, wi, wk, kv_norm.reshape(1, A_KV_RANK))


def _float_key(v):
    bits = pltpu.bitcast(v, jnp.int32)
    return bits ^ ((bits >> 31) & 0x7FFFFFFF)


def _key_float(k):
    return pltpu.bitcast(k ^ ((k >> 31) & 0x7FFFFFFF), jnp.float32)


def _indexer_kernel(ik_ref, iqt_ref, iwt_ref, mask_ref, keys_ref, p_ref, *, tq, ck, topk, s_len):
    i = pl.program_id(1)
    q0 = i * tq
    nkc = (q0 + tq + ck - 1) // ck
    w = iwt_ref[0] * (IDX_HEADS ** -0.5 * IDX_DIM ** -0.5)
    tpos = q0 + lax.broadcasted_iota(jnp.int32, (1, tq), 1)
    srow = lax.broadcasted_iota(jnp.int32, (ck, 1), 0)
    int_max = jnp.int32(2 ** 31 - 1)
    int_min = jnp.int32(-2 ** 31)

    def fold(x, op):
        slabs = [x[r * SUB:(r + 1) * SUB] for r in range(ck // SUB)]
        lanes = FOLD_CHAINS
        chains = slabs[:lanes]
        for r, slab in enumerate(slabs[lanes:]):
            chains[r % lanes] = op(chains[r % lanes], slab)
        slabs = chains
        while len(slabs) > 1:
            slabs = [op(a, b) for a, b in zip(slabs[0::2], slabs[1::2])]
        return slabs[0]

    def score_chunk(kc, carry):
        kmax, kmin = carry
        k0 = pl.multiple_of(kc * ck, ck)
        ik = ik_ref[0, pl.ds(k0, ck), :]
        sc = jnp.zeros((ck, tq), jnp.float32)
        for h in range(IDX_HEADS):
            lg = _dot(ik, iqt_ref[0, h * IDX_DIM:(h + 1) * IDX_DIM, :])
            sc = sc + jnp.maximum(lg, 0.0) * w[h:h + 1, :]
        causal = k0 + srow <= tpos
        key = _float_key(jnp.where(causal, sc, -jnp.inf))
        keys_ref[pl.ds(k0, ck), :] = key
        return (jnp.maximum(kmax, fold(key, jnp.maximum)),
                jnp.minimum(kmin, fold(jnp.where(causal, key, int_max), jnp.minimum)))

    kmax, kmin = lax.fori_loop(0, nkc, score_chunk, (jnp.full((SUB, tq), int_min, jnp.int32),
                                                      jnp.full((SUB, tq), int_max, jnp.int32)))
    kmax = jnp.max(kmax, axis=0, keepdims=True)
    kmin = jnp.min(kmin, axis=0, keepdims=True)

    def count(pred):
        def body(kc, accs):
            k0 = pl.multiple_of(kc * ck, ck)
            accs = list(accs)
            chunk = keys_ref.at[pl.ds(k0, ck), :]
            for r in range(ck // SUB):
                k = chunk[r * SUB:(r + 1) * SUB, :]
                hit = pred(k, k0 + srow[r * SUB:(r + 1) * SUB]).astype(jnp.float32)
                accs[r % FOLD_CHAINS] = accs[r % FOLD_CHAINS] + hit
            return tuple(accs)
        accs = lax.fori_loop(0, nkc, body, (jnp.zeros((SUB, tq), jnp.float32),) * FOLD_CHAINS)
        acc = functools.reduce(jnp.add, accs)
        return jnp.sum(acc, axis=0, keepdims=True).astype(jnp.int32)

    k_eff = jnp.minimum(topk, tpos + 1)
    log_keff = jnp.log(k_eff.astype(jnp.float32))

    def unsettled(lo, hi, cnt_lo):
        return (cnt_lo != k_eff) & (hi > lo + 1)

    def search_pass(st, kind, bisect_only):
        lo, hi, cnt_lo, cnt_hi = st
        key_mid = (lo >> 1) + (hi >> 1) + (lo & hi & 1)
        if kind == "key_mid":
            cand = key_mid
        elif kind in ("zero", "above_zero"):
            cand = jnp.full((1, tq), 0 if kind == "zero" else 1, jnp.int32)
        else:
            vlo, vhi = _key_float(lo), _key_float(hi)
            if kind == "interpolate":
                a = jnp.log(cnt_lo.astype(jnp.float32) + 0.5)
                b = jnp.log(cnt_hi.astype(jnp.float32) + 0.5)
                cand = _float_key(vlo + (vhi - vlo) * ((a - log_keff) / (a - b)))
            else:
                cand = _float_key(0.5 * vlo + 0.5 * vhi)
            cand = jnp.where(bisect_only, key_mid, cand)
        cand = jnp.clip(cand, lo + 1, hi - 1)
        cnt = count(lambda k, sp: k >= cand)
        upd = unsettled(lo, hi, cnt_lo)
        up = upd & (cnt >= k_eff)
        dn = upd & (cnt < k_eff)
        return (jnp.where(up, cand, lo), jnp.where(dn, cand, hi),
                jnp.where(up, cnt, cnt_lo), jnp.where(dn, cnt, cnt_hi))

    def search_cond(st):
        return jnp.max(unsettled(st[1], st[2], st[3]).astype(jnp.int32)) > 0

    def search_body(st):
        rounds, st = st[0], st[1:]
        bisect_only = rounds >= SEARCH_INTERP_ROUNDS
        for kind in SEARCH_ROUND:
            st = search_pass(st, kind, bisect_only)
        return (rounds + 1,) + st

    st = (kmin, kmax + 1, tpos + 1, jnp.zeros((1, tq), jnp.int32))
    for kind in SEARCH_OPENING:
        st = search_pass(st, kind, False)
    _, tau, hi, cnt_ge, cnt_gt = lax.while_loop(search_cond, search_body, (jnp.int32(0),) + st)
    need = k_eff - cnt_gt
    overflow = cnt_ge > k_eff
    p_ref[...] = jnp.full((1, tq), s_len, jnp.int32)

    @pl.when(jnp.max(overflow.astype(jnp.int32)) > 0)
    def _():
        def pos_step(it, pos):
            cand = pos + jnp.left_shift(jnp.int32(1), (s_len.bit_length() - 1) - it)
            below = count(lambda k, sp: (k == tau) & (sp < cand))
            return jnp.where(below < need, cand, pos)
        pos = lax.fori_loop(0, s_len.bit_length(), pos_step, jnp.zeros((1, tq), jnp.int32))
        p_ref[...] = jnp.where(overflow, pos, s_len)

    last_eq = p_ref[...]

    def write_chunk(kc, carry):
        k0 = pl.multiple_of(kc * ck, ck)

        @pl.when(kc < nkc)
        def _():
            k = keys_ref[pl.ds(k0, ck), :]
            sel = (k > tau) | ((k == tau) & (k0 + srow <= last_eq))
            mask_ref[0, :, pl.ds(k0, ck)] = jnp.where(sel, 0.0, NEG).T.astype(mask_ref.dtype)

        @pl.when(kc >= nkc)
        def _():
            mask_ref[0, :, pl.ds(k0, ck)] = jnp.full((tq, ck), NEG, mask_ref.dtype)
        return carry

    lax.fori_loop(0, s_len // ck, write_chunk, 0)


def _indexer_mask(ik, iqt, iwt, topk):
    b, s, di = ik.shape
    ni = iqt.shape[1]
    tq, ck = 256, 512
    return pl.pallas_call(
        functools.partial(_indexer_kernel, tq=tq, ck=ck, topk=topk, s_len=s),
        grid=(b, s // tq),
        in_specs=[
            pl.BlockSpec((1, s, di), lambda bb, i: (bb, 0, 0)),
            pl.BlockSpec((1, ni, tq), lambda bb, i: (bb, 0, i)),
            pl.BlockSpec((1, IDX_HEADS, tq), lambda bb, i: (bb, 0, i)),
        ],
        out_specs=pl.BlockSpec((1, tq, s), lambda bb, i: (bb, i, 0)),
        out_shape=jax.ShapeDtypeStruct((b, s, s), jnp.bfloat16),
        scratch_shapes=[pltpu.VMEM((s, tq), jnp.int32), pltpu.VMEM((1, tq), jnp.int32)],
        compiler_params=_cparams("parallel", "parallel"),
        name="indexer_topk",
    )(ik, iqt, iwt)


def _flash_update(s_cols, m_ref, l_ref, rows):
    m_old = m_ref[rows, :]
    m_cur = functools.reduce(jnp.maximum, s_cols)
    m_new = jnp.maximum(m_old, jnp.max(m_cur, axis=-1, keepdims=True))
    alpha = jnp.exp2(m_old - m_new)
    ps = [jnp.exp2(sc - m_new) for sc in s_cols]
    l_ref[rows, :] = alpha * l_ref[rows, :] + functools.reduce(jnp.add, ps)
    m_ref[rows, :] = m_new
    return jnp.concatenate(ps, axis=1).astype(MXU_DTYPE), alpha


def _tile_is_near(i, j, tq, tk):
    return (j + 1) * (tk // LANE) - i * (tq // LANE) >= 0


def _dispatch_tile(step, i, j, tq, tk):
    near = _tile_is_near(i, j, tq, tk)
    causal_cols = (i + 1) * tq - j * tk
    part = tk // DIAG_PARTS
    for w in range(1, DIAG_PARTS + 1):
        fits = causal_cols > (w - 1) * part
        if w < DIAG_PARTS:
            fits = fits & (causal_cols <= w * part)
        pl.when(near & fits)(functools.partial(step, True, w * part))
    pl.when(jnp.logical_not(near))(functools.partial(step, False, tk))


def _dsa_attn_kernel(itab, jtab, q_ref, wuk_ref, ckvt_ref, ckv_ref, mask_ref, nb_ref, wuv_ref,
                     o_ref, qlat_ref, m_ref, l_ref, acc_ref, *, tq, tk, hc):
    pidx = pl.program_id(1)
    i = itab[pidx]
    j = jtab[pidx]
    jlast = (i * tq + tq - 1) // tk
    heads = A_HEADS
    nchunk = heads // hc
    crows = hc * tq

    @pl.when(j == 0)
    def _():
        for h in range(heads):
            qh = q_ref[0, :, h * A_QK_DIM:(h + 1) * A_QK_DIM]
            ql = _dot(qh, wuk_ref[h]) * (A_QK_DIM ** -0.5 * LOG2E)
            qlat_ref[h * tq:(h + 1) * tq, :] = ql.astype(qlat_ref.dtype)
        m_ref[...] = jnp.full(m_ref.shape, M_INIT, jnp.float32)
        l_ref[...] = jnp.zeros(l_ref.shape, jnp.float32)
        acc_ref[...] = jnp.zeros(acc_ref.shape, jnp.float32)

    def step(with_bias, width):
        nsub = width // LANE
        ckvt = ckvt_ref[0, :, :width]
        ckv = ckv_ref[0, :width, :]
        maskf = mask_ref[0, :, :width].astype(jnp.float32)
        mcols = [maskf[:, c * LANE:(c + 1) * LANE] for c in range(nsub)]
        bidx = [_bias_index(i, j, c, tq, tk) for c in range(nsub)] if with_bias else None
        s_next = _dot(qlat_ref[0:crows, :], ckvt)
        for ci in range(nchunk):
            s_all = s_next
            if ci + 1 < nchunk:
                s_next = _dot(qlat_ref[(ci + 1) * crows:(ci + 2) * crows, :], ckvt)
            ps, alphas = [], []
            for hh in range(hc):
                h = ci * hc + hh
                cols = []
                for c in range(nsub):
                    add = mcols[c] + nb_ref[bidx[c], h] if with_bias else mcols[c]
                    cols.append(s_all[hh * tq:(hh + 1) * tq, c * LANE:(c + 1) * LANE] + add)
                p, alpha = _flash_update(cols, m_ref, l_ref, slice(h * tq, (h + 1) * tq))
                ps.append(p)
                alphas.append(alpha)
            pv = _dot(jnp.concatenate(ps, axis=0), ckv)
            a = jnp.concatenate(alphas, axis=0)
            rows = slice(ci * crows, (ci + 1) * crows)
            for half in range(A_KV_RANK // LANE):
                cs = slice(half * LANE, (half + 1) * LANE)
                acc_ref[rows, cs] = a * acc_ref[rows, cs] + pv[:, cs]

    _dispatch_tile(step, i, j, tq, tk)

    @pl.when(j == jlast)
    def _():
        for h in range(heads):
            rows = slice(h * tq, (h + 1) * tq)
            l = jnp.sum(l_ref[rows, :], axis=-1, keepdims=True)
            oh = _dot((acc_ref[rows, :] / l).astype(MXU_DTYPE), wuv_ref[h])
            o_ref[0, :, h * A_V_DIM:(h + 1) * A_V_DIM] = oh.astype(o_ref.dtype)


def _dsa_attention(q, ckv, ckvt, mask, nb, w_uk, w_uv):
    b, s, nq = q.shape
    tq, tk, hc = DSA_TQ, min(1024, s), 2
    itab, jtab = _causal_pairs(s, tq, tk)
    rows = A_HEADS * tq
    once = pl.Buffered(1)
    grid_spec = pltpu.PrefetchScalarGridSpec(
        num_scalar_prefetch=2,
        grid=(b, int(itab.shape[0])),
        in_specs=[
            pl.BlockSpec((1, tq, nq), lambda bb, p, it, jt: (bb, it[p], 0)),
            pl.BlockSpec(w_uk.shape, lambda bb, p, it, jt: (0, 0, 0), pipeline_mode=once),
            pl.BlockSpec((1, A_KV_RANK, tk), lambda bb, p, it, jt: (bb, 0, jt[p])),
            pl.BlockSpec((1, tk, A_KV_RANK), lambda bb, p, it, jt: (bb, jt[p], 0)),
            pl.BlockSpec((1, tq, tk), lambda bb, p, it, jt: (bb, it[p], jt[p])),
            pl.BlockSpec(nb.shape, lambda bb, p, it, jt: (0, 0, 0, 0), pipeline_mode=once),
            pl.BlockSpec(w_uv.shape, lambda bb, p, it, jt: (0, 0, 0), pipeline_mode=once),
        ],
        out_specs=pl.BlockSpec((1, tq, A_HEADS * A_V_DIM), lambda bb, p, it, jt: (bb, it[p], 0)),
        scratch_shapes=[
            pltpu.VMEM((rows, A_KV_RANK), MXU_DTYPE),
            pltpu.VMEM((rows, LANE), jnp.float32),
            pltpu.VMEM((rows, LANE), jnp.float32),
            pltpu.VMEM((rows, A_KV_RANK), jnp.float32),
        ],
    )
    return pl.pallas_call(
        functools.partial(_dsa_attn_kernel, tq=tq, tk=tk, hc=hc),
        grid_spec=grid_spec,
        out_shape=jax.ShapeDtypeStruct((b, s, A_HEADS * A_V_DIM), MXU_DTYPE),
        compiler_params=_cparams("parallel", "arbitrary"),
        name="dsa_attention",
    )(itab, jtab, q, w_uk, ckvt, ckv, mask, nb, w_uv)


def _proj_diff_kernel(x_ref, sc_ref, sh_ref, wq_ref, wk_ref, wv_ref, q_ref, k_ref, v_ref):
    h = (x_ref[0] * (1.0 + sc_ref[...]) + sh_ref[...]).astype(MXU_DTYPE)
    q_ref[0] = (_dot(h, wq_ref[...]) * (B_HEAD_DIM ** -0.5 * LOG2E)).astype(q_ref.dtype)
    k_ref[0] = _dot(h, wk_ref[...]).astype(k_ref.dtype)
    v_ref[0] = _dot(h, wv_ref[...]).astype(v_ref.dtype)


def _proj_diff(x, mod, layer, w_in):
    b, s, d = x.shape
    tm = 512
    n = w_in.shape[1] // 3
    ws = [w_in[:, k * n:(k + 1) * n].astype(MXU_DTYPE) for k in range(3)]
    row = lambda bb, t: (bb, t, 0)
    return pl.pallas_call(
        _proj_diff_kernel,
        grid=(b, s // tm),
        in_specs=[pl.BlockSpec((1, tm, d), row), _mod_spec(layer, 1, d), _mod_spec(layer, 0, d)]
        + [_const_spec((d, n))] * 3,
        out_specs=[pl.BlockSpec((1, tm, n), row)] * 3,
        out_shape=[jax.ShapeDtypeStruct((b, s, n), MXU_DTYPE)] * 3,
        compiler_params=_cparams("parallel", "parallel"),
        name="proj_diff",
    )(x, mod, mod, *ws)


def _diff_attn_kernel(itab, jtab, q_ref, kt_ref, v_ref, nb_ref, lam_ref, g_ref, o_ref,
                      m_ref, l_ref, acc_ref, *, tq, tk, lam_init):
    pidx = pl.program_id(1)
    i = itab[pidx]
    j = jtab[pidx]
    jlast = (i * tq + tq - 1) // tk
    dv = 2 * B_HEAD_DIM

    @pl.when(j == 0)
    def _():
        m_ref[...] = jnp.full(m_ref.shape, M_INIT, jnp.float32)
        l_ref[...] = jnp.zeros(l_ref.shape, jnp.float32)
        acc_ref[...] = jnp.zeros(acc_ref.shape, jnp.float32)

    def step(with_bias, width):
        nsub = width // LANE
        bidx = [_bias_index(i, j, c, tq, tk) for c in range(nsub)] if with_bias else None

        def logits(h):
            return [_dot(q_ref[0, 2 * h + mm], kt_ref[0, 2 * h + mm, :, :width]) for mm in range(2)]

        s_next = logits(0)
        for h in range(B_HEADS):
            s_pair = s_next
            if h + 1 < B_HEADS:
                s_next = logits(h + 1)
            ps, alphas = [], []
            for mm in range(2):
                col = 2 * h + mm
                cols = [s_pair[mm][:, c * LANE:(c + 1) * LANE] for c in range(nsub)]
                if with_bias:
                    cols = [cols[c] + nb_ref[bidx[c], col] for c in range(nsub)]
                p, alpha = _flash_update(cols, m_ref, l_ref, slice(col * tq, (col + 1) * tq))
                ps.append(p)
                alphas.append(alpha)
            pv = _dot(jnp.concatenate(ps, axis=0), v_ref[0, h, :width, :])
            rows = slice(2 * h * tq, (2 * h + 2) * tq)
            acc_ref[rows, :] = jnp.concatenate(alphas, axis=0) * acc_ref[rows, :] + pv

    _dispatch_tile(step, i, j, tq, tk)

    @pl.when(j == jlast)
    def _():
        lam = lam_ref[...]
        lam_full = (jnp.exp(jnp.sum(lam[0:1] * lam[1:2], axis=-1, keepdims=True))
                    - jnp.exp(jnp.sum(lam[2:3] * lam[3:4], axis=-1, keepdims=True)) + lam_init)
        for h in range(B_HEADS):
            r1 = slice(2 * h * tq, (2 * h + 1) * tq)
            r2 = slice((2 * h + 1) * tq, (2 * h + 2) * tq)
            l1 = jnp.sum(l_ref[r1, :], axis=-1, keepdims=True)
            l2 = jnp.sum(l_ref[r2, :], axis=-1, keepdims=True)
            o = acc_ref[r1, :] / l1 - lam_full * (acc_ref[r2, :] / l2)
            o = o * lax.rsqrt(jnp.mean(o * o, axis=-1, keepdims=True) + LN_EPS) * g_ref[...]
            o_ref[0, :, h * dv:(h + 1) * dv] = (o * (1.0 - lam_init)).astype(o_ref.dtype)


def _diff_attention(qm, ktm, vh, nb, lam, subln_g, lam_init):
    b, maps, s, dh = qm.shape
    tq, tk = DIFF_TQ, min(1024, s)
    dv = 2 * B_HEAD_DIM
    assert dv == LANE
    itab, jtab = _causal_pairs(s, tq, tk)
    once = pl.Buffered(1)
    grid_spec = pltpu.PrefetchScalarGridSpec(
        num_scalar_prefetch=2,
        grid=(b, int(itab.shape[0])),
        in_specs=[
            pl.BlockSpec((1, maps, tq, dh), lambda bb, p, it, jt: (bb, 0, it[p], 0)),
            pl.BlockSpec((1, maps, dh, tk), lambda bb, p, it, jt: (bb, 0, 0, jt[p])),
            pl.BlockSpec((1, B_HEADS, tk, dv), lambda bb, p, it, jt: (bb, 0, jt[p], 0)),
            pl.BlockSpec(nb.shape, lambda bb, p, it, jt: (0, 0, 0, 0), pipeline_mode=once),
            pl.BlockSpec(lam.shape, lambda bb, p, it, jt: (0, 0)),
            pl.BlockSpec((1, dv), lambda bb, p, it, jt: (0, 0)),
        ],
        out_specs=pl.BlockSpec((1, tq, B_HEADS * dv), lambda bb, p, it, jt: (bb, it[p], 0)),
        scratch_shapes=[
            pltpu.VMEM((maps * tq, LANE), jnp.float32),
            pltpu.VMEM((maps * tq, LANE), jnp.float32),
            pltpu.VMEM((maps * tq, dv), jnp.float32),
        ],
    )
    return pl.pallas_call(
        functools.partial(_diff_attn_kernel, tq=tq, tk=tk, lam_init=lam_init),
        grid_spec=grid_spec,
        out_shape=jax.ShapeDtypeStruct((b, s, B_HEADS * dv), MXU_DTYPE),
        compiler_params=_cparams("parallel", "arbitrary"),
        name="diff_attention",
    )(itab, jtab, qm, ktm, vh, nb, lam, subln_g.reshape(1, dv))


def _post_mixer_kernel(o_ref, x_ref, gt_ref, sc_ref, sh_ref, gc_ref, wo_ref, w1_ref, w2_ref,
                       lng_ref, lnb_ref, out_ref, *, tf, alpha):
    y = _dot(o_ref[0], wo_ref[...])
    x = _layer_norm_rows(alpha * x_ref[0] + (1.0 + gt_ref[...]) * y, lng_ref[0:1], lnb_ref[0:1])
    h = (x * (1.0 + sc_ref[...]) + sh_ref[...]).astype(MXU_DTYPE)
    y = jnp.zeros(x.shape, jnp.float32)
    for c in range(w1_ref.shape[1] // tf):
        a = jnp.maximum(_dot(h, w1_ref[:, c * tf:(c + 1) * tf]), 0.0)
        y = y + _dot((a * a).astype(MXU_DTYPE), w2_ref[c * tf:(c + 1) * tf, :])
    z = alpha * x + (1.0 + gc_ref[...]) * y
    out_ref[0] = _layer_norm_rows(z, lng_ref[1:2], lnb_ref[1:2])


def _post_mixer(o, x, mod, layer, w_o, w1, w2, ln_g, ln_b, alpha):
    b, s, d = x.shape
    dff = w1.shape[1]
    n_in = o.shape[-1]
    tm, tf = 512, 1024
    row = lambda bb, t: (bb, t, 0)
    once = pl.Buffered(1)
    return pl.pallas_call(
        functools.partial(_post_mixer_kernel, tf=tf, alpha=alpha),
        grid=(b, s // tm),
        in_specs=[
            pl.BlockSpec((1, tm, n_in), row), pl.BlockSpec((1, tm, d), row),
            _mod_spec(layer, 2, d), _mod_spec(layer, 4, d), _mod_spec(layer, 3, d),
            _mod_spec(layer, 5, d),
            pl.BlockSpec((n_in, d), lambda bb, t: (0, 0), pipeline_mode=once),
            pl.BlockSpec((d, dff), lambda bb, t: (0, 0), pipeline_mode=once),
            pl.BlockSpec((dff, d), lambda bb, t: (0, 0), pipeline_mode=once),
            _const_spec((2, d)), _const_spec((2, d)),
        ],
        out_specs=pl.BlockSpec((1, tm, d), row),
        out_shape=jax.ShapeDtypeStruct((b, s, d), jnp.float32),
        compiler_params=_cparams("parallel", "parallel"),
        name="post_mixer",
    )(o, x, mod, mod, mod, mod, w_o.astype(MXU_DTYPE), w1.astype(MXU_DTYPE),
      w2.astype(MXU_DTYPE), ln_g, ln_b)


def kernel(x, c, rel_bias, ada_w, ada_b, ln_g, ln_b, a_w_in, a_kv_norm, a_w_uk, a_w_uv, a_w_o,
           b_w_in, b_lambda, b_subln, b_w_o, mlp_w1, mlp_w2):
    depth = ada_w.shape[0]
    b, s, d = x.shape
    alpha = (2 * depth) ** 0.25
    topk = min(TOPK_MAX, s // 4)
    mod = _ada_mod(c, ada_w, ada_b)
    nb = {tq: _bias_tiles(rel_bias, tq) for tq in {DSA_TQ, DIFF_TQ}}
    nb_a, nb_b = nb[DSA_TQ], nb[DIFF_TQ]
    for i in range(depth):
        jm = i // N_MIXERS
        if i % N_MIXERS == 0:
            q, ckv, iq, ikw = _proj_dsa(x, mod, i, a_w_in[jm], a_kv_norm[jm])
            ik = ikw[..., :IDX_DIM].astype(MXU_DTYPE)
            iwt = jnp.swapaxes(ikw[..., IDX_DIM:IDX_DIM + IDX_HEADS], 1, 2)
            mask = _indexer_mask(ik, jnp.swapaxes(iq, 1, 2), iwt, topk)
            o = _dsa_attention(q, ckv, jnp.swapaxes(ckv, 1, 2), mask, nb_a,
                               a_w_uk[jm].astype(MXU_DTYPE), a_w_uv[jm].astype(MXU_DTYPE))
            w_o = a_w_o[jm]
        else:
            lam_init = 0.8 - 0.6 * math.exp(-0.3 * i)
            qf, kf, vf = _proj_diff(x, mod, i, b_w_in[jm])
            qm = jnp.transpose(qf.reshape(b, s, B_MAPS, B_HEAD_DIM), (0, 2, 1, 3))
            ktm = jnp.transpose(kf.reshape(b, s, B_MAPS, B_HEAD_DIM), (0, 2, 3, 1))
            vh = jnp.transpose(vf.reshape(b, s, B_HEADS, 2 * B_HEAD_DIM), (0, 2, 1, 3))
            o = _diff_attention(qm, ktm, vh, nb_b, b_lambda[jm], b_subln[jm], lam_init)
            w_o = b_w_o[jm]
        x = _post_mixer(o, x, mod, i, w_o, mlp_w1[i], mlp_w2[i], ln_g[i], ln_b[i], alpha)
    return x
```

```python
import functools
import math

import jax
import jax.numpy as jnp
from jax import lax
from jax.experimental import pallas as pl
from jax.experimental.pallas import tpu as pltpu

N_MIXERS = 2
A_HEADS = 16
A_QK_DIM = 64
A_V_DIM = 64
A_KV_RANK = 256
IDX_HEADS = 8
IDX_DIM = 64
TOPK_MAX = 256
B_HEADS = 8
B_HEAD_DIM = 64
B_MAPS = 2 * B_HEADS
REL_BUCKETS = 32
REL_MAX_DIST = 128
LN_EPS = 1e-5
NEG = -1e30
M_INIT = 0.5 * NEG
LOG2E = math.log2(math.e)
SEARCH_ROUND = ("interpolate", "value_mid", "interpolate", "key_mid")
SEARCH_OPENING = ("zero", "above_zero")
SEARCH_INTERP_ROUNDS = 6
FOLD_CHAINS = 8
COARSE_BITS = 15

LANE = 128
SUB = 8
VMEM_LIMIT = 56 * 1024 * 1024
DSA_TQ = 256
DIFF_TQ = 256
DIAG_PARTS = 4

MXU_DTYPE = jnp.bfloat16


def _dot(a, b):
    return jnp.dot(a, b, preferred_element_type=jnp.float32)


def _cparams(*sem):
    return pltpu.CompilerParams(dimension_semantics=sem, vmem_limit_bytes=VMEM_LIMIT)


def _const_spec(shape):
    nd = len(shape)
    return pl.BlockSpec(shape, lambda *_: (0,) * nd)


def _layer_norm_rows(z, g, b):
    mu = jnp.mean(z, axis=-1, keepdims=True)
    zc = z - mu
    var = jnp.mean(zc * zc, axis=-1, keepdims=True)
    return zc * lax.rsqrt(var + LN_EPS) * g + b


def _ada_kernel(ct_ref, w_ref, b_ref, o_ref, *, nb):
    ct = ct_ref[...]
    s = ct * jax.nn.sigmoid(ct)
    w = w_ref[0]
    rows = [jnp.sum(w * s[:, b:b + 1], axis=0, keepdims=True) for b in range(nb)]
    o_ref[0] = jnp.concatenate(rows, axis=0) + b_ref[0]


def _ada_mod(c, ada_w, ada_b):
    depth, d, n = ada_w.shape
    nb = c.shape[0]
    tn = 768
    out = pl.pallas_call(
        functools.partial(_ada_kernel, nb=nb),
        grid=(depth, n // tn),
        in_specs=[
            pl.BlockSpec((d, nb), lambda l, j: (0, 0)),
            pl.BlockSpec((1, d, tn), lambda l, j: (l, 0, j)),
            pl.BlockSpec((1, 1, tn), lambda l, j: (l, 0, j)),
        ],
        out_specs=pl.BlockSpec((1, nb, tn), lambda l, j: (l, 0, j)),
        out_shape=jax.ShapeDtypeStruct((depth, nb, n), jnp.float32),
        compiler_params=_cparams("arbitrary", "arbitrary"),
        name="ada_mod",
    )(c.T, ada_w, ada_b.reshape(depth, 1, n))
    return out.reshape(depth, nb, 6, 1, d)


def _mod_spec(layer, k, d):
    return pl.BlockSpec((None, None, None, 1, d), lambda b, *_: (layer, b, k, 0, 0))


def _rel_bucket(dist):
    n = jnp.maximum(dist, 0)
    max_exact = REL_BUCKETS // 2
    nf = jnp.maximum(n, 1).astype(jnp.float32)
    large = max_exact + (jnp.log(nf / max_exact) / math.log(REL_MAX_DIST / max_exact)
                         * (REL_BUCKETS - max_exact)).astype(jnp.int32)
    large = jnp.minimum(large, REL_BUCKETS - 1)
    return jnp.where(n < max_exact, n, large)


def _bias_tile_kernel(tab_ref, bk_ref, o_ref, *, heads):
    bk = bk_ref[0]
    for h in range(heads):
        far = tab_ref[REL_BUCKETS - 1, h]
        acc = jnp.zeros(bk.shape, jnp.float32)
        for b in range(REL_BUCKETS - 1):
            acc = jnp.where(bk == b, (tab_ref[b, h] - far) * LOG2E, acc)
        o_ref[0, h] = jnp.where(bk < 0, NEG, acc)


def _bias_tiles(rel_bias, tq):
    heads = rel_bias.shape[1]
    r_sub = tq // LANE
    rels = jnp.arange(-2, r_sub + 1, dtype=jnp.int32)
    ql = jnp.arange(tq, dtype=jnp.int32)[None, :, None]
    sl = jnp.arange(LANE, dtype=jnp.int32)[None, None, :]
    dist = ql - rels[:, None, None] * LANE - sl
    bk = jnp.where(dist < 0, -1, _rel_bucket(dist))
    nk = r_sub + 3
    return pl.pallas_call(
        functools.partial(_bias_tile_kernel, heads=heads),
        grid=(nk,),
        in_specs=[
            pl.BlockSpec(memory_space=pltpu.SMEM),
            pl.BlockSpec((1, tq, LANE), lambda k: (k, 0, 0)),
        ],
        out_specs=pl.BlockSpec((1, heads, tq, LANE), lambda k: (k, 0, 0, 0)),
        out_shape=jax.ShapeDtypeStruct((nk, heads, tq, LANE), jnp.float32),
        compiler_params=_cparams("arbitrary"),
        name="bias_tiles",
    )(rel_bias, bk)


def _bias_index(i, j, c, tq, tk):
    rel = j * (tk // LANE) + c - i * (tq // LANE)
    return jnp.clip(rel + 2, 0, tq // LANE + 2)


def _causal_pairs(s, tq, tk):
    ii, jj = [], []
    for i in range(s // tq):
        for j in range((i * tq + tq - 1) // tk + 1):
            ii.append(i)
            jj.append(j)
    return jnp.asarray(ii, jnp.int32), jnp.asarray(jj, jnp.int32)


def _proj_dsa_kernel(x_ref, sc_ref, sh_ref, wq_ref, wc_ref, wi_ref, wk_ref, kvg_ref,
                     q_ref, ckv_ref, iq_ref, ikw_ref):
    h = (x_ref[0] * (1.0 + sc_ref[...]) + sh_ref[...]).astype(MXU_DTYPE)
    q_ref[0] = _dot(h, wq_ref[...]).astype(q_ref.dtype)
    ckv = _dot(h, wc_ref[...])
    ckv = ckv * lax.rsqrt(jnp.mean(ckv * ckv, axis=-1, keepdims=True) + LN_EPS) * kvg_ref[...]
    ckv_ref[0] = ckv.astype(ckv_ref.dtype)
    iq_ref[0] = _dot(h, wi_ref[...]).astype(iq_ref.dtype)
    ikw_ref[0] = _dot(h, wk_ref[...])


def _proj_dsa(x, mod, layer, w_in, kv_norm):
    b, s, d = x.shape
    tm = 512
    nq = A_HEADS * A_QK_DIM
    ni = IDX_HEADS * IDX_DIM
    o1, o2, o3 = nq, nq + A_KV_RANK, nq + A_KV_RANK + ni
    wq = w_in[:, :o1].astype(MXU_DTYPE)
    wc = w_in[:, o1:o2].astype(MXU_DTYPE)
    wi = w_in[:, o2:o3].astype(MXU_DTYPE)
    wk = jnp.pad(w_in[:, o3:], ((0, 0), (0, LANE - (w_in.shape[1] - o3)))).astype(MXU_DTYPE)
    row = lambda bb, t: (bb, t, 0)
    return pl.pallas_call(
        _proj_dsa_kernel,
        grid=(b, s // tm),
        in_specs=[
            pl.BlockSpec((1, tm, d), row),
            _mod_spec(layer, 1, d), _mod_spec(layer, 0, d),
            _const_spec((d, nq)), _const_spec((d, A_KV_RANK)), _const_spec((d, ni)),
            _const_spec((d, LANE)), _const_spec((1, A_KV_RANK)),
        ],
        out_specs=[
            pl.BlockSpec((1, tm, nq), row), pl.BlockSpec((1, tm, A_KV_RANK), row),
            pl.BlockSpec((1, tm, ni), row), pl.BlockSpec((1, tm, LANE), row),
        ],
        out_shape=[
            jax.ShapeDtypeStruct((b, s, nq), MXU_DTYPE),
            jax.ShapeDtypeStruct((b, s, A_KV_RANK), MXU_DTYPE),
            jax.ShapeDtypeStruct((b, s, ni), MXU_DTYPE),
            jax.ShapeDtypeStruct((b, s, LANE), jnp.float32),
        ],
        compiler_params=_cparams("parallel", "parallel"),
        name="proj_dsa",
    )(x, mod, mod, wq, wc, wi, wk, kv_norm.reshape(1, A_KV_RANK))


def _float_key(v):
    bits = pltpu.bitcast(v, jnp.int32)
    return bits ^ ((bits >> 31) & 0x7FFFFFFF)


def _key_float(k):
    return pltpu.bitcast(k ^ ((k >> 31) & 0x7FFFFFFF), jnp.float32)


def _indexer_kernel(ik_ref, iqt_ref, iwt_ref, mask_ref, keys_ref, keys16_ref, p_ref, *,
                    tq, ck, topk, s_len):
    i = pl.program_id(1)
    q0 = i * tq
    nkc = (q0 + tq + ck - 1) // ck
    w = iwt_ref[0] * (IDX_HEADS ** -0.5 * IDX_DIM ** -0.5)
    tpos = q0 + lax.broadcasted_iota(jnp.int32, (1, tq), 1)
    srow = lax.broadcasted_iota(jnp.int32, (ck, 1), 0)
    int_max = jnp.int32(2 ** 31 - 1)
    int_min = jnp.int32(-2 ** 31)

    def fold(x, op):
        slabs = [x[r * SUB:(r + 1) * SUB] for r in range(ck // SUB)]
        lanes = FOLD_CHAINS
        chains = slabs[:lanes]
        for r, slab in enumerate(slabs[lanes:]):
            chains[r % lanes] = op(chains[r % lanes], slab)
        slabs = chains
        while len(slabs) > 1:
            slabs = [op(a, b) for a, b in zip(slabs[0::2], slabs[1::2])]
        return slabs[0]

    def score_chunk(kc, carry):
        kmax, kmin = carry
        k0 = pl.multiple_of(kc * ck, ck)
        ik = ik_ref[0, pl.ds(k0, ck), :]
        sc = jnp.zeros((ck, tq), jnp.float32)
        for h in range(IDX_HEADS):
            lg = _dot(ik, iqt_ref[0, h * IDX_DIM:(h + 1) * IDX_DIM, :])
            sc = sc + jnp.maximum(lg, 0.0) * w[h:h + 1, :]
        causal = k0 + srow <= tpos
        key = _float_key(jnp.where(causal, sc, -jnp.inf))
        keys_ref[pl.ds(k0, ck), :] = key
        return (jnp.maximum(kmax, fold(key, jnp.maximum)),
                jnp.minimum(kmin, fold(jnp.where(causal, key, int_max), jnp.minimum)))

    kmax, kmin = lax.fori_loop(0, nkc, score_chunk, (jnp.full((SUB, tq), int_min, jnp.int32),
                                                      jnp.full((SUB, tq), int_max, jnp.int32)))
    kmax = jnp.max(kmax, axis=0, keepdims=True)
    kmin = jnp.min(kmin, axis=0, keepdims=True)

    def count(pred):
        def body(kc, accs):
            k0 = pl.multiple_of(kc * ck, ck)
            accs = list(accs)
            chunk = keys_ref.at[pl.ds(k0, ck), :]
            for r in range(ck // SUB):
                k = chunk[r * SUB:(r + 1) * SUB, :]
                hit = pred(k, k0 + srow[r * SUB:(r + 1) * SUB]).astype(jnp.float32)
                accs[r % FOLD_CHAINS] = accs[r % FOLD_CHAINS] + hit
            return tuple(accs)
        accs = lax.fori_loop(0, nkc, body, (jnp.zeros((SUB, tq), jnp.float32),) * FOLD_CHAINS)
        acc = functools.reduce(jnp.add, accs)
        return jnp.sum(acc, axis=0, keepdims=True).astype(jnp.int32)

    k_eff = jnp.minimum(topk, tpos + 1)
    log_keff = jnp.log(k_eff.astype(jnp.float32))

    def unsettled(lo, hi, cnt_lo):
        return (cnt_lo != k_eff) & (hi > lo + 1)

    def candidate(st, kind, bisect_only):
        lo, hi, cnt_lo, cnt_hi = st
        key_mid = (lo >> 1) + (hi >> 1) + (lo & hi & 1)
        if kind == "key_mid":
            cand = key_mid
        elif kind in ("zero", "above_zero"):
            cand = jnp.full((1, tq), 0 if kind == "zero" else 1, jnp.int32)
        else:
            vlo, vhi = _key_float(lo), _key_float(hi)
            if kind == "interpolate":
                a = jnp.log(cnt_lo.astype(jnp.float32) + 0.5)
                b = jnp.log(cnt_hi.astype(jnp.float32) + 0.5)
                cand = _float_key(vlo + (vhi - vlo) * ((a - log_keff) / (a - b)))
            else:
                cand = _float_key(0.5 * vlo + 0.5 * vhi)
            cand = jnp.where(bisect_only, key_mid, cand)
        return jnp.clip(cand, lo + 1, hi - 1)

    def narrow(st, cand, cnt, upd):
        lo, hi, cnt_lo, cnt_hi = st
        up = upd & (cnt >= k_eff)
        dn = upd & (cnt < k_eff)
        return (jnp.where(up, cand, lo), jnp.where(dn, cand, hi),
                jnp.where(up, cnt, cnt_lo), jnp.where(dn, cnt, cnt_hi))

    def search_pass(st, kind, bisect_only):
        cand = candidate(st, kind, bisect_only)
        cnt = count(lambda k, sp: k >= cand)
        return narrow(st, cand, cnt, unsettled(st[0], st[1], st[2]))

    def run_rounds(one_pass, active, st):
        def cond(c):
            return jnp.max(active(c[1:]).astype(jnp.int32)) > 0

        def body(c):
            bisect_only = c[0] >= SEARCH_INTERP_ROUNDS
            st = c[1:]
            for kind in SEARCH_ROUND:
                st = one_pass(st, kind, bisect_only)
            return (c[0] + 1,) + st
        return lax.while_loop(cond, body, (jnp.int32(0),) + st)[1:]

    st = (kmin, kmax + 1, tpos + 1, jnp.zeros((1, tq), jnp.int32))
    for kind in SEARCH_OPENING:
        st = search_pass(st, kind, False)

    lo0, hi0 = st[0], st[1]
    span_bits = (pltpu.bitcast((hi0 - lo0).astype(jnp.float32), jnp.int32) >> 23) - 126
    shift = jnp.maximum(span_bits - COARSE_BITS, 0)
    coarse_max = 2 ** COARSE_BITS - 1

    def build_chunk(kc, carry):
        k0 = pl.multiple_of(kc * ck, ck)
        k = keys_ref[pl.ds(k0, ck), :]
        t = jnp.where(k < lo0, -1, jnp.where(k >= hi0, coarse_max, (k - lo0) >> shift))
        keys16_ref[pl.ds(k0, ck), :] = t.astype(jnp.int16)
        return carry

    lax.fori_loop(0, nkc, build_chunk, 0)
    pack = 2 * SUB

    def count16(cand16):
        def body(kc, accs):
            k0 = pl.multiple_of(kc * ck, ck)
            accs = list(accs)
            chunk = keys16_ref.at[pl.ds(k0, ck), :]
            for r in range(ck // pack):
                hit = (chunk[r * pack:(r + 1) * pack, :] >= cand16).astype(jnp.int16)
                accs[r % FOLD_CHAINS] = accs[r % FOLD_CHAINS] + hit
            return tuple(accs)
        accs = lax.fori_loop(0, nkc, body, (jnp.zeros((pack, tq), jnp.int16),) * FOLD_CHAINS)
        acc = functools.reduce(jnp.add, accs)
        return jnp.sum(acc.astype(jnp.int32), axis=0, keepdims=True)

    def coarse_range(st):
        return ((st[0] - lo0) >> shift) + 1, (st[1] - 1 - lo0) >> shift

    def coarse_active(st):
        lower, upper = coarse_range(st)
        return unsettled(st[0], st[1], st[2]) & (lower <= upper)

    def coarse_pass(st, kind, bisect_only):
        lower, upper = coarse_range(st)
        cand16 = jnp.clip((candidate(st, kind, bisect_only) - lo0) >> shift, lower, upper)
        cnt = count16(cand16.astype(jnp.int16))
        return narrow(st, lo0 + (cand16 << shift), cnt, coarse_active(st))

    st = run_rounds(coarse_pass, coarse_active, st)
    tau, hi, cnt_ge, cnt_gt = run_rounds(search_pass, lambda st: unsettled(st[0], st[1], st[2]), st)
    need = k_eff - cnt_gt
    overflow = cnt_ge > k_eff
    p_ref[...] = jnp.full((1, tq), s_len, jnp.int32)

    @pl.when(jnp.max(overflow.astype(jnp.int32)) > 0)
    def _():
        def pos_step(it, pos):
            cand = pos + jnp.left_shift(jnp.int32(1), (s_len.bit_length() - 1) - it)
            below = count(lambda k, sp: (k == tau) & (sp < cand))
            return jnp.where(below < need, cand, pos)
        pos = lax.fori_loop(0, s_len.bit_length(), pos_step, jnp.zeros((1, tq), jnp.int32))
        p_ref[...] = jnp.where(overflow, pos, s_len)

    last_eq = p_ref[...]

    def write_chunk(kc, carry):
        k0 = pl.multiple_of(kc * ck, ck)

        @pl.when(kc < nkc)
        def _():
            k = keys_ref[pl.ds(k0, ck), :]
            sel = (k > tau) | ((k == tau) & (k0 + srow <= last_eq))
            mask_ref[0, :, pl.ds(k0, ck)] = jnp.where(sel, 0.0, NEG).T.astype(mask_ref.dtype)

        @pl.when(kc >= nkc)
        def _():
            mask_ref[0, :, pl.ds(k0, ck)] = jnp.full((tq, ck), NEG, mask_ref.dtype)
        return carry

    lax.fori_loop(0, s_len // ck, write_chunk, 0)


def _indexer_mask(ik, iqt, iwt, topk):
    b, s, di = ik.shape
    ni = iqt.shape[1]
    tq, ck = 256, 512
    return pl.pallas_call(
        functools.partial(_indexer_kernel, tq=tq, ck=ck, topk=topk, s_len=s),
        grid=(b, s // tq),
        in_specs=[
            pl.BlockSpec((1, s, di), lambda bb, i: (bb, 0, 0)),
            pl.BlockSpec((1, ni, tq), lambda bb, i: (bb, 0, i)),
            pl.BlockSpec((1, IDX_HEADS, tq), lambda bb, i: (bb, 0, i)),
        ],
        out_specs=pl.BlockSpec((1, tq, s), lambda bb, i: (bb, i, 0)),
        out_shape=jax.ShapeDtypeStruct((b, s, s), jnp.bfloat16),
        scratch_shapes=[pltpu.VMEM((s, tq), jnp.int32), pltpu.VMEM((s, tq), jnp.int16),
                        pltpu.VMEM((1, tq), jnp.int32)],
        compiler_params=_cparams("parallel", "parallel"),
        name="indexer_topk",
    )(ik, iqt, iwt)


def _flash_update(s_cols, m_ref, l_ref, rows):
    m_old = m_ref[rows, :]
    m_cur = functools.reduce(jnp.maximum, s_cols)
    m_new = jnp.maximum(m_old, jnp.max(m_cur, axis=-1, keepdims=True))
    alpha = jnp.exp2(m_old - m_new)
    ps = [jnp.exp2(sc - m_new) for sc in s_cols]
    l_ref[rows, :] = alpha * l_ref[rows, :] + functools.reduce(jnp.add, ps)
    m_ref[rows, :] = m_new
    return jnp.concatenate(ps, axis=1).astype(MXU_DTYPE), alpha


def _tile_is_near(i, j, tq, tk):
    return (j + 1) * (tk // LANE) - i * (tq // LANE) >= 0


def _dispatch_tile(step, i, j, tq, tk):
    near = _tile_is_near(i, j, tq, tk)
    causal_cols = (i + 1) * tq - j * tk
    part = tk // DIAG_PARTS
    for w in range(1, DIAG_PARTS + 1):
        fits = causal_cols > (w - 1) * part
        if w < DIAG_PARTS:
            fits = fits & (causal_cols <= w * part)
        pl.when(near & fits)(functools.partial(step, True, w * part))
    pl.when(jnp.logical_not(near))(functools.partial(step, False, tk))


def _dsa_attn_kernel(itab, jtab, q_ref, wuk_ref, ckvt_ref, ckv_ref, mask_ref, nb_ref, wuv_ref,
                     o_ref, qlat_ref, m_ref, l_ref, acc_ref, *, tq, tk, hc):
    pidx = pl.program_id(1)
    i = itab[pidx]
    j = jtab[pidx]
    jlast = (i * tq + tq - 1) // tk
    heads = A_HEADS
    nchunk = heads // hc
    crows = hc * tq

    @pl.when(j == 0)
    def _():
        for h in range(heads):
            qh = q_ref[0, :, h * A_QK_DIM:(h + 1) * A_QK_DIM]
            ql = _dot(qh, wuk_ref[h]) * (A_QK_DIM ** -0.5 * LOG2E)
            qlat_ref[h * tq:(h + 1) * tq, :] = ql.astype(qlat_ref.dtype)
        m_ref[...] = jnp.full(m_ref.shape, M_INIT, jnp.float32)
        l_ref[...] = jnp.zeros(l_ref.shape, jnp.float32)
        acc_ref[...] = jnp.zeros(acc_ref.shape, jnp.float32)

    def step(with_bias, width):
        nsub = width // LANE
        ckvt = ckvt_ref[0, :, :width]
        ckv = ckv_ref[0, :width, :]
        maskf = mask_ref[0, :, :width].astype(jnp.float32)
        mcols = [maskf[:, c * LANE:(c + 1) * LANE] for c in range(nsub)]
        bidx = [_bias_index(i, j, c, tq, tk) for c in range(nsub)] if with_bias else None
        s_next = _dot(qlat_ref[0:crows, :], ckvt)
        for ci in range(nchunk):
            s_all = s_next
            if ci + 1 < nchunk:
                s_next = _dot(qlat_ref[(ci + 1) * crows:(ci + 2) * crows, :], ckvt)
            ps, alphas = [], []
            for hh in range(hc):
                h = ci * hc + hh
                cols = []
                for c in range(nsub):
                    add = mcols[c] + nb_ref[bidx[c], h] if with_bias else mcols[c]
                    cols.append(s_all[hh * tq:(hh + 1) * tq, c * LANE:(c + 1) * LANE] + add)
                p, alpha = _flash_update(cols, m_ref, l_ref, slice(h * tq, (h + 1) * tq))
                ps.append(p)
                alphas.append(alpha)
            pv = _dot(jnp.concatenate(ps, axis=0), ckv)
            a = jnp.concatenate(alphas, axis=0)
            rows = slice(ci * crows, (ci + 1) * crows)
            for half in range(A_KV_RANK // LANE):
                cs = slice(half * LANE, (half + 1) * LANE)
                acc_ref[rows, cs] = a * acc_ref[rows, cs] + pv[:, cs]

    _dispatch_tile(step, i, j, tq, tk)

    @pl.when(j == jlast)
    def _():
        for h in range(heads):
            rows = slice(h * tq, (h + 1) * tq)
            l = jnp.sum(l_ref[rows, :], axis=-1, keepdims=True)
            oh = _dot((acc_ref[rows, :] / l).astype(MXU_DTYPE), wuv_ref[h])
            o_ref[0, :, h * A_V_DIM:(h + 1) * A_V_DIM] = oh.astype(o_ref.dtype)


def _dsa_attention(q, ckv, ckvt, mask, nb, w_uk, w_uv):
    b, s, nq = q.shape
    tq, tk, hc = DSA_TQ, min(1024, s), 2
    itab, jtab = _causal_pairs(s, tq, tk)
    rows = A_HEADS * tq
    once = pl.Buffered(1)
    grid_spec = pltpu.PrefetchScalarGridSpec(
        num_scalar_prefetch=2,
        grid=(b, int(itab.shape[0])),
        in_specs=[
            pl.BlockSpec((1, tq, nq), lambda bb, p, it, jt: (bb, it[p], 0)),
            pl.BlockSpec(w_uk.shape, lambda bb, p, it, jt: (0, 0, 0), pipeline_mode=once),
            pl.BlockSpec((1, A_KV_RANK, tk), lambda bb, p, it, jt: (bb, 0, jt[p])),
            pl.BlockSpec((1, tk, A_KV_RANK), lambda bb, p, it, jt: (bb, jt[p], 0)),
            pl.BlockSpec((1, tq, tk), lambda bb, p, it, jt: (bb, it[p], jt[p])),
            pl.BlockSpec(nb.shape, lambda bb, p, it, jt: (0, 0, 0, 0), pipeline_mode=once),
            pl.BlockSpec(w_uv.shape, lambda bb, p, it, jt: (0, 0, 0), pipeline_mode=once),
        ],
        out_specs=pl.BlockSpec((1, tq, A_HEADS * A_V_DIM), lambda bb, p, it, jt: (bb, it[p], 0)),
        scratch_shapes=[
            pltpu.VMEM((rows, A_KV_RANK), MXU_DTYPE),
            pltpu.VMEM((rows, LANE), jnp.float32),
            pltpu.VMEM((rows, LANE), jnp.float32),
            pltpu.VMEM((rows, A_KV_RANK), jnp.float32),
        ],
    )
    return pl.pallas_call(
        functools.partial(_dsa_attn_kernel, tq=tq, tk=tk, hc=hc),
        grid_spec=grid_spec,
        out_shape=jax.ShapeDtypeStruct((b, s, A_HEADS * A_V_DIM), MXU_DTYPE),
        compiler_params=_cparams("parallel", "arbitrary"),
        name="dsa_attention",
    )(itab, jtab, q, w_uk, ckvt, ckv, mask, nb, w_uv)


def _proj_diff_kernel(x_ref, sc_ref, sh_ref, wq_ref, wk_ref, wv_ref, q_ref, k_ref, v_ref):
    h = (x_ref[0] * (1.0 + sc_ref[...]) + sh_ref[...]).astype(MXU_DTYPE)
    q_ref[0] = (_dot(h, wq_ref[...]) * (B_HEAD_DIM ** -0.5 * LOG2E)).astype(q_ref.dtype)
    k_ref[0] = _dot(h, wk_ref[...]).astype(k_ref.dtype)
    v_ref[0] = _dot(h, wv_ref[...]).astype(v_ref.dtype)


def _proj_diff(x, mod, layer, w_in):
    b, s, d = x.shape
    tm = 512
    n = w_in.shape[1] // 3
    ws = [w_in[:, k * n:(k + 1) * n].astype(MXU_DTYPE) for k in range(3)]
    row = lambda bb, t: (bb, t, 0)
    return pl.pallas_call(
        _proj_diff_kernel,
        grid=(b, s // tm),
        in_specs=[pl.BlockSpec((1, tm, d), row), _mod_spec(layer, 1, d), _mod_spec(layer, 0, d)]
        + [_const_spec((d, n))] * 3,
        out_specs=[pl.BlockSpec((1, tm, n), row)] * 3,
        out_shape=[jax.ShapeDtypeStruct((b, s, n), MXU_DTYPE)] * 3,
        compiler_params=_cparams("parallel", "parallel"),
        name="proj_diff",
    )(x, mod, mod, *ws)


def _diff_attn_kernel(itab, jtab, q_ref, kt_ref, v_ref, nb_ref, lam_ref, g_ref, o_ref,
                      m_ref, l_ref, acc_ref, *, tq, tk, lam_init):
    pidx = pl.program_id(1)
    i = itab[pidx]
    j = jtab[pidx]
    jlast = (i * tq + tq - 1) // tk
    dv = 2 * B_HEAD_DIM

    @pl.when(j == 0)
    def _():
        m_ref[...] = jnp.full(m_ref.shape, M_INIT, jnp.float32)
        l_ref[...] = jnp.zeros(l_ref.shape, jnp.float32)
        acc_ref[...] = jnp.zeros(acc_ref.shape, jnp.float32)

    def step(with_bias, width):
        nsub = width // LANE
        bidx = [_bias_index(i, j, c, tq, tk) for c in range(nsub)] if with_bias else None

        def logits(h):
            return [_dot(q_ref[0, 2 * h + mm], kt_ref[0, 2 * h + mm, :, :width]) for mm in range(2)]

        s_next = logits(0)
        for h in range(B_HEADS):
            s_pair = s_next
            if h + 1 < B_HEADS:
                s_next = logits(h + 1)
            ps, alphas = [], []
            for mm in range(2):
                col = 2 * h + mm
                cols = [s_pair[mm][:, c * LANE:(c + 1) * LANE] for c in range(nsub)]
                if with_bias:
                    cols = [cols[c] + nb_ref[bidx[c], col] for c in range(nsub)]
                p, alpha = _flash_update(cols, m_ref, l_ref, slice(col * tq, (col + 1) * tq))
                ps.append(p)
                alphas.append(alpha)
            pv = _dot(jnp.concatenate(ps, axis=0), v_ref[0, h, :width, :])
            rows = slice(2 * h * tq, (2 * h + 2) * tq)
            acc_ref[rows, :] = jnp.concatenate(alphas, axis=0) * acc_ref[rows, :] + pv

    _dispatch_tile(step, i, j, tq, tk)

    @pl.when(j == jlast)
    def _():
        lam = lam_ref[...]
        lam_full = (jnp.exp(jnp.sum(lam[0:1] * lam[1:2], axis=-1, keepdims=True))
                    - jnp.exp(jnp.sum(lam[2:3] * lam[3:4], axis=-1, keepdims=True)) + lam_init)
        for h in range(B_HEADS):
            r1 = slice(2 * h * tq, (2 * h + 1) * tq)
            r2 = slice((2 * h + 1) * tq, (2 * h + 2) * tq)
            l1 = jnp.sum(l_ref[r1, :], axis=-1, keepdims=True)
            l2 = jnp.sum(l_ref[r2, :], axis=-1, keepdims=True)
            o = acc_ref[r1, :] / l1 - lam_full * (acc_ref[r2, :] / l2)
            o = o * lax.rsqrt(jnp.mean(o * o, axis=-1, keepdims=True) + LN_EPS) * g_ref[...]
            o_ref[0, :, h * dv:(h + 1) * dv] = (o * (1.0 - lam_init)).astype(o_ref.dtype)


def _diff_attention(qm, ktm, vh, nb, lam, subln_g, lam_init):
    b, maps, s, dh = qm.shape
    tq, tk = DIFF_TQ, min(1024, s)
    dv = 2 * B_HEAD_DIM
    assert dv == LANE
    itab, jtab = _causal_pairs(s, tq, tk)
    once = pl.Buffered(1)
    grid_spec = pltpu.PrefetchScalarGridSpec(
        num_scalar_prefetch=2,
        grid=(b, int(itab.shape[0])),
        in_specs=[
            pl.BlockSpec((1, maps, tq, dh), lambda bb, p, it, jt: (bb, 0, it[p], 0)),
            pl.BlockSpec((1, maps, dh, tk), lambda bb, p, it, jt: (bb, 0, 0, jt[p])),
            pl.BlockSpec((1, B_HEADS, tk, dv), lambda bb, p, it, jt: (bb, 0, jt[p], 0)),
            pl.BlockSpec(nb.shape, lambda bb, p, it, jt: (0, 0, 0, 0), pipeline_mode=once),
            pl.BlockSpec(lam.shape, lambda bb, p, it, jt: (0, 0)),
            pl.BlockSpec((1, dv), lambda bb, p, it, jt: (0, 0)),
        ],
        out_specs=pl.BlockSpec((1, tq, B_HEADS * dv), lambda bb, p, it, jt: (bb, it[p], 0)),
        scratch_shapes=[
            pltpu.VMEM((maps * tq, LANE), jnp.float32),
            pltpu.VMEM((maps * tq, LANE), jnp.float32),
            pltpu.VMEM((maps * tq, dv), jnp.float32),
        ],
    )
    return pl.pallas_call(
        functools.partial(_diff_attn_kernel, tq=tq, tk=tk, lam_init=lam_init),
        grid_spec=grid_spec,
        out_shape=jax.ShapeDtypeStruct((b, s, B_HEADS * dv), MXU_DTYPE),
        compiler_params=_cparams("parallel", "arbitrary"),
        name="diff_attention",
    )(itab, jtab, qm, ktm, vh, nb, lam, subln_g.reshape(1, dv))


def _post_mixer_kernel(o_ref, x_ref, gt_ref, sc_ref, sh_ref, gc_ref, wo_ref, w1_ref, w2_ref,
                       lng_ref, lnb_ref, out_ref, *, tf, alpha):
    y = _dot(o_ref[0], wo_ref[...])
    x = _layer_norm_rows(alpha * x_ref[0] + (1.0 + gt_ref[...]) * y, lng_ref[0:1], lnb_ref[0:1])
    h = (x * (1.0 + sc_ref[...]) + sh_ref[...]).astype(MXU_DTYPE)
    y = jnp.zeros(x.shape, jnp.float32)
    for c in range(w1_ref.shape[1] // tf):
        a = jnp.maximum(_dot(h, w1_ref[:, c * tf:(c + 1) * tf]), 0.0)
        y = y + _dot((a * a).astype(MXU_DTYPE), w2_ref[c * tf:(c + 1) * tf, :])
    z = alpha * x + (1.0 + gc_ref[...]) * y
    out_ref[0] = _layer_norm_rows(z, lng_ref[1:2], lnb_ref[1:2])


def _post_mixer(o, x, mod, layer, w_o, w1, w2, ln_g, ln_b, alpha):
    b, s, d = x.shape
    dff = w1.shape[1]
    n_in = o.shape[-1]
    tm, tf = 512, 1024
    row = lambda bb, t: (bb, t, 0)
    once = pl.Buffered(1)
    return pl.pallas_call(
        functools.partial(_post_mixer_kernel, tf=tf, alpha=alpha),
        grid=(b, s // tm),
        in_specs=[
            pl.BlockSpec((1, tm, n_in), row), pl.BlockSpec((1, tm, d), row),
            _mod_spec(layer, 2, d), _mod_spec(layer, 4, d), _mod_spec(layer, 3, d),
            _mod_spec(layer, 5, d),
            pl.BlockSpec((n_in, d), lambda bb, t: (0, 0), pipeline_mode=once),
            pl.BlockSpec((d, dff), lambda bb, t: (0, 0), pipeline_mode=once),
            pl.BlockSpec((dff, d), lambda bb, t: (0, 0), pipeline_mode=once),
            _const_spec((2, d)), _const_spec((2, d)),
        ],
        out_specs=pl.BlockSpec((1, tm, d), row),
        out_shape=jax.ShapeDtypeStruct((b, s, d), jnp.float32),
        compiler_params=_cparams("parallel", "parallel"),
        name="post_mixer",
    )(o, x, mod, mod, mod, mod, w_o.astype(MXU_DTYPE), w1.astype(MXU_DTYPE),
      w2.astype(MXU_DTYPE), ln_g, ln_b)


def kernel(x, c, rel_bias, ada_w, ada_b, ln_g, ln_b, a_w_in, a_kv_norm, a_w_uk, a_w_uv, a_w_o,
           b_w_in, b_lambda, b_subln, b_w_o, mlp_w1, mlp_w2):
    depth = ada_w.shape[0]
    b, s, d = x.shape
    alpha = (2 * depth) ** 0.25
    topk = min(TOPK_MAX, s // 4)
    mod = _ada_mod(c, ada_w, ada_b)
    nb = {tq: _bias_tiles(rel_bias, tq) for tq in {DSA_TQ, DIFF_TQ}}
    nb_a, nb_b = nb[DSA_TQ], nb[DIFF_TQ]
    for i in range(depth):
        jm = i // N_MIXERS
        if i % N_MIXERS == 0:
            q, ckv, iq, ikw = _proj_dsa(x, mod, i, a_w_in[jm], a_kv_norm[jm])
            ik = ikw[..., :IDX_DIM].astype(MXU_DTYPE)
            iwt = jnp.swapaxes(ikw[..., IDX_DIM:IDX_DIM + IDX_HEADS], 1, 2)
            mask = _indexer_mask(ik, jnp.swapaxes(iq, 1, 2), iwt, topk)
            o = _dsa_attention(q, ckv, jnp.swapaxes(ckv, 1, 2), mask, nb_a,
                               a_w_uk[jm].astype(MXU_DTYPE), a_w_uv[jm].astype(MXU_DTYPE))
            w_o = a_w_o[jm]
        else:
            lam_init = 0.8 - 0.6 * math.exp(-0.3 * i)
            qf, kf, vf = _proj_diff(x, mod, i, b_w_in[jm])
            qm = jnp.transpose(qf.reshape(b, s, B_MAPS, B_HEAD_DIM), (0, 2, 1, 3))
            ktm = jnp.transpose(kf.reshape(b, s, B_MAPS, B_HEAD_DIM), (0, 2, 3, 1))
            vh = jnp.transpose(vf.reshape(b, s, B_HEADS, 2 * B_HEAD_DIM), (0, 2, 1, 3))
            o = _diff_attention(qm, ktm, vh, nb_b, b_lambda[jm], b_subln[jm], lam_init)
            w_o = b_w_o[jm]
        x = _post_mixer(o, x, mod, i, w_o, mlp_w1[i], mlp_w2[i], ln_g[i], ln_b[i], alpha)
    return x
```

```python
import functools
import math

import jax
import jax.numpy as jnp
from jax import lax
from jax.experimental import pallas as pl
from jax.experimental.pallas import tpu as pltpu

N_MIXERS = 2
A_HEADS = 16
A_QK_DIM = 64
A_V_DIM = 64
A_KV_RANK = 256
IDX_HEADS = 8
IDX_DIM = 64
TOPK_MAX = 256
B_HEADS = 8
B_HEAD_DIM = 64
B_MAPS = 2 * B_HEADS
REL_BUCKETS = 32
REL_MAX_DIST = 128
LN_EPS = 1e-5
NEG = -1e30
M_INIT = 0.5 * NEG
LOG2E = math.log2(math.e)
SEARCH_ROUND = ("interpolate", "value_mid", "interpolate", "key_mid")
SEARCH_OPENING = ("zero", "above_zero")
SEARCH_INTERP_ROUNDS = 6
FOLD_CHAINS = 8

LANE = 128
SUB = 8
VMEM_LIMIT = 56 * 1024 * 1024
DSA_TQ = 256
DIFF_TQ = 256
DIAG_PARTS = 4

MXU_DTYPE = jnp.bfloat16


def _dot(a, b):
    return jnp.dot(a, b, preferred_element_type=jnp.float32)


def _cparams(*sem):
    return pltpu.CompilerParams(dimension_semantics=sem, vmem_limit_bytes=VMEM_LIMIT)


def _const_spec(shape):
    nd = len(shape)
    return pl.BlockSpec(shape, lambda *_: (0,) * nd)


def _layer_norm_rows(z, g, b):
    mu = jnp.mean(z, axis=-1, keepdims=True)
    zc = z - mu
    var = jnp.mean(zc * zc, axis=-1, keepdims=True)
    return zc * lax.rsqrt(var + LN_EPS) * g + b


def _ada_kernel(ct_ref, w_ref, b_ref, o_ref, *, nb):
    ct = ct_ref[...]
    s = ct * jax.nn.sigmoid(ct)
    w = w_ref[0]
    rows = [jnp.sum(w * s[:, b:b + 1], axis=0, keepdims=True) for b in range(nb)]
    o_ref[0] = jnp.concatenate(rows, axis=0) + b_ref[0]


def _ada_mod(c, ada_w, ada_b):
    depth, d, n = ada_w.shape
    nb = c.shape[0]
    tn = 768
    out = pl.pallas_call(
        functools.partial(_ada_kernel, nb=nb),
        grid=(depth, n // tn),
        in_specs=[
            pl.BlockSpec((d, nb), lambda l, j: (0, 0)),
            pl.BlockSpec((1, d, tn), lambda l, j: (l, 0, j)),
            pl.BlockSpec((1, 1, tn), lambda l, j: (l, 0, j)),
        ],
        out_specs=pl.BlockSpec((1, nb, tn), lambda l, j: (l, 0, j)),
        out_shape=jax.ShapeDtypeStruct((depth, nb, n), jnp.float32),
        compiler_params=_cparams("arbitrary", "arbitrary"),
        name="ada_mod",
    )(c.T, ada_w, ada_b.reshape(depth, 1, n))
    return out.reshape(depth, nb, 6, 1, d)


def _mod_spec(layer, k, d):
    return pl.BlockSpec((None, None, None, 1, d), lambda b, *_: (layer, b, k, 0, 0))


def _rel_bucket(dist):
    n = jnp.maximum(dist, 0)
    max_exact = REL_BUCKETS // 2
    nf = jnp.maximum(n, 1).astype(jnp.float32)
    large = max_exact + (jnp.log(nf / max_exact) / math.log(REL_MAX_DIST / max_exact)
                         * (REL_BUCKETS - max_exact)).astype(jnp.int32)
    large = jnp.minimum(large, REL_BUCKETS - 1)
    return jnp.where(n < max_exact, n, large)


def _bias_tile_kernel(tab_ref, bk_ref, o_ref, *, heads):
    bk = bk_ref[0]
    for h in range(heads):
        far = tab_ref[REL_BUCKETS - 1, h]
        acc = jnp.zeros(bk.shape, jnp.float32)
        for b in range(REL_BUCKETS - 1):
            acc = jnp.where(bk == b, (tab_ref[b, h] - far) * LOG2E, acc)
        o_ref[0, h] = jnp.where(bk < 0, NEG, acc)


def _bias_tiles(rel_bias, tq):
    heads = rel_bias.shape[1]
    r_sub = tq // LANE
    rels = jnp.arange(-2, r_sub + 1, dtype=jnp.int32)
    ql = jnp.arange(tq, dtype=jnp.int32)[None, :, None]
    sl = jnp.arange(LANE, dtype=jnp.int32)[None, None, :]
    dist = ql - rels[:, None, None] * LANE - sl
    bk = jnp.where(dist < 0, -1, _rel_bucket(dist))
    nk = r_sub + 3
    return pl.pallas_call(
        functools.partial(_bias_tile_kernel, heads=heads),
        grid=(nk,),
        in_specs=[
            pl.BlockSpec(memory_space=pltpu.SMEM),
            pl.BlockSpec((1, tq, LANE), lambda k: (k, 0, 0)),
        ],
        out_specs=pl.BlockSpec((1, heads, tq, LANE), lambda k: (k, 0, 0, 0)),
        out_shape=jax.ShapeDtypeStruct((nk, heads, tq, LANE), jnp.float32),
        compiler_params=_cparams("arbitrary"),
        name="bias_tiles",
    )(rel_bias, bk)


def _bias_index(i, j, c, tq, tk):
    rel = j * (tk // LANE) + c - i * (tq // LANE)
    return jnp.clip(rel + 2, 0, tq // LANE + 2)


def _causal_pairs(s, tq, tk):
    ii, jj = [], []
    for i in range(s // tq):
        for j in range((i * tq + tq - 1) // tk + 1):
            ii.append(i)
            jj.append(j)
    return jnp.asarray(ii, jnp.int32), jnp.asarray(jj, jnp.int32)


def _proj_dsa_kernel(x_ref, sc_ref, sh_ref, wq_ref, wc_ref, wi_ref, wk_ref, kvg_ref,
                     q_ref, ckv_ref, ckvt_ref, iqt_ref, ik_ref, iwt_ref):
    h = (x_ref[0] * (1.0 + sc_ref[...]) + sh_ref[...]).astype(MXU_DTYPE)
    q_ref[0] = _dot(h, wq_ref[...]).astype(q_ref.dtype)
    ckv = _dot(h, wc_ref[...])
    ckv = ckv * lax.rsqrt(jnp.mean(ckv * ckv, axis=-1, keepdims=True) + LN_EPS) * kvg_ref[...]
    ckv_ref[0] = ckv.astype(ckv_ref.dtype)
    ckvt_ref[0] = ckv.T.astype(ckvt_ref.dtype)
    iqt_ref[0] = _dot(h, wi_ref[...]).T.astype(iqt_ref.dtype)
    ikw = _dot(h, wk_ref[...])
    ik_ref[0] = ikw[:, :IDX_DIM].astype(ik_ref.dtype)
    iwt_ref[0] = ikw.T[IDX_DIM:IDX_DIM + IDX_HEADS, :]


def _proj_dsa(x, mod, layer, w_in, kv_norm):
    b, s, d = x.shape
    tm = 512
    nq = A_HEADS * A_QK_DIM
    ni = IDX_HEADS * IDX_DIM
    o1, o2, o3 = nq, nq + A_KV_RANK, nq + A_KV_RANK + ni
    wq = w_in[:, :o1].astype(MXU_DTYPE)
    wc = w_in[:, o1:o2].astype(MXU_DTYPE)
    wi = w_in[:, o2:o3].astype(MXU_DTYPE)
    wk = jnp.pad(w_in[:, o3:], ((0, 0), (0, LANE - (w_in.shape[1] - o3)))).astype(MXU_DTYPE)
    row = lambda bb, t: (bb, t, 0)
    col = lambda bb, t: (bb, 0, t)
    return pl.pallas_call(
        _proj_dsa_kernel,
        grid=(b, s // tm),
        in_specs=[
            pl.BlockSpec((1, tm, d), row),
            _mod_spec(layer, 1, d), _mod_spec(layer, 0, d),
            _const_spec((d, nq)), _const_spec((d, A_KV_RANK)), _const_spec((d, ni)),
            _const_spec((d, LANE)), _const_spec((1, A_KV_RANK)),
        ],
        out_specs=[
            pl.BlockSpec((1, tm, nq), row), pl.BlockSpec((1, tm, A_KV_RANK), row),
            pl.BlockSpec((1, A_KV_RANK, tm), col), pl.BlockSpec((1, ni, tm), col),
            pl.BlockSpec((1, tm, IDX_DIM), row), pl.BlockSpec((1, IDX_HEADS, tm), col),
        ],
        out_shape=[
            jax.ShapeDtypeStruct((b, s, nq), MXU_DTYPE),
            jax.ShapeDtypeStruct((b, s, A_KV_RANK), MXU_DTYPE),
            jax.ShapeDtypeStruct((b, A_KV_RANK, s), MXU_DTYPE),
            jax.ShapeDtypeStruct((b, ni, s), MXU_DTYPE),
            jax.ShapeDtypeStruct((b, s, IDX_DIM), MXU_DTYPE),
            jax.ShapeDtypeStruct((b, IDX_HEADS, s), jnp.float32),
        ],
        compiler_params=_cparams("parallel", "parallel"),
        name="proj_dsa",
    )(x, mod, mod, wq, wc, wi, wk, kv_norm.reshape(1, A_KV_RANK))


def _float_key(v):
    bits = pltpu.bitcast(v, jnp.int32)
    return bits ^ ((bits >> 31) & 0x7FFFFFFF)


def _key_float(k):
    return pltpu.bitcast(k ^ ((k >> 31) & 0x7FFFFFFF), jnp.float32)


def _indexer_kernel(ik_ref, iqt_ref, iwt_ref, mask_ref, keys_ref, p_ref, *, tq, ck, topk, s_len):
    i = pl.program_id(1)
    q0 = i * tq
    nkc = (q0 + tq + ck - 1) // ck
    w = iwt_ref[0] * (IDX_HEADS ** -0.5 * IDX_DIM ** -0.5)
    tpos = q0 + lax.broadcasted_iota(jnp.int32, (1, tq), 1)
    srow = lax.broadcasted_iota(jnp.int32, (ck, 1), 0)
    int_max = jnp.int32(2 ** 31 - 1)
    int_min = jnp.int32(-2 ** 31)

    def fold(x, op):
        slabs = [x[r * SUB:(r + 1) * SUB] for r in range(ck // SUB)]
        lanes = FOLD_CHAINS
        chains = slabs[:lanes]
        for r, slab in enumerate(slabs[lanes:]):
            chains[r % lanes] = op(chains[r % lanes], slab)
        slabs = chains
        while len(slabs) > 1:
            slabs = [op(a, b) for a, b in zip(slabs[0::2], slabs[1::2])]
        return slabs[0]

    def score_chunk(kc, carry):
        kmax, kmin = carry
        k0 = pl.multiple_of(kc * ck, ck)
        ik = ik_ref[0, pl.ds(k0, ck), :]
        sc = jnp.zeros((ck, tq), jnp.float32)
        for h in range(IDX_HEADS):
            lg = _dot(ik, iqt_ref[0, h * IDX_DIM:(h + 1) * IDX_DIM, :])
            sc = sc + jnp.maximum(lg, 0.0) * w[h:h + 1, :]
        causal = k0 + srow <= tpos
        key = _float_key(jnp.where(causal, sc, -jnp.inf))
        keys_ref[pl.ds(k0, ck), :] = key
        return (jnp.maximum(kmax, fold(key, jnp.maximum)),
                jnp.minimum(kmin, fold(jnp.where(causal, key, int_max), jnp.minimum)))

    kmax, kmin = lax.fori_loop(0, nkc, score_chunk, (jnp.full((SUB, tq), int_min, jnp.int32),
                                                      jnp.full((SUB, tq), int_max, jnp.int32)))
    kmax = jnp.max(kmax, axis=0, keepdims=True)
    kmin = jnp.min(kmin, axis=0, keepdims=True)

    def count(pred):
        def body(kc, accs):
            k0 = pl.multiple_of(kc * ck, ck)
            accs = list(accs)
            chunk = keys_ref.at[pl.ds(k0, ck), :]
            for r in range(ck // SUB):
                k = chunk[r * SUB:(r + 1) * SUB, :]
                hit = pred(k, k0 + srow[r * SUB:(r + 1) * SUB]).astype(jnp.float32)
                accs[r % FOLD_CHAINS] = accs[r % FOLD_CHAINS] + hit
            return tuple(accs)
        accs = lax.fori_loop(0, nkc, body, (jnp.zeros((SUB, tq), jnp.float32),) * FOLD_CHAINS)
        acc = functools.reduce(jnp.add, accs)
        return jnp.sum(acc, axis=0, keepdims=True).astype(jnp.int32)

    k_eff = jnp.minimum(topk, tpos + 1)
    log_keff = jnp.log(k_eff.astype(jnp.float32))

    def unsettled(lo, hi, cnt_lo):
        return (cnt_lo != k_eff) & (hi > lo + 1)

    def candidate(st, kind, bisect_only):
        lo, hi, cnt_lo, cnt_hi = st
        key_mid = (lo >> 1) + (hi >> 1) + (lo & hi & 1)
        if kind == "key_mid":
            cand = key_mid
        elif kind in ("zero", "above_zero"):
            cand = jnp.full((1, tq), 0 if kind == "zero" else 1, jnp.int32)
        else:
            vlo, vhi = _key_float(lo), _key_float(hi)
            if kind == "interpolate":
                a = jnp.log(cnt_lo.astype(jnp.float32) + 0.5)
                b = jnp.log(cnt_hi.astype(jnp.float32) + 0.5)
                cand = _float_key(vlo + (vhi - vlo) * ((a - log_keff) / (a - b)))
            else:
                cand = _float_key(0.5 * vlo + 0.5 * vhi)
            cand = jnp.where(bisect_only, key_mid, cand)
        return jnp.clip(cand, lo + 1, hi - 1)

    def narrow(st, cand, cnt, upd):
        lo, hi, cnt_lo, cnt_hi = st
        up = upd & (cnt >= k_eff)
        dn = upd & (cnt < k_eff)
        return (jnp.where(up, cand, lo), jnp.where(dn, cand, hi),
                jnp.where(up, cnt, cnt_lo), jnp.where(dn, cnt, cnt_hi))

    def search_pass(st, kind, bisect_only):
        cand = candidate(st, kind, bisect_only)
        cnt = count(lambda k, sp: k >= cand)
        return narrow(st, cand, cnt, unsettled(st[0], st[1], st[2]))

    def run_rounds(one_pass, active, st):
        def cond(c):
            return jnp.max(active(c[1:]).astype(jnp.int32)) > 0

        def body(c):
            bisect_only = c[0] >= SEARCH_INTERP_ROUNDS
            st = c[1:]
            for kind in SEARCH_ROUND:
                st = one_pass(st, kind, bisect_only)
            return (c[0] + 1,) + st
        return lax.while_loop(cond, body, (jnp.int32(0),) + st)[1:]

    st = (kmin, kmax + 1, tpos + 1, jnp.zeros((1, tq), jnp.int32))
    for kind in SEARCH_OPENING:
        st = search_pass(st, kind, False)

    tau, hi, cnt_ge, cnt_gt = run_rounds(search_pass, lambda st: unsettled(st[0], st[1], st[2]), st)
    need = k_eff - cnt_gt
    overflow = cnt_ge > k_eff
    p_ref[...] = jnp.full((1, tq), s_len, jnp.int32)

    @pl.when(jnp.max(overflow.astype(jnp.int32)) > 0)
    def _():
        def pos_step(it, pos):
            cand = pos + jnp.left_shift(jnp.int32(1), (s_len.bit_length() - 1) - it)
            below = count(lambda k, sp: (k == tau) & (sp < cand))
            return jnp.where(below < need, cand, pos)
        pos = lax.fori_loop(0, s_len.bit_length(), pos_step, jnp.zeros((1, tq), jnp.int32))
        p_ref[...] = jnp.where(overflow, pos, s_len)

    last_eq = p_ref[...]

    def write_chunk(kc, carry):
        k0 = pl.multiple_of(kc * ck, ck)

        @pl.when(kc < nkc)
        def _():
            k = keys_ref[pl.ds(k0, ck), :]
            sel = (k > tau) | ((k == tau) & (k0 + srow <= last_eq))
            mask_ref[0, :, pl.ds(k0, ck)] = jnp.where(sel, 0.0, NEG).T.astype(mask_ref.dtype)

        @pl.when(kc >= nkc)
        def _():
            mask_ref[0, :, pl.ds(k0, ck)] = jnp.full((tq, ck), NEG, mask_ref.dtype)
        return carry

    lax.fori_loop(0, s_len // ck, write_chunk, 0)


def _indexer_mask(ik, iqt, iwt, topk):
    b, s, di = ik.shape
    ni = iqt.shape[1]
    tq, ck = 256, 512
    return pl.pallas_call(
        functools.partial(_indexer_kernel, tq=tq, ck=ck, topk=topk, s_len=s),
        grid=(b, s // tq),
        in_specs=[
            pl.BlockSpec((1, s, di), lambda bb, i: (bb, 0, 0)),
            pl.BlockSpec((1, ni, tq), lambda bb, i: (bb, 0, i)),
            pl.BlockSpec((1, IDX_HEADS, tq), lambda bb, i: (bb, 0, i)),
        ],
        out_specs=pl.BlockSpec((1, tq, s), lambda bb, i: (bb, i, 0)),
        out_shape=jax.ShapeDtypeStruct((b, s, s), jnp.bfloat16),
        scratch_shapes=[pltpu.VMEM((s, tq), jnp.int32), pltpu.VMEM((1, tq), jnp.int32)],
        compiler_params=_cparams("parallel", "parallel"),
        name="indexer_topk",
    )(ik, iqt, iwt)


def _flash_update(s_cols, m_ref, l_ref, rows):
    m_old = m_ref[rows, :]
    m_cur = functools.reduce(jnp.maximum, s_cols)
    m_new = jnp.maximum(m_old, jnp.max(m_cur, axis=-1, keepdims=True))
    alpha = jnp.exp2(m_old - m_new)
    ps = [jnp.exp2(sc - m_new) for sc in s_cols]
    l_ref[rows, :] = alpha * l_ref[rows, :] + functools.reduce(jnp.add, ps)
    m_ref[rows, :] = m_new
    return jnp.concatenate(ps, axis=1).astype(MXU_DTYPE), alpha


def _tile_is_near(i, j, tq, tk):
    return (j + 1) * (tk // LANE) - i * (tq // LANE) >= 0


def _dispatch_tile(step, i, j, tq, tk):
    near = _tile_is_near(i, j, tq, tk)
    causal_cols = (i + 1) * tq - j * tk
    part = tk // DIAG_PARTS
    for w in range(1, DIAG_PARTS + 1):
        fits = causal_cols > (w - 1) * part
        if w < DIAG_PARTS:
            fits = fits & (causal_cols <= w * part)
        pl.when(near & fits)(functools.partial(step, True, w * part))
    pl.when(jnp.logical_not(near))(functools.partial(step, False, tk))


def _dsa_attn_kernel(itab, jtab, q_ref, wuk_ref, ckvt_ref, ckv_ref, mask_ref, nb_ref, wuv_ref,
                     o_ref, qlat_ref, m_ref, l_ref, acc_ref, *, tq, tk, hc):
    pidx = pl.program_id(1)
    i = itab[pidx]
    j = jtab[pidx]
    jlast = (i * tq + tq - 1) // tk
    heads = A_HEADS
    nchunk = heads // hc
    crows = hc * tq

    @pl.when(j == 0)
    def _():
        for h in range(heads):
            qh = q_ref[0, :, h * A_QK_DIM:(h + 1) * A_QK_DIM]
            ql = _dot(qh, wuk_ref[h]) * (A_QK_DIM ** -0.5 * LOG2E)
            qlat_ref[h * tq:(h + 1) * tq, :] = ql.astype(qlat_ref.dtype)
        m_ref[...] = jnp.full(m_ref.shape, M_INIT, jnp.float32)
        l_ref[...] = jnp.zeros(l_ref.shape, jnp.float32)
        acc_ref[...] = jnp.zeros(acc_ref.shape, jnp.float32)

    def step(with_bias, width):
        nsub = width // LANE
        ckvt = ckvt_ref[0, :, :width]
        ckv = ckv_ref[0, :width, :]
        maskf = mask_ref[0, :, :width].astype(jnp.float32)
        mcols = [maskf[:, c * LANE:(c + 1) * LANE] for c in range(nsub)]
        bidx = [_bias_index(i, j, c, tq, tk) for c in range(nsub)] if with_bias else None
        s_next = _dot(qlat_ref[0:crows, :], ckvt)
        for ci in range(nchunk):
            s_all = s_next
            if ci + 1 < nchunk:
                s_next = _dot(qlat_ref[(ci + 1) * crows:(ci + 2) * crows, :], ckvt)
            ps, alphas = [], []
            for hh in range(hc):
                h = ci * hc + hh
                cols = []
                for c in range(nsub):
                    add = mcols[c] + nb_ref[bidx[c], h] if with_bias else mcols[c]
                    cols.append(s_all[hh * tq:(hh + 1) * tq, c * LANE:(c + 1) * LANE] + add)
                p, alpha = _flash_update(cols, m_ref, l_ref, slice(h * tq, (h + 1) * tq))
                ps.append(p)
                alphas.append(alpha)
            pv = _dot(jnp.concatenate(ps, axis=0), ckv)
            a = jnp.concatenate(alphas, axis=0)
            rows = slice(ci * crows, (ci + 1) * crows)
            for half in range(A_KV_RANK // LANE):
                cs = slice(half * LANE, (half + 1) * LANE)
                acc_ref[rows, cs] = a * acc_ref[rows, cs] + pv[:, cs]

    _dispatch_tile(step, i, j, tq, tk)

    @pl.when(j == jlast)
    def _():
        for h in range(heads):
            rows = slice(h * tq, (h + 1) * tq)
            l = jnp.sum(l_ref[rows, :], axis=-1, keepdims=True)
            oh = _dot((acc_ref[rows, :] / l).astype(MXU_DTYPE), wuv_ref[h])
            o_ref[0, :, h * A_V_DIM:(h + 1) * A_V_DIM] = oh.astype(o_ref.dtype)


def _dsa_attention(q, ckv, ckvt, mask, nb, w_uk, w_uv):
    b, s, nq = q.shape
    tq, tk, hc = DSA_TQ, min(1024, s), 2
    itab, jtab = _causal_pairs(s, tq, tk)
    rows = A_HEADS * tq
    once = pl.Buffered(1)
    grid_spec = pltpu.PrefetchScalarGridSpec(
        num_scalar_prefetch=2,
        grid=(b, int(itab.shape[0])),
        in_specs=[
            pl.BlockSpec((1, tq, nq), lambda bb, p, it, jt: (bb, it[p], 0)),
            pl.BlockSpec(w_uk.shape, lambda bb, p, it, jt: (0, 0, 0), pipeline_mode=once),
            pl.BlockSpec((1, A_KV_RANK, tk), lambda bb, p, it, jt: (bb, 0, jt[p])),
            pl.BlockSpec((1, tk, A_KV_RANK), lambda bb, p, it, jt: (bb, jt[p], 0)),
            pl.BlockSpec((1, tq, tk), lambda bb, p, it, jt: (bb, it[p], jt[p])),
            pl.BlockSpec(nb.shape, lambda bb, p, it, jt: (0, 0, 0, 0), pipeline_mode=once),
            pl.BlockSpec(w_uv.shape, lambda bb, p, it, jt: (0, 0, 0), pipeline_mode=once),
        ],
        out_specs=pl.BlockSpec((1, tq, A_HEADS * A_V_DIM), lambda bb, p, it, jt: (bb, it[p], 0)),
        scratch_shapes=[
            pltpu.VMEM((rows, A_KV_RANK), MXU_DTYPE),
            pltpu.VMEM((rows, LANE), jnp.float32),
            pltpu.VMEM((rows, LANE), jnp.float32),
            pltpu.VMEM((rows, A_KV_RANK), jnp.float32),
        ],
    )
    return pl.pallas_call(
        functools.partial(_dsa_attn_kernel, tq=tq, tk=tk, hc=hc),
        grid_spec=grid_spec,
        out_shape=jax.ShapeDtypeStruct((b, s, A_HEADS * A_V_DIM), MXU_DTYPE),
        compiler_params=_cparams("parallel", "arbitrary"),
        name="dsa_attention",
    )(itab, jtab, q, w_uk, ckvt, ckv, mask, nb, w_uv)


def _proj_diff_kernel(x_ref, sc_ref, sh_ref, wq_ref, wk_ref, wv_ref, q_ref, kt_ref, v_ref):
    h = (x_ref[0] * (1.0 + sc_ref[...]) + sh_ref[...]).astype(MXU_DTYPE)
    q = _dot(h, wq_ref[...]) * (B_HEAD_DIM ** -0.5 * LOG2E)
    kt = _dot(h, wk_ref[...]).T
    v = _dot(h, wv_ref[...])
    for m in range(B_MAPS):
        q_ref[0, m] = q[:, m * B_HEAD_DIM:(m + 1) * B_HEAD_DIM].astype(q_ref.dtype)
        kt_ref[0, m] = kt[m * B_HEAD_DIM:(m + 1) * B_HEAD_DIM, :].astype(kt_ref.dtype)
    dv = 2 * B_HEAD_DIM
    for hd in range(B_HEADS):
        v_ref[0, hd] = v[:, hd * dv:(hd + 1) * dv].astype(v_ref.dtype)


def _proj_diff(x, mod, layer, w_in):
    b, s, d = x.shape
    tm = 512
    n = w_in.shape[1] // 3
    dv = 2 * B_HEAD_DIM
    ws = [w_in[:, k * n:(k + 1) * n].astype(MXU_DTYPE) for k in range(3)]
    return pl.pallas_call(
        _proj_diff_kernel,
        grid=(b, s // tm),
        in_specs=[pl.BlockSpec((1, tm, d), lambda bb, t: (bb, t, 0)),
                  _mod_spec(layer, 1, d), _mod_spec(layer, 0, d)] + [_const_spec((d, n))] * 3,
        out_specs=[
            pl.BlockSpec((1, B_MAPS, tm, B_HEAD_DIM), lambda bb, t: (bb, 0, t, 0)),
            pl.BlockSpec((1, B_MAPS, B_HEAD_DIM, tm), lambda bb, t: (bb, 0, 0, t)),
            pl.BlockSpec((1, B_HEADS, tm, dv), lambda bb, t: (bb, 0, t, 0)),
        ],
        out_shape=[
            jax.ShapeDtypeStruct((b, B_MAPS, s, B_HEAD_DIM), MXU_DTYPE),
            jax.ShapeDtypeStruct((b, B_MAPS, B_HEAD_DIM, s), MXU_DTYPE),
            jax.ShapeDtypeStruct((b, B_HEADS, s, dv), MXU_DTYPE),
        ],
        compiler_params=_cparams("parallel", "parallel"),
        name="proj_diff",
    )(x, mod, mod, *ws)


def _diff_attn_kernel(itab, jtab, q_ref, kt_ref, v_ref, nb_ref, lam_ref, g_ref, o_ref,
                      m_ref, l_ref, acc_ref, *, tq, tk, lam_init):
    pidx = pl.program_id(1)
    i = itab[pidx]
    j = jtab[pidx]
    jlast = (i * tq + tq - 1) // tk
    dv = 2 * B_HEAD_DIM

    @pl.when(j == 0)
    def _():
        m_ref[...] = jnp.full(m_ref.shape, M_INIT, jnp.float32)
        l_ref[...] = jnp.zeros(l_ref.shape, jnp.float32)
        acc_ref[...] = jnp.zeros(acc_ref.shape, jnp.float32)

    def step(with_bias, width):
        nsub = width // LANE
        bidx = [_bias_index(i, j, c, tq, tk) for c in range(nsub)] if with_bias else None

        def logits(h):
            return [_dot(q_ref[0, 2 * h + mm], kt_ref[0, 2 * h + mm, :, :width]) for mm in range(2)]

        s_next = logits(0)
        for h in range(B_HEADS):
            s_pair = s_next
            if h + 1 < B_HEADS:
                s_next = logits(h + 1)
            ps, alphas = [], []
            for mm in range(2):
                col = 2 * h + mm
                cols = [s_pair[mm][:, c * LANE:(c + 1) * LANE] for c in range(nsub)]
                if with_bias:
                    cols = [cols[c] + nb_ref[bidx[c], col] for c in range(nsub)]
                p, alpha = _flash_update(cols, m_ref, l_ref, slice(col * tq, (col + 1) * tq))
                ps.append(p)
                alphas.append(alpha)
            pv = _dot(jnp.concatenate(ps, axis=0), v_ref[0, h, :width, :])
            rows = slice(2 * h * tq, (2 * h + 2) * tq)
            acc_ref[rows, :] = jnp.concatenate(alphas, axis=0) * acc_ref[rows, :] + pv

    _dispatch_tile(step, i, j, tq, tk)

    @pl.when(j == jlast)
    def _():
        lam = lam_ref[...]
        lam_full = (jnp.exp(jnp.sum(lam[0:1] * lam[1:2], axis=-1, keepdims=True))
                    - jnp.exp(jnp.sum(lam[2:3] * lam[3:4], axis=-1, keepdims=True)) + lam_init)
        for h in range(B_HEADS):
            r1 = slice(2 * h * tq, (2 * h + 1) * tq)
            r2 = slice((2 * h + 1) * tq, (2 * h + 2) * tq)
            l1 = jnp.sum(l_ref[r1, :], axis=-1, keepdims=True)
            l2 = jnp.sum(l_ref[r2, :], axis=-1, keepdims=True)
            o = acc_ref[r1, :] / l1 - lam_full * (acc_ref[r2, :] / l2)
            o = o * lax.rsqrt(jnp.mean(o * o, axis=-1, keepdims=True) + LN_EPS) * g_ref[...]
            o_ref[0, :, h * dv:(h + 1) * dv] = (o * (1.0 - lam_init)).astype(o_ref.dtype)


def _diff_attention(qm, ktm, vh, nb, lam, subln_g, lam_init):
    b, maps, s, dh = qm.shape
    tq, tk = DIFF_TQ, min(1024, s)
    dv = 2 * B_HEAD_DIM
    assert dv == LANE
    itab, jtab = _causal_pairs(s, tq, tk)
    once = pl.Buffered(1)
    grid_spec = pltpu.PrefetchScalarGridSpec(
        num_scalar_prefetch=2,
        grid=(b, int(itab.shape[0])),
        in_specs=[
            pl.BlockSpec((1, maps, tq, dh), lambda bb, p, it, jt: (bb, 0, it[p], 0)),
            pl.BlockSpec((1, maps, dh, tk), lambda bb, p, it, jt: (bb, 0, 0, jt[p])),
            pl.BlockSpec((1, B_HEADS, tk, dv), lambda bb, p, it, jt: (bb, 0, jt[p], 0)),
            pl.BlockSpec(nb.shape, lambda bb, p, it, jt: (0, 0, 0, 0), pipeline_mode=once),
            pl.BlockSpec(lam.shape, lambda bb, p, it, jt: (0, 0)),
            pl.BlockSpec((1, dv), lambda bb, p, it, jt: (0, 0)),
        ],
        out_specs=pl.BlockSpec((1, tq, B_HEADS * dv), lambda bb, p, it, jt: (bb, it[p], 0)),
        scratch_shapes=[
            pltpu.VMEM((maps * tq, LANE), jnp.float32),
            pltpu.VMEM((maps * tq, LANE), jnp.float32),
            pltpu.VMEM((maps * tq, dv), jnp.float32),
        ],
    )
    return pl.pallas_call(
        functools.partial(_diff_attn_kernel, tq=tq, tk=tk, lam_init=lam_init),
        grid_spec=grid_spec,
        out_shape=jax.ShapeDtypeStruct((b, s, B_HEADS * dv), MXU_DTYPE),
        compiler_params=_cparams("parallel", "arbitrary"),
        name="diff_attention",
    )(itab, jtab, qm, ktm, vh, nb, lam, subln_g.reshape(1, dv))


def _post_mixer_kernel(o_ref, x_ref, gt_ref, sc_ref, sh_ref, gc_ref, wo_ref, w1_ref, w2_ref,
                       lng_ref, lnb_ref, out_ref, *, tf, alpha):
    y = _dot(o_ref[0], wo_ref[...])
    x = _layer_norm_rows(alpha * x_ref[0] + (1.0 + gt_ref[...]) * y, lng_ref[0:1], lnb_ref[0:1])
    h = (x * (1.0 + sc_ref[...]) + sh_ref[...]).astype(MXU_DTYPE)
    y = jnp.zeros(x.shape, jnp.float32)
    for c in range(w1_ref.shape[1] // tf):
        a = jnp.maximum(_dot(h, w1_ref[:, c * tf:(c + 1) * tf]), 0.0)
        y = y + _dot((a * a).astype(MXU_DTYPE), w2_ref[c * tf:(c + 1) * tf, :])
    z = alpha * x + (1.0 + gc_ref[...]) * y
    out_ref[0] = _layer_norm_rows(z, lng_ref[1:2], lnb_ref[1:2])


def _post_mixer(o, x, mod, layer, w_o, w1, w2, ln_g, ln_b, alpha):
    b, s, d = x.shape
    dff = w1.shape[1]
    n_in = o.shape[-1]
    tm, tf = 512, 1024
    row = lambda bb, t: (bb, t, 0)
    once = pl.Buffered(1)
    return pl.pallas_call(
        functools.partial(_post_mixer_kernel, tf=tf, alpha=alpha),
        grid=(b, s // tm),
        in_specs=[
            pl.BlockSpec((1, tm, n_in), row), pl.BlockSpec((1, tm, d), row),
            _mod_spec(layer, 2, d), _mod_spec(layer, 4, d), _mod_spec(layer, 3, d),
            _mod_spec(layer, 5, d),
            pl.BlockSpec((n_in, d), lambda bb, t: (0, 0), pipeline_mode=once),
            pl.BlockSpec((d, dff), lambda bb, t: (0, 0), pipeline_mode=once),
            pl.BlockSpec((dff, d), lambda bb, t: (0, 0), pipeline_mode=once),
            _const_spec((2, d)), _const_spec((2, d)),
        ],
        out_specs=pl.BlockSpec((1, tm, d), row),
        out_shape=jax.ShapeDtypeStruct((b, s, d), jnp.float32),
        compiler_params=_cparams("parallel", "parallel"),
        name="post_mixer",
    )(o, x, mod, mod, mod, mod, w_o.astype(MXU_DTYPE), w1.astype(MXU_DTYPE),
      w2.astype(MXU_DTYPE), ln_g, ln_b)


def kernel(x, c, rel_bias, ada_w, ada_b, ln_g, ln_b, a_w_in, a_kv_norm, a_w_uk, a_w_uv, a_w_o,
           b_w_in, b_lambda, b_subln, b_w_o, mlp_w1, mlp_w2):
    depth = ada_w.shape[0]
    b, s, d = x.shape
    alpha = (2 * depth) ** 0.25
    topk = min(TOPK_MAX, s // 4)
    mod = _ada_mod(c, ada_w, ada_b)
    nb = {tq: _bias_tiles(rel_bias, tq) for tq in {DSA_TQ, DIFF_TQ}}
    nb_a, nb_b = nb[DSA_TQ], nb[DIFF_TQ]
    for i in range(depth):
        jm = i // N_MIXERS
        if i % N_MIXERS == 0:
            q, ckv, ckvt, iqt, ik, iwt = _proj_dsa(x, mod, i, a_w_in[jm], a_kv_norm[jm])
            mask = _indexer_mask(ik, iqt, iwt, topk)
            o = _dsa_attention(q, ckv, ckvt, mask, nb_a,
                               a_w_uk[jm].astype(MXU_DTYPE), a_w_uv[jm].astype(MXU_DTYPE))
            w_o = a_w_o[jm]
        else:
            lam_init = 0.8 - 0.6 * math.exp(-0.3 * i)
            qm, ktm, vh = _proj_diff(x, mod, i, b_w_in[jm])
            o = _diff_attention(qm, ktm, vh, nb_b, b_lambda[jm], b_subln[jm], lam_init)
            w_o = b_w_o[jm]
        x = _post_mixer(o, x, mod, i, w_o, mlp_w1[i], mlp_w2[i], ln_g[i], ln_b[i], alpha)
    return x
```

```python
import functools
import math

import jax
import jax.numpy as jnp
from jax import lax
from jax.experimental import pallas as pl
from jax.experimental.pallas import tpu as pltpu

N_MIXERS = 2
A_HEADS = 16
A_QK_DIM = 64
A_V_DIM = 64
A_KV_RANK = 256
IDX_HEADS = 8
IDX_DIM = 64
TOPK_MAX = 256
B_HEADS = 8
B_HEAD_DIM = 64
B_MAPS = 2 * B_HEADS
REL_BUCKETS = 32
REL_MAX_DIST = 128
LN_EPS = 1e-5
NEG = -1e30
M_INIT = 0.5 * NEG
LOG2E = math.log2(math.e)
SEARCH_ROUND = ("interpolate", "value_mid", "interpolate", "key_mid")
SEARCH_OPENING = ("zero", "above_zero")
SEARCH_INTERP_ROUNDS = 6
FOLD_CHAINS = 8

LANE = 128
SUB = 8
VMEM_LIMIT = 56 * 1024 * 1024
DSA_TQ = 256
DIFF_TQ = 256
DIAG_PARTS = 4

MXU_DTYPE = jnp.bfloat16


def _dot(a, b):
    return jnp.dot(a, b, preferred_element_type=jnp.float32)


def _cparams(*sem):
    return pltpu.CompilerParams(dimension_semantics=sem, vmem_limit_bytes=VMEM_LIMIT)


def _const_spec(shape):
    nd = len(shape)
    return pl.BlockSpec(shape, lambda *_: (0,) * nd)


def _layer_norm_rows(z, g, b):
    mu = jnp.mean(z, axis=-1, keepdims=True)
    zc = z - mu
    var = jnp.mean(zc * zc, axis=-1, keepdims=True)
    return zc * lax.rsqrt(var + LN_EPS) * g + b


def _ada_kernel(ct_ref, w_ref, b_ref, o_ref, *, nb):
    ct = ct_ref[...]
    s = ct * jax.nn.sigmoid(ct)
    w = w_ref[0]
    rows = [jnp.sum(w * s[:, b:b + 1], axis=0, keepdims=True) for b in range(nb)]
    o_ref[0] = jnp.concatenate(rows, axis=0) + b_ref[0]


def _ada_mod(c, ada_w, ada_b):
    depth, d, n = ada_w.shape
    nb = c.shape[0]
    tn = 768
    out = pl.pallas_call(
        functools.partial(_ada_kernel, nb=nb),
        grid=(depth, n // tn),
        in_specs=[
            pl.BlockSpec((d, nb), lambda l, j: (0, 0)),
            pl.BlockSpec((1, d, tn), lambda l, j: (l, 0, j)),
            pl.BlockSpec((1, 1, tn), lambda l, j: (l, 0, j)),
        ],
        out_specs=pl.BlockSpec((1, nb, tn), lambda l, j: (l, 0, j)),
        out_shape=jax.ShapeDtypeStruct((depth, nb, n), jnp.float32),
        compiler_params=_cparams("arbitrary", "arbitrary"),
        name="ada_mod",
    )(c.T, ada_w, ada_b.reshape(depth, 1, n))
    return out.reshape(depth, nb, 6, 1, d)


def _mod_spec(layer, k, d):
    return pl.BlockSpec((None, None, None, 1, d), lambda b, *_: (layer, b, k, 0, 0))


def _rel_bucket(dist):
    n = jnp.maximum(dist, 0)
    max_exact = REL_BUCKETS // 2
    nf = jnp.maximum(n, 1).astype(jnp.float32)
    large = max_exact + (jnp.log(nf / max_exact) / math.log(REL_MAX_DIST / max_exact)
                         * (REL_BUCKETS - max_exact)).astype(jnp.int32)
    large = jnp.minimum(large, REL_BUCKETS - 1)
    return jnp.where(n < max_exact, n, large)


def _bias_tile_kernel(tab_ref, bk_ref, o_ref, *, heads):
    bk = bk_ref[0]
    for h in range(heads):
        far = tab_ref[REL_BUCKETS - 1, h]
        acc = jnp.zeros(bk.shape, jnp.float32)
        for b in range(REL_BUCKETS - 1):
            acc = jnp.where(bk == b, (tab_ref[b, h] - far) * LOG2E, acc)
        o_ref[0, h] = jnp.where(bk < 0, NEG, acc)


def _bias_tiles(rel_bias, tq):
    heads = rel_bias.shape[1]
    r_sub = tq // LANE
    rels = jnp.arange(-2, r_sub + 1, dtype=jnp.int32)
    ql = jnp.arange(tq, dtype=jnp.int32)[None, :, None]
    sl = jnp.arange(LANE, dtype=jnp.int32)[None, None, :]
    dist = ql - rels[:, None, None] * LANE - sl
    bk = jnp.where(dist < 0, -1, _rel_bucket(dist))
    nk = r_sub + 3
    return pl.pallas_call(
        functools.partial(_bias_tile_kernel, heads=heads),
        grid=(nk,),
        in_specs=[
            pl.BlockSpec(memory_space=pltpu.SMEM),
            pl.BlockSpec((1, tq, LANE), lambda k: (k, 0, 0)),
        ],
        out_specs=pl.BlockSpec((1, heads, tq, LANE), lambda k: (k, 0, 0, 0)),
        out_shape=jax.ShapeDtypeStruct((nk, heads, tq, LANE), jnp.float32),
        compiler_params=_cparams("arbitrary"),
        name="bias_tiles",
    )(rel_bias, bk)


def _causal_pairs(s, tq, tk):
    ii, jj = [], []
    for i in range(s // tq):
        for j in range((i * tq + tq - 1) // tk + 1):
            ii.append(i)
            jj.append(j)
    return jnp.asarray(ii, jnp.int32), jnp.asarray(jj, jnp.int32)


def _proj_dsa_kernel(x_ref, sc_ref, sh_ref, wq_ref, wc_ref, wi_ref, wk_ref, kvg_ref,
                     q_ref, ckv_ref, ckvt_ref, iqt_ref, ik_ref, iwt_ref):
    h = (x_ref[0] * (1.0 + sc_ref[...]) + sh_ref[...]).astype(MXU_DTYPE)
    q_ref[0] = _dot(h, wq_ref[...]).astype(q_ref.dtype)
    ckv = _dot(h, wc_ref[...])
    ckv = ckv * lax.rsqrt(jnp.mean(ckv * ckv, axis=-1, keepdims=True) + LN_EPS) * kvg_ref[...]
    ckv_ref[0] = ckv.astype(ckv_ref.dtype)
    ckvt_ref[0] = ckv.T.astype(ckvt_ref.dtype)
    iqt_ref[0] = _dot(h, wi_ref[...]).T.astype(iqt_ref.dtype)
    ikw = _dot(h, wk_ref[...])
    ik_ref[0] = ikw[:, :IDX_DIM].astype(ik_ref.dtype)
    iwt_ref[0] = ikw.T[IDX_DIM:IDX_DIM + IDX_HEADS, :]


def _proj_dsa(x, mod, layer, w_in, kv_norm):
    b, s, d = x.shape
    tm = 512
    nq = A_HEADS * A_QK_DIM
    ni = IDX_HEADS * IDX_DIM
    o1, o2, o3 = nq, nq + A_KV_RANK, nq + A_KV_RANK + ni
    wq = w_in[:, :o1].astype(MXU_DTYPE)
    wc = w_in[:, o1:o2].astype(MXU_DTYPE)
    wi = w_in[:, o2:o3].astype(MXU_DTYPE)
    wk = jnp.pad(w_in[:, o3:], ((0, 0), (0, LANE - (w_in.shape[1] - o3)))).astype(MXU_DTYPE)
    row = lambda bb, t: (bb, t, 0)
    col = lambda bb, t: (bb, 0, t)
    return pl.pallas_call(
        _proj_dsa_kernel,
        grid=(b, s // tm),
        in_specs=[
            pl.BlockSpec((1, tm, d), row),
            _mod_spec(layer, 1, d), _mod_spec(layer, 0, d),
            _const_spec((d, nq)), _const_spec((d, A_KV_RANK)), _const_spec((d, ni)),
            _const_spec((d, LANE)), _const_spec((1, A_KV_RANK)),
        ],
        out_specs=[
            pl.BlockSpec((1, tm, nq), row), pl.BlockSpec((1, tm, A_KV_RANK), row),
            pl.BlockSpec((1, A_KV_RANK, tm), col), pl.BlockSpec((1, ni, tm), col),
            pl.BlockSpec((1, tm, IDX_DIM), row), pl.BlockSpec((1, IDX_HEADS, tm), col),
        ],
        out_shape=[
            jax.ShapeDtypeStruct((b, s, nq), MXU_DTYPE),
            jax.ShapeDtypeStruct((b, s, A_KV_RANK), MXU_DTYPE),
            jax.ShapeDtypeStruct((b, A_KV_RANK, s), MXU_DTYPE),
            jax.ShapeDtypeStruct((b, ni, s), MXU_DTYPE),
            jax.ShapeDtypeStruct((b, s, IDX_DIM), MXU_DTYPE),
            jax.ShapeDtypeStruct((b, IDX_HEADS, s), jnp.float32),
        ],
        compiler_params=_cparams("parallel", "parallel"),
        name="proj_dsa",
    )(x, mod, mod, wq, wc, wi, wk, kv_norm.reshape(1, A_KV_RANK))


def _float_key(v):
    bits = pltpu.bitcast(v, jnp.int32)
    return bits ^ ((bits >> 31) & 0x7FFFFFFF)


def _key_float(k):
    return pltpu.bitcast(k ^ ((k >> 31) & 0x7FFFFFFF), jnp.float32)


def _indexer_kernel(ik_ref, iqt_ref, iwt_ref, mask_ref, keys_ref, p_ref, *, tq, ck, topk, s_len):
    i = pl.program_id(1)
    q0 = i * tq
    nkc = (q0 + tq + ck - 1) // ck
    w = iwt_ref[0] * (IDX_HEADS ** -0.5 * IDX_DIM ** -0.5)
    tpos = q0 + lax.broadcasted_iota(jnp.int32, (1, tq), 1)
    srow = lax.broadcasted_iota(jnp.int32, (ck, 1), 0)
    int_max = jnp.int32(2 ** 31 - 1)
    int_min = jnp.int32(-2 ** 31)

    def fold(x, op):
        slabs = [x[r * SUB:(r + 1) * SUB] for r in range(ck // SUB)]
        lanes = FOLD_CHAINS
        chains = slabs[:lanes]
        for r, slab in enumerate(slabs[lanes:]):
            chains[r % lanes] = op(chains[r % lanes], slab)
        slabs = chains
        while len(slabs) > 1:
            slabs = [op(a, b) for a, b in zip(slabs[0::2], slabs[1::2])]
        return slabs[0]

    def score_chunk(kc, carry):
        kmax, kmin = carry
        k0 = pl.multiple_of(kc * ck, ck)
        ik = ik_ref[0, pl.ds(k0, ck), :]
        sc = jnp.zeros((ck, tq), jnp.float32)
        for h in range(IDX_HEADS):
            lg = _dot(ik, iqt_ref[0, h * IDX_DIM:(h + 1) * IDX_DIM, :])
            sc = sc + jnp.maximum(lg, 0.0) * w[h:h + 1, :]
        causal = k0 + srow <= tpos
        key = _float_key(jnp.where(causal, sc, -jnp.inf))
        keys_ref[pl.ds(k0, ck), :] = key
        return (jnp.maximum(kmax, fold(key, jnp.maximum)),
                jnp.minimum(kmin, fold(jnp.where(causal, key, int_max), jnp.minimum)))

    kmax, kmin = lax.fori_loop(0, nkc, score_chunk, (jnp.full((SUB, tq), int_min, jnp.int32),
                                                      jnp.full((SUB, tq), int_max, jnp.int32)))
    kmax = jnp.max(kmax, axis=0, keepdims=True)
    kmin = jnp.min(kmin, axis=0, keepdims=True)

    def count(pred):
        def body(kc, accs):
            k0 = pl.multiple_of(kc * ck, ck)
            accs = list(accs)
            chunk = keys_ref.at[pl.ds(k0, ck), :]
            for r in range(ck // SUB):
                k = chunk[r * SUB:(r + 1) * SUB, :]
                hit = pred(k, k0 + srow[r * SUB:(r + 1) * SUB]).astype(jnp.float32)
                accs[r % FOLD_CHAINS] = accs[r % FOLD_CHAINS] + hit
            return tuple(accs)
        accs = lax.fori_loop(0, nkc, body, (jnp.zeros((SUB, tq), jnp.float32),) * FOLD_CHAINS)
        acc = functools.reduce(jnp.add, accs)
        return jnp.sum(acc, axis=0, keepdims=True).astype(jnp.int32)

    k_eff = jnp.minimum(topk, tpos + 1)
    log_keff = jnp.log(k_eff.astype(jnp.float32))

    def unsettled(lo, hi, cnt_lo):
        return (cnt_lo != k_eff) & (hi > lo + 1)

    def candidate(st, kind, bisect_only):
        lo, hi, cnt_lo, cnt_hi = st
        key_mid = (lo >> 1) + (hi >> 1) + (lo & hi & 1)
        if kind == "key_mid":
            cand = key_mid
        elif kind in ("zero", "above_zero"):
            cand = jnp.full((1, tq), 0 if kind == "zero" else 1, jnp.int32)
        else:
            vlo, vhi = _key_float(lo), _key_float(hi)
            if kind == "interpolate":
                a = jnp.log(cnt_lo.astype(jnp.float32) + 0.5)
                b = jnp.log(cnt_hi.astype(jnp.float32) + 0.5)
                cand = _float_key(vlo + (vhi - vlo) * ((a - log_keff) / (a - b)))
            else:
                cand = _float_key(0.5 * vlo + 0.5 * vhi)
            cand = jnp.where(bisect_only, key_mid, cand)
        return jnp.clip(cand, lo + 1, hi - 1)

    def narrow(st, cand, cnt, upd):
        lo, hi, cnt_lo, cnt_hi = st
        up = upd & (cnt >= k_eff)
        dn = upd & (cnt < k_eff)
        return (jnp.where(up, cand, lo), jnp.where(dn, cand, hi),
                jnp.where(up, cnt, cnt_lo), jnp.where(dn, cnt, cnt_hi))

    def search_pass(st, kind, bisect_only):
        cand = candidate(st, kind, bisect_only)
        cnt = count(lambda k, sp: k >= cand)
        return narrow(st, cand, cnt, unsettled(st[0], st[1], st[2]))

    def run_rounds(one_pass, active, st):
        def cond(c):
            return jnp.max(active(c[1:]).astype(jnp.int32)) > 0

        def body(c):
            bisect_only = c[0] >= SEARCH_INTERP_ROUNDS
            st = c[1:]
            for kind in SEARCH_ROUND:
                st = one_pass(st, kind, bisect_only)
            return (c[0] + 1,) + st
        return lax.while_loop(cond, body, (jnp.int32(0),) + st)[1:]

    st = (kmin, kmax + 1, tpos + 1, jnp.zeros((1, tq), jnp.int32))
    for kind in SEARCH_OPENING:
        st = search_pass(st, kind, False)

    tau, hi, cnt_ge, cnt_gt = run_rounds(search_pass, lambda st: unsettled(st[0], st[1], st[2]), st)
    need = k_eff - cnt_gt
    overflow = cnt_ge > k_eff
    p_ref[...] = jnp.full((1, tq), s_len, jnp.int32)

    @pl.when(jnp.max(overflow.astype(jnp.int32)) > 0)
    def _():
        def pos_step(it, pos):
            cand = pos + jnp.left_shift(jnp.int32(1), (s_len.bit_length() - 1) - it)
            below = count(lambda k, sp: (k == tau) & (sp < cand))
            return jnp.where(below < need, cand, pos)
        pos = lax.fori_loop(0, s_len.bit_length(), pos_step, jnp.zeros((1, tq), jnp.int32))
        p_ref[...] = jnp.where(overflow, pos, s_len)

    last_eq = p_ref[...]

    def write_chunk(kc, carry):
        k0 = pl.multiple_of(kc * ck, ck)

        @pl.when(kc < nkc)
        def _():
            k = keys_ref[pl.ds(k0, ck), :]
            sel = (k > tau) | ((k == tau) & (k0 + srow <= last_eq))
            mask_ref[0, :, pl.ds(k0, ck)] = jnp.where(sel, 0.0, NEG).T.astype(mask_ref.dtype)

        @pl.when(kc >= nkc)
        def _():
            mask_ref[0, :, pl.ds(k0, ck)] = jnp.full((tq, ck), NEG, mask_ref.dtype)
        return carry

    lax.fori_loop(0, s_len // ck, write_chunk, 0)


def _indexer_mask(ik, iqt, iwt, topk):
    b, s, di = ik.shape
    ni = iqt.shape[1]
    tq, ck = 256, 512
    return pl.pallas_call(
        functools.partial(_indexer_kernel, tq=tq, ck=ck, topk=topk, s_len=s),
        grid=(b, s // tq),
        in_specs=[
            pl.BlockSpec((1, s, di), lambda bb, i: (bb, 0, 0)),
            pl.BlockSpec((1, ni, tq), lambda bb, i: (bb, 0, i)),
            pl.BlockSpec((1, IDX_HEADS, tq), lambda bb, i: (bb, 0, i)),
        ],
        out_specs=pl.BlockSpec((1, tq, s), lambda bb, i: (bb, i, 0)),
        out_shape=jax.ShapeDtypeStruct((b, s, s), jnp.bfloat16),
        scratch_shapes=[pltpu.VMEM((s, tq), jnp.int32), pltpu.VMEM((1, tq), jnp.int32)],
        compiler_params=_cparams("parallel", "parallel"),
        name="indexer_topk",
    )(ik, iqt, iwt)


def _flash_update(s_cols, m_ref, l_ref, rows):
    m_old = m_ref[rows, :]
    m_cur = functools.reduce(jnp.maximum, s_cols)
    m_new = jnp.maximum(m_old, jnp.max(m_cur, axis=-1, keepdims=True))
    alpha = jnp.exp2(m_old - m_new)
    ps = [jnp.exp2(sc - m_new) for sc in s_cols]
    l_ref[rows, :] = alpha * l_ref[rows, :] + functools.reduce(jnp.add, ps)
    m_ref[rows, :] = m_new
    return jnp.concatenate(ps, axis=1).astype(MXU_DTYPE), alpha


def _row_sums(l):
    ones = jnp.ones((LANE, LANE), MXU_DTYPE)
    hi = l.astype(MXU_DTYPE)
    lo = (l - hi.astype(jnp.float32)).astype(MXU_DTYPE)
    return _dot(hi, ones) + _dot(lo, ones)


def _dispatch_tile(step, i, j, tq, tk):
    assert tq * DIAG_PARTS == tk
    sub_q = tq // LANE
    causal_cols = (i + 1) * tq - j * tk
    for w in range(1, DIAG_PARTS + 1):
        first = (w - 1) * sub_q
        cols = {first + r: r + 2 for r in range(-1 if first else 0, sub_q)}
        pl.when(causal_cols == w * tq)(functools.partial(step, cols, w * tq))
    nsub = tk // LANE
    before_block_start = causal_cols == tk + tq
    pl.when(before_block_start)(functools.partial(step, {nsub - 1: 1}, tk))
    pl.when((causal_cols > tk) & jnp.logical_not(before_block_start))(
        functools.partial(step, {}, tk))


def _dsa_attn_kernel(itab, jtab, q_ref, wuk_ref, ckvt_ref, ckv_ref, mask_ref, nb_ref, wuv_ref,
                     o_ref, qlat_ref, m_ref, l_ref, acc_ref, *, tq, tk, hc):
    pidx = pl.program_id(1)
    i = itab[pidx]
    j = jtab[pidx]
    jlast = (i * tq + tq - 1) // tk
    heads = A_HEADS
    nchunk = heads // hc
    crows = hc * tq

    @pl.when(j == 0)
    def _():
        for h in range(heads):
            qh = q_ref[0, :, h * A_QK_DIM:(h + 1) * A_QK_DIM]
            ql = _dot(qh, wuk_ref[h]) * (A_QK_DIM ** -0.5 * LOG2E)
            qlat_ref[h * tq:(h + 1) * tq, :] = ql.astype(qlat_ref.dtype)
        m_ref[...] = jnp.full(m_ref.shape, M_INIT, jnp.float32)
        l_ref[...] = jnp.zeros(l_ref.shape, jnp.float32)
        acc_ref[...] = jnp.zeros(acc_ref.shape, jnp.float32)

    def step(bias_cols, width):
        nsub = width // LANE
        ckvt = ckvt_ref[0, :, :width]
        ckv = ckv_ref[0, :width, :]
        maskf = mask_ref[0, :, :width].astype(jnp.float32)
        mcols = [maskf[:, c * LANE:(c + 1) * LANE] for c in range(nsub)]
        s_next = _dot(qlat_ref[0:crows, :], ckvt)
        for ci in range(nchunk):
            s_all = s_next
            if ci + 1 < nchunk:
                s_next = _dot(qlat_ref[(ci + 1) * crows:(ci + 2) * crows, :], ckvt)
            ps, alphas = [], []
            for hh in range(hc):
                h = ci * hc + hh
                cols = []
                for c in range(nsub):
                    add = mcols[c] + nb_ref[bias_cols[c], h] if c in bias_cols else mcols[c]
                    cols.append(s_all[hh * tq:(hh + 1) * tq, c * LANE:(c + 1) * LANE] + add)
                p, alpha = _flash_update(cols, m_ref, l_ref, slice(h * tq, (h + 1) * tq))
                ps.append(p)
                alphas.append(alpha)
            pv = _dot(jnp.concatenate(ps, axis=0), ckv)
            a = jnp.concatenate(alphas, axis=0)
            rows = slice(ci * crows, (ci + 1) * crows)
            for half in range(A_KV_RANK // LANE):
                cs = slice(half * LANE, (half + 1) * LANE)
                acc_ref[rows, cs] = a * acc_ref[rows, cs] + pv[:, cs]

    _dispatch_tile(step, i, j, tq, tk)

    @pl.when(j == jlast)
    def _():
        for h in range(heads):
            rows = slice(h * tq, (h + 1) * tq)
            inv_l = 1.0 / _row_sums(l_ref[rows, :])
            o_lat = jnp.concatenate([acc_ref[rows, c * LANE:(c + 1) * LANE] * inv_l
                                     for c in range(A_KV_RANK // LANE)], axis=1)
            oh = _dot(o_lat.astype(MXU_DTYPE), wuv_ref[h])
            o_ref[0, :, h * A_V_DIM:(h + 1) * A_V_DIM] = oh.astype(o_ref.dtype)


def _dsa_attention(q, ckv, ckvt, mask, nb, w_uk, w_uv):
    b, s, nq = q.shape
    tq, tk, hc = DSA_TQ, min(1024, s), 2
    itab, jtab = _causal_pairs(s, tq, tk)
    rows = A_HEADS * tq
    once = pl.Buffered(1)
    grid_spec = pltpu.PrefetchScalarGridSpec(
        num_scalar_prefetch=2,
        grid=(b, int(itab.shape[0])),
        in_specs=[
            pl.BlockSpec((1, tq, nq), lambda bb, p, it, jt: (bb, it[p], 0)),
            pl.BlockSpec(w_uk.shape, lambda bb, p, it, jt: (0, 0, 0), pipeline_mode=once),
            pl.BlockSpec((1, A_KV_RANK, tk), lambda bb, p, it, jt: (bb, 0, jt[p])),
            pl.BlockSpec((1, tk, A_KV_RANK), lambda bb, p, it, jt: (bb, jt[p], 0)),
            pl.BlockSpec((1, tq, tk), lambda bb, p, it, jt: (bb, it[p], jt[p])),
            pl.BlockSpec(nb.shape, lambda bb, p, it, jt: (0, 0, 0, 0), pipeline_mode=once),
            pl.BlockSpec(w_uv.shape, lambda bb, p, it, jt: (0, 0, 0), pipeline_mode=once),
        ],
        out_specs=pl.BlockSpec((1, tq, A_HEADS * A_V_DIM), lambda bb, p, it, jt: (bb, it[p], 0)),
        scratch_shapes=[
            pltpu.VMEM((rows, A_KV_RANK), MXU_DTYPE),
            pltpu.VMEM((rows, LANE), jnp.float32),
            pltpu.VMEM((rows, LANE), jnp.float32),
            pltpu.VMEM((rows, A_KV_RANK), jnp.float32),
        ],
    )
    return pl.pallas_call(
        functools.partial(_dsa_attn_kernel, tq=tq, tk=tk, hc=hc),
        grid_spec=grid_spec,
        out_shape=jax.ShapeDtypeStruct((b, s, A_HEADS * A_V_DIM), MXU_DTYPE),
        compiler_params=_cparams("parallel", "arbitrary"),
        name="dsa_attention",
    )(itab, jtab, q, w_uk, ckvt, ckv, mask, nb, w_uv)


def _proj_diff_kernel(x_ref, sc_ref, sh_ref, wq_ref, wk_ref, wv_ref, q_ref, kt_ref, v_ref):
    h = (x_ref[0] * (1.0 + sc_ref[...]) + sh_ref[...]).astype(MXU_DTYPE)
    q = _dot(h, wq_ref[...]) * (B_HEAD_DIM ** -0.5 * LOG2E)
    kt = _dot(h, wk_ref[...]).T
    v = _dot(h, wv_ref[...])
    for m in range(B_MAPS):
        q_ref[0, m] = q[:, m * B_HEAD_DIM:(m + 1) * B_HEAD_DIM].astype(q_ref.dtype)
        kt_ref[0, m] = kt[m * B_HEAD_DIM:(m + 1) * B_HEAD_DIM, :].astype(kt_ref.dtype)
    dv = 2 * B_HEAD_DIM
    for hd in range(B_HEADS):
        v_ref[0, hd] = v[:, hd * dv:(hd + 1) * dv].astype(v_ref.dtype)


def _proj_diff(x, mod, layer, w_in):
    b, s, d = x.shape
    tm = 512
    n = w_in.shape[1] // 3
    dv = 2 * B_HEAD_DIM
    ws = [w_in[:, k * n:(k + 1) * n].astype(MXU_DTYPE) for k in range(3)]
    return pl.pallas_call(
        _proj_diff_kernel,
        grid=(b, s // tm),
        in_specs=[pl.BlockSpec((1, tm, d), lambda bb, t: (bb, t, 0)),
                  _mod_spec(layer, 1, d), _mod_spec(layer, 0, d)] + [_const_spec((d, n))] * 3,
        out_specs=[
            pl.BlockSpec((1, B_MAPS, tm, B_HEAD_DIM), lambda bb, t: (bb, 0, t, 0)),
            pl.BlockSpec((1, B_MAPS, B_HEAD_DIM, tm), lambda bb, t: (bb, 0, 0, t)),
            pl.BlockSpec((1, B_HEADS, tm, dv), lambda bb, t: (bb, 0, t, 0)),
        ],
        out_shape=[
            jax.ShapeDtypeStruct((b, B_MAPS, s, B_HEAD_DIM), MXU_DTYPE),
            jax.ShapeDtypeStruct((b, B_MAPS, B_HEAD_DIM, s), MXU_DTYPE),
            jax.ShapeDtypeStruct((b, B_HEADS, s, dv), MXU_DTYPE),
        ],
        compiler_params=_cparams("parallel", "parallel"),
        name="proj_diff",
    )(x, mod, mod, *ws)


def _diff_attn_kernel(itab, jtab, q_ref, kt_ref, v_ref, nb_ref, lam_ref, g_ref, o_ref,
                      m_ref, l_ref, acc_ref, *, tq, tk, lam_init):
    pidx = pl.program_id(1)
    i = itab[pidx]
    j = jtab[pidx]
    jlast = (i * tq + tq - 1) // tk
    dv = 2 * B_HEAD_DIM

    @pl.when(j == 0)
    def _():
        m_ref[...] = jnp.full(m_ref.shape, M_INIT, jnp.float32)
        l_ref[...] = jnp.zeros(l_ref.shape, jnp.float32)
        acc_ref[...] = jnp.zeros(acc_ref.shape, jnp.float32)

    def step(bias_cols, width):
        nsub = width // LANE

        def logits(h):
            return [_dot(q_ref[0, 2 * h + mm], kt_ref[0, 2 * h + mm, :, :width]) for mm in range(2)]

        s_next = logits(0)
        for h in range(B_HEADS):
            s_pair = s_next
            if h + 1 < B_HEADS:
                s_next = logits(h + 1)
            ps, alphas = [], []
            for mm in range(2):
                col = 2 * h + mm
                cols = [s_pair[mm][:, c * LANE:(c + 1) * LANE] for c in range(nsub)]
                cols = [cols[c] + nb_ref[bias_cols[c], col] if c in bias_cols else cols[c]
                        for c in range(nsub)]
                p, alpha = _flash_update(cols, m_ref, l_ref, slice(col * tq, (col + 1) * tq))
                ps.append(p)
                alphas.append(alpha)
            pv = _dot(jnp.concatenate(ps, axis=0), v_ref[0, h, :width, :])
            rows = slice(2 * h * tq, (2 * h + 2) * tq)
            acc_ref[rows, :] = jnp.concatenate(alphas, axis=0) * acc_ref[rows, :] + pv

    _dispatch_tile(step, i, j, tq, tk)

    @pl.when(j == jlast)
    def _():
        lam = lam_ref[...]
        lam_full = (jnp.exp(jnp.sum(lam[0:1] * lam[1:2], axis=-1, keepdims=True))
                    - jnp.exp(jnp.sum(lam[2:3] * lam[3:4], axis=-1, keepdims=True)) + lam_init)
        for h in range(B_HEADS):
            r1 = slice(2 * h * tq, (2 * h + 1) * tq)
            r2 = slice((2 * h + 1) * tq, (2 * h + 2) * tq)
            o = (acc_ref[r1, :] / _row_sums(l_ref[r1, :])
                 - lam_full * (acc_ref[r2, :] / _row_sums(l_ref[r2, :])))
            o = o * lax.rsqrt(jnp.mean(o * o, axis=-1, keepdims=True) + LN_EPS) * g_ref[...]
            o_ref[0, :, h * dv:(h + 1) * dv] = (o * (1.0 - lam_init)).astype(o_ref.dtype)


def _diff_attention(qm, ktm, vh, nb, lam, subln_g, lam_init):
    b, maps, s, dh = qm.shape
    tq, tk = DIFF_TQ, min(1024, s)
    dv = 2 * B_HEAD_DIM
    assert dv == LANE
    itab, jtab = _causal_pairs(s, tq, tk)
    once = pl.Buffered(1)
    grid_spec = pltpu.PrefetchScalarGridSpec(
        num_scalar_prefetch=2,
        grid=(b, int(itab.shape[0])),
        in_specs=[
            pl.BlockSpec((1, maps, tq, dh), lambda bb, p, it, jt: (bb, 0, it[p], 0)),
            pl.BlockSpec((1, maps, dh, tk), lambda bb, p, it, jt: (bb, 0, 0, jt[p])),
            pl.BlockSpec((1, B_HEADS, tk, dv), lambda bb, p, it, jt: (bb, 0, jt[p], 0)),
            pl.BlockSpec(nb.shape, lambda bb, p, it, jt: (0, 0, 0, 0), pipeline_mode=once),
            pl.BlockSpec(lam.shape, lambda bb, p, it, jt: (0, 0)),
            pl.BlockSpec((1, dv), lambda bb, p, it, jt: (0, 0)),
        ],
        out_specs=pl.BlockSpec((1, tq, B_HEADS * dv), lambda bb, p, it, jt: (bb, it[p], 0)),
        scratch_shapes=[
            pltpu.VMEM((maps * tq, LANE), jnp.float32),
            pltpu.VMEM((maps * tq, LANE), jnp.float32),
            pltpu.VMEM((maps * tq, dv), jnp.float32),
        ],
    )
    return pl.pallas_call(
        functools.partial(_diff_attn_kernel, tq=tq, tk=tk, lam_init=lam_init),
        grid_spec=grid_spec,
        out_shape=jax.ShapeDtypeStruct((b, s, B_HEADS * dv), MXU_DTYPE),
        compiler_params=_cparams("parallel", "arbitrary"),
        name="diff_attention",
    )(itab, jtab, qm, ktm, vh, nb, lam, subln_g.reshape(1, dv))


def _post_mixer_kernel(o_ref, x_ref, gt_ref, sc_ref, sh_ref, gc_ref, wo_ref, w1_ref, w2_ref,
                       lng_ref, lnb_ref, out_ref, *, tf, alpha):
    y = _dot(o_ref[0], wo_ref[...])
    x = _layer_norm_rows(alpha * x_ref[0] + (1.0 + gt_ref[...]) * y, lng_ref[0:1], lnb_ref[0:1])
    h = (x * (1.0 + sc_ref[...]) + sh_ref[...]).astype(MXU_DTYPE)
    y = jnp.zeros(x.shape, jnp.float32)
    for c in range(w1_ref.shape[1] // tf):
        a = jnp.maximum(_dot(h, w1_ref[:, c * tf:(c + 1) * tf]), 0.0)
        y = y + _dot((a * a).astype(MXU_DTYPE), w2_ref[c * tf:(c + 1) * tf, :])
    z = alpha * x + (1.0 + gc_ref[...]) * y
    out_ref[0] = _layer_norm_rows(z, lng_ref[1:2], lnb_ref[1:2])


def _post_mixer(o, x, mod, layer, w_o, w1, w2, ln_g, ln_b, alpha):
    b, s, d = x.shape
    dff = w1.shape[1]
    n_in = o.shape[-1]
    tm, tf = 512, 1024
    row = lambda bb, t: (bb, t, 0)
    once = pl.Buffered(1)
    return pl.pallas_call(
        functools.partial(_post_mixer_kernel, tf=tf, alpha=alpha),
        grid=(b, s // tm),
        in_specs=[
            pl.BlockSpec((1, tm, n_in), row), pl.BlockSpec((1, tm, d), row),
            _mod_spec(layer, 2, d), _mod_spec(layer, 4, d), _mod_spec(layer, 3, d),
            _mod_spec(layer, 5, d),
            pl.BlockSpec((n_in, d), lambda bb, t: (0, 0), pipeline_mode=once),
            pl.BlockSpec((d, dff), lambda bb, t: (0, 0), pipeline_mode=once),
            pl.BlockSpec((dff, d), lambda bb, t: (0, 0), pipeline_mode=once),
            _const_spec((2, d)), _const_spec((2, d)),
        ],
        out_specs=pl.BlockSpec((1, tm, d), row),
        out_shape=jax.ShapeDtypeStruct((b, s, d), jnp.float32),
        compiler_params=_cparams("parallel", "parallel"),
        name="post_mixer",
    )(o, x, mod, mod, mod, mod, w_o.astype(MXU_DTYPE), w1.astype(MXU_DTYPE),
      w2.astype(MXU_DTYPE), ln_g, ln_b)


def kernel(x, c, rel_bias, ada_w, ada_b, ln_g, ln_b, a_w_in, a_kv_norm, a_w_uk, a_w_uv, a_w_o,
           b_w_in, b_lambda, b_subln, b_w_o, mlp_w1, mlp_w2):
    depth = ada_w.shape[0]
    b, s, d = x.shape
    alpha = (2 * depth) ** 0.25
    topk = min(TOPK_MAX, s // 4)
    mod = _ada_mod(c, ada_w, ada_b)
    nb = {tq: _bias_tiles(rel_bias, tq) for tq in {DSA_TQ, DIFF_TQ}}
    nb_a, nb_b = nb[DSA_TQ], nb[DIFF_TQ]
    for i in range(depth):
        jm = i // N_MIXERS
        if i % N_MIXERS == 0:
            q, ckv, ckvt, iqt, ik, iwt = _proj_dsa(x, mod, i, a_w_in[jm], a_kv_norm[jm])
            mask = _indexer_mask(ik, iqt, iwt, topk)
            o = _dsa_attention(q, ckv, ckvt, mask, nb_a,
                               a_w_uk[jm].astype(MXU_DTYPE), a_w_uv[jm].astype(MXU_DTYPE))
            w_o = a_w_o[jm]
        else:
            lam_init = 0.8 - 0.6 * math.exp(-0.3 * i)
            qm, ktm, vh = _proj_diff(x, mod, i, b_w_in[jm])
            o = _diff_attention(qm, ktm, vh, nb_b, b_lambda[jm], b_subln[jm], lam_init)
            w_o = b_w_o[jm]
        x = _post_mixer(o, x, mod, i, w_o, mlp_w1[i], mlp_w2[i], ln_g[i], ln_b[i], alpha)
    return x
```

```python
import functools
import math

import jax
import jax.numpy as jnp
from jax import lax
from jax.experimental import pallas as pl
from jax.experimental.pallas import tpu as pltpu

N_MIXERS = 2
A_HEADS = 16
A_QK_DIM = 64
A_V_DIM = 64
A_KV_RANK = 256
IDX_HEADS = 8
IDX_DIM = 64
TOPK_MAX = 256
B_HEADS = 8
B_HEAD_DIM = 64
B_MAPS = 2 * B_HEADS
REL_BUCKETS = 32
REL_MAX_DIST = 128
LN_EPS = 1e-5
NEG = -1e30
M_INIT = 0.5 * NEG
LOG2E = math.log2(math.e)
SEARCH_ROUND = ("interpolate", "value_mid", "interpolate", "key_mid")
SEARCH_OPENING = ("zero", "above_zero")
SEARCH_INTERP_ROUNDS = 6
FOLD_CHAINS = 8

LANE = 128
SUB = 8
VMEM_LIMIT = 56 * 1024 * 1024
DSA_TQ = 256
DIFF_TQ = 256
DIAG_PARTS = 4

MXU_DTYPE = jnp.bfloat16


def _dot(a, b):
    return jnp.dot(a, b, preferred_element_type=jnp.float32)


def _cparams(*sem):
    return pltpu.CompilerParams(dimension_semantics=sem, vmem_limit_bytes=VMEM_LIMIT)


def _const_spec(shape):
    nd = len(shape)
    return pl.BlockSpec(shape, lambda *_: (0,) * nd)


def _layer_norm_rows(z, g, b):
    mu = jnp.mean(z, axis=-1, keepdims=True)
    zc = z - mu
    var = jnp.mean(zc * zc, axis=-1, keepdims=True)
    return zc * lax.rsqrt(var + LN_EPS) * g + b


def _ada_kernel(ct_ref, w_ref, b_ref, o_ref, *, nb):
    ct = ct_ref[...]
    s = ct * jax.nn.sigmoid(ct)
    w = w_ref[0]
    rows = [jnp.sum(w * s[:, b:b + 1], axis=0, keepdims=True) for b in range(nb)]
    o_ref[0] = jnp.concatenate(rows, axis=0) + b_ref[0]


def _ada_mod(c, ada_w, ada_b):
    depth, d, n = ada_w.shape
    nb = c.shape[0]
    tn = 768
    out = pl.pallas_call(
        functools.partial(_ada_kernel, nb=nb),
        grid=(depth, n // tn),
        in_specs=[
            pl.BlockSpec((d, nb), lambda l, j: (0, 0)),
            pl.BlockSpec((1, d, tn), lambda l, j: (l, 0, j)),
            pl.BlockSpec((1, 1, tn), lambda l, j: (l, 0, j)),
        ],
        out_specs=pl.BlockSpec((1, nb, tn), lambda l, j: (l, 0, j)),
        out_shape=jax.ShapeDtypeStruct((depth, nb, n), jnp.float32),
        compiler_params=_cparams("arbitrary", "arbitrary"),
        name="ada_mod",
    )(c.T, ada_w, ada_b.reshape(depth, 1, n))
    return out.reshape(depth, nb, 6, 1, d)


def _mod_spec(layer, k, d):
    return pl.BlockSpec((None, None, None, 1, d), lambda b, *_: (layer, b, k, 0, 0))


def _rel_bucket(dist):
    n = jnp.maximum(dist, 0)
    max_exact = REL_BUCKETS // 2
    nf = jnp.maximum(n, 1).astype(jnp.float32)
    large = max_exact + (jnp.log(nf / max_exact) / math.log(REL_MAX_DIST / max_exact)
                         * (REL_BUCKETS - max_exact)).astype(jnp.int32)
    large = jnp.minimum(large, REL_BUCKETS - 1)
    return jnp.where(n < max_exact, n, large)


def _bias_tile_kernel(tab_ref, bk_ref, o_ref, *, heads):
    bk = bk_ref[0]
    for h in range(heads):
        far = tab_ref[REL_BUCKETS - 1, h]
        acc = jnp.zeros(bk.shape, jnp.float32)
        for b in range(REL_BUCKETS - 1):
            acc = jnp.where(bk == b, (tab_ref[b, h] - far) * LOG2E, acc)
        o_ref[0, h] = jnp.where(bk < 0, NEG, acc)


def _bias_tiles(rel_bias, tq):
    heads = rel_bias.shape[1]
    r_sub = tq // LANE
    rels = jnp.arange(-2, r_sub + 1, dtype=jnp.int32)
    ql = jnp.arange(tq, dtype=jnp.int32)[None, :, None]
    sl = jnp.arange(LANE, dtype=jnp.int32)[None, None, :]
    dist = ql - rels[:, None, None] * LANE - sl
    bk = jnp.where(dist < 0, -1, _rel_bucket(dist))
    nk = r_sub + 3
    return pl.pallas_call(
        functools.partial(_bias_tile_kernel, heads=heads),
        grid=(nk,),
        in_specs=[
            pl.BlockSpec(memory_space=pltpu.SMEM),
            pl.BlockSpec((1, tq, LANE), lambda k: (k, 0, 0)),
        ],
        out_specs=pl.BlockSpec((1, heads, tq, LANE), lambda k: (k, 0, 0, 0)),
        out_shape=jax.ShapeDtypeStruct((nk, heads, tq, LANE), jnp.float32),
        compiler_params=_cparams("arbitrary"),
        name="bias_tiles",
    )(rel_bias, bk)


def _causal_pairs(s, tq, tk):
    ii, jj = [], []
    for i in range(s // tq):
        for j in range((i * tq + tq - 1) // tk + 1):
            ii.append(i)
            jj.append(j)
    return jnp.asarray(ii, jnp.int32), jnp.asarray(jj, jnp.int32)


def _proj_dsa_kernel(x_ref, sc_ref, sh_ref, wq_ref, wc_ref, wi_ref, wk_ref, kvg_ref,
                     q_ref, ckv_ref, ckvt_ref, iqt_ref, ik_ref, iwt_ref):
    h = (x_ref[0] * (1.0 + sc_ref[...]) + sh_ref[...]).astype(MXU_DTYPE)
    q_ref[0] = _dot(h, wq_ref[...]).astype(q_ref.dtype)
    ckv = _dot(h, wc_ref[...])
    ckv = ckv * lax.rsqrt(jnp.mean(ckv * ckv, axis=-1, keepdims=True) + LN_EPS) * kvg_ref[...]
    ckv_ref[0] = ckv.astype(ckv_ref.dtype)
    ckvt_ref[0] = ckv.T.astype(ckvt_ref.dtype)
    iqt_ref[0] = _dot(h, wi_ref[...]).T.astype(iqt_ref.dtype)
    ikw = _dot(h, wk_ref[...])
    ik_ref[0] = ikw[:, :IDX_DIM].astype(ik_ref.dtype)
    iwt_ref[0] = ikw.T[IDX_DIM:IDX_DIM + IDX_HEADS, :]


def _proj_dsa(x, mod, layer, w_in, kv_norm):
    b, s, d = x.shape
    tm = 512
    nq = A_HEADS * A_QK_DIM
    ni = IDX_HEADS * IDX_DIM
    o1, o2, o3 = nq, nq + A_KV_RANK, nq + A_KV_RANK + ni
    wq = w_in[:, :o1].astype(MXU_DTYPE)
    wc = w_in[:, o1:o2].astype(MXU_DTYPE)
    wi = w_in[:, o2:o3].astype(MXU_DTYPE)
    wk = jnp.pad(w_in[:, o3:], ((0, 0), (0, LANE - (w_in.shape[1] - o3)))).astype(MXU_DTYPE)
    row = lambda bb, t: (bb, t, 0)
    col = lambda bb, t: (bb, 0, t)
    return pl.pallas_call(
        _proj_dsa_kernel,
        grid=(b, s // tm),
        in_specs=[
            pl.BlockSpec((1, tm, d), row),
            _mod_spec(layer, 1, d), _mod_spec(layer, 0, d),
            _const_spec((d, nq)), _const_spec((d, A_KV_RANK)), _const_spec((d, ni)),
            _const_spec((d, LANE)), _const_spec((1, A_KV_RANK)),
        ],
        out_specs=[
            pl.BlockSpec((1, tm, nq), row), pl.BlockSpec((1, tm, A_KV_RANK), row),
            pl.BlockSpec((1, A_KV_RANK, tm), col), pl.BlockSpec((1, ni, tm), col),
            pl.BlockSpec((1, tm, IDX_DIM), row), pl.BlockSpec((1, IDX_HEADS, tm), col),
        ],
        out_shape=[
            jax.ShapeDtypeStruct((b, s, nq), MXU_DTYPE),
            jax.ShapeDtypeStruct((b, s, A_KV_RANK), MXU_DTYPE),
            jax.ShapeDtypeStruct((b, A_KV_RANK, s), MXU_DTYPE),
            jax.ShapeDtypeStruct((b, ni, s), MXU_DTYPE),
            jax.ShapeDtypeStruct((b, s, IDX_DIM), MXU_DTYPE),
            jax.ShapeDtypeStruct((b, IDX_HEADS, s), jnp.float32),
        ],
        compiler_params=_cparams("parallel", "parallel"),
        name="proj_dsa",
    )(x, mod, mod, wq, wc, wi, wk, kv_norm.reshape(1, A_KV_RANK))


def _float_key(v):
    bits = pltpu.bitcast(v, jnp.int32)
    return bits ^ ((bits >> 31) & 0x7FFFFFFF)


def _key_float(k):
    return pltpu.bitcast(k ^ ((k >> 31) & 0x7FFFFFFF), jnp.float32)


def _indexer_kernel(ik_ref, iqt_ref, iwt_ref, mask_ref, keys_ref, p_ref, *, tq, ck, topk, s_len):
    i = pl.program_id(1)
    q0 = i * tq
    nkc = (q0 + tq + ck - 1) // ck
    w = iwt_ref[0] * (IDX_HEADS ** -0.5 * IDX_DIM ** -0.5)
    tpos = q0 + lax.broadcasted_iota(jnp.int32, (1, tq), 1)
    srow = lax.broadcasted_iota(jnp.int32, (ck, 1), 0)
    int_max = jnp.int32(2 ** 31 - 1)
    int_min = jnp.int32(-2 ** 31)

    def fold(x, op):
        slabs = [x[r * SUB:(r + 1) * SUB] for r in range(ck // SUB)]
        lanes = FOLD_CHAINS
        chains = slabs[:lanes]
        for r, slab in enumerate(slabs[lanes:]):
            chains[r % lanes] = op(chains[r % lanes], slab)
        slabs = chains
        while len(slabs) > 1:
            slabs = [op(a, b) for a, b in zip(slabs[0::2], slabs[1::2])]
        return slabs[0]

    def score_chunk(kc, carry):
        kmax, kmin = carry
        k0 = pl.multiple_of(kc * ck, ck)
        ik = ik_ref[0, pl.ds(k0, ck), :]
        sc = jnp.zeros((ck, tq), jnp.float32)
        for h in range(IDX_HEADS):
            lg = _dot(ik, iqt_ref[0, h * IDX_DIM:(h + 1) * IDX_DIM, :])
            sc = sc + jnp.maximum(lg, 0.0) * w[h:h + 1, :]
        causal = k0 + srow <= tpos
        key = _float_key(jnp.where(causal, sc, -jnp.inf))
        keys_ref[pl.ds(k0, ck), :] = key
        return (jnp.maximum(kmax, fold(key, jnp.maximum)),
                jnp.minimum(kmin, fold(jnp.where(causal, key, int_max), jnp.minimum)))

    kmax, kmin = lax.fori_loop(0, nkc, score_chunk, (jnp.full((SUB, tq), int_min, jnp.int32),
                                                      jnp.full((SUB, tq), int_max, jnp.int32)))
    kmax = jnp.max(kmax, axis=0, keepdims=True)
    kmin = jnp.min(kmin, axis=0, keepdims=True)

    def count(pred):
        def body(kc, accs):
            k0 = pl.multiple_of(kc * ck, ck)
            accs = list(accs)
            chunk = keys_ref.at[pl.ds(k0, ck), :]
            for r in range(ck // SUB):
                k = chunk[r * SUB:(r + 1) * SUB, :]
                hit = pred(k, k0 + srow[r * SUB:(r + 1) * SUB]).astype(jnp.float32)
                accs[r % FOLD_CHAINS] = accs[r % FOLD_CHAINS] + hit
            return tuple(accs)
        accs = lax.fori_loop(0, nkc, body, (jnp.zeros((SUB, tq), jnp.float32),) * FOLD_CHAINS)
        acc = functools.reduce(jnp.add, accs)
        return jnp.sum(acc, axis=0, keepdims=True).astype(jnp.int32)

    k_eff = jnp.minimum(topk, tpos + 1)
    log_keff = jnp.log(k_eff.astype(jnp.float32))

    def unsettled(lo, hi, cnt_lo):
        return (cnt_lo != k_eff) & (hi > lo + 1)

    def candidate(st, kind, bisect_only):
        lo, hi, cnt_lo, cnt_hi = st
        key_mid = (lo >> 1) + (hi >> 1) + (lo & hi & 1)
        if kind == "key_mid":
            cand = key_mid
        elif kind in ("zero", "above_zero"):
            cand = jnp.full((1, tq), 0 if kind == "zero" else 1, jnp.int32)
        else:
            vlo, vhi = _key_float(lo), _key_float(hi)
            if kind == "interpolate":
                a = jnp.log(cnt_lo.astype(jnp.float32) + 0.5)
                b = jnp.log(cnt_hi.astype(jnp.float32) + 0.5)
                cand = _float_key(vlo + (vhi - vlo) * ((a - log_keff) / (a - b)))
            else:
                cand = _float_key(0.5 * vlo + 0.5 * vhi)
            cand = jnp.where(bisect_only, key_mid, cand)
        return jnp.clip(cand, lo + 1, hi - 1)

    def narrow(st, cand, cnt, upd):
        lo, hi, cnt_lo, cnt_hi = st
        up = upd & (cnt >= k_eff)
        dn = upd & (cnt < k_eff)
        return (jnp.where(up, cand, lo), jnp.where(dn, cand, hi),
                jnp.where(up, cnt, cnt_lo), jnp.where(dn, cnt, cnt_hi))

    def search_pass(st, kind, bisect_only):
        cand = candidate(st, kind, bisect_only)
        cnt = count(lambda k, sp: k >= cand)
        return narrow(st, cand, cnt, unsettled(st[0], st[1], st[2]))

    def run_rounds(one_pass, active, st):
        def cond(c):
            return jnp.max(active(c[1:]).astype(jnp.int32)) > 0

        def body(c):
            bisect_only = c[0] >= SEARCH_INTERP_ROUNDS
            st = c[1:]
            for kind in SEARCH_ROUND:
                st = one_pass(st, kind, bisect_only)
            return (c[0] + 1,) + st
        return lax.while_loop(cond, body, (jnp.int32(0),) + st)[1:]

    st = (kmin, kmax + 1, tpos + 1, jnp.zeros((1, tq), jnp.int32))
    for kind in SEARCH_OPENING:
        st = search_pass(st, kind, False)

    tau, hi, cnt_ge, cnt_gt = run_rounds(search_pass, lambda st: unsettled(st[0], st[1], st[2]), st)
    need = k_eff - cnt_gt
    overflow = cnt_ge > k_eff
    p_ref[...] = jnp.full((1, tq), s_len, jnp.int32)

    any_overflow = jnp.max(overflow.astype(jnp.int32)) > 0

    @pl.when(any_overflow)
    def _():
        def pos_step(it, pos):
            cand = pos + jnp.left_shift(jnp.int32(1), (s_len.bit_length() - 1) - it)
            below = count(lambda k, sp: (k == tau) & (sp < cand))
            return jnp.where(below < need, cand, pos)
        pos = lax.fori_loop(0, s_len.bit_length(), pos_step, jnp.zeros((1, tq), jnp.int32))
        p_ref[...] = jnp.where(overflow, pos, s_len)

    last_eq = p_ref[...]

    def write_chunk(kc, carry):
        k0 = pl.multiple_of(kc * ck, ck)

        @pl.when((kc < nkc) & any_overflow)
        def _():
            k = keys_ref[pl.ds(k0, ck), :]
            sel = (k > tau) | ((k == tau) & (k0 + srow <= last_eq))
            mask_ref[0, :, pl.ds(k0, ck)] = jnp.where(sel, 0.0, NEG).T.astype(mask_ref.dtype)

        @pl.when((kc < nkc) & jnp.logical_not(any_overflow))
        def _():
            sel = keys_ref[pl.ds(k0, ck), :] >= tau
            mask_ref[0, :, pl.ds(k0, ck)] = jnp.where(sel, 0.0, NEG).T.astype(mask_ref.dtype)

        @pl.when(kc >= nkc)
        def _():
            mask_ref[0, :, pl.ds(k0, ck)] = jnp.full((tq, ck), NEG, mask_ref.dtype)
        return carry

    lax.fori_loop(0, s_len // ck, write_chunk, 0)


def _indexer_mask(ik, iqt, iwt, topk):
    b, s, di = ik.shape
    ni = iqt.shape[1]
    tq, ck = 256, 512
    return pl.pallas_call(
        functools.partial(_indexer_kernel, tq=tq, ck=ck, topk=topk, s_len=s),
        grid=(b, s // tq),
        in_specs=[
            pl.BlockSpec((1, s, di), lambda bb, i: (bb, 0, 0)),
            pl.BlockSpec((1, ni, tq), lambda bb, i: (bb, 0, i)),
            pl.BlockSpec((1, IDX_HEADS, tq), lambda bb, i: (bb, 0, i)),
        ],
        out_specs=pl.BlockSpec((1, tq, s), lambda bb, i: (bb, i, 0)),
        out_shape=jax.ShapeDtypeStruct((b, s, s), jnp.bfloat16),
        scratch_shapes=[pltpu.VMEM((s, tq), jnp.int32), pltpu.VMEM((1, tq), jnp.int32)],
        compiler_params=_cparams("parallel", "parallel"),
        name="indexer_topk",
    )(ik, iqt, iwt)


def _flash_update(s_cols, m_ref, l_ref, rows):
    m_old = m_ref[rows, :]
    m_cur = functools.reduce(jnp.maximum, s_cols)
    m_new = jnp.maximum(m_old, jnp.max(m_cur, axis=-1, keepdims=True))
    alpha = jnp.exp2(m_old - m_new)
    ps = [jnp.exp2(sc - m_new) for sc in s_cols]
    l_ref[rows, :] = alpha * l_ref[rows, :] + functools.reduce(jnp.add, ps)
    m_ref[rows, :] = m_new
    return jnp.concatenate(ps, axis=1).astype(MXU_DTYPE), alpha


def _row_sums(l):
    ones = jnp.ones((LANE, LANE), MXU_DTYPE)
    hi = l.astype(MXU_DTYPE)
    lo = (l - hi.astype(jnp.float32)).astype(MXU_DTYPE)
    return _dot(hi, ones) + _dot(lo, ones)


def _dispatch_tile(step, i, j, tq, tk):
    assert tq * DIAG_PARTS == tk
    sub_q = tq // LANE
    causal_cols = (i + 1) * tq - j * tk
    for w in range(1, DIAG_PARTS + 1):
        first = (w - 1) * sub_q
        cols = {first + r: r + 2 for r in range(-1 if first else 0, sub_q)}
        pl.when(causal_cols == w * tq)(functools.partial(step, cols, w * tq))
    nsub = tk // LANE
    before_block_start = causal_cols == tk + tq
    pl.when(before_block_start)(functools.partial(step, {nsub - 1: 1}, tk))
    pl.when((causal_cols > tk) & jnp.logical_not(before_block_start))(
        functools.partial(step, {}, tk))


def _dsa_attn_kernel(itab, jtab, q_ref, wuk_ref, ckvt_ref, ckv_ref, mask_ref, nb_ref, wuv_ref,
                     o_ref, qlat_ref, m_ref, l_ref, acc_ref, *, tq, tk, hc):
    pidx = pl.program_id(1)
    i = itab[pidx]
    j = jtab[pidx]
    jlast = (i * tq + tq - 1) // tk
    heads = A_HEADS
    nchunk = heads // hc
    crows = hc * tq

    @pl.when(j == 0)
    def _():
        for h in range(heads):
            qh = q_ref[0, :, h * A_QK_DIM:(h + 1) * A_QK_DIM]
            ql = _dot(qh, wuk_ref[h]) * (A_QK_DIM ** -0.5 * LOG2E)
            qlat_ref[h * tq:(h + 1) * tq, :] = ql.astype(qlat_ref.dtype)
        m_ref[...] = jnp.full(m_ref.shape, M_INIT, jnp.float32)
        l_ref[...] = jnp.zeros(l_ref.shape, jnp.float32)
        acc_ref[...] = jnp.zeros(acc_ref.shape, jnp.float32)

    def step(bias_cols, width):
        nsub = width // LANE
        ckvt = ckvt_ref[0, :, :width]
        ckv = ckv_ref[0, :width, :]
        maskf = mask_ref[0, :, :width].astype(jnp.float32)
        mcols = [maskf[:, c * LANE:(c + 1) * LANE] for c in range(nsub)]
        s_next = _dot(qlat_ref[0:crows, :], ckvt)
        for ci in range(nchunk):
            s_all = s_next
            if ci + 1 < nchunk:
                s_next = _dot(qlat_ref[(ci + 1) * crows:(ci + 2) * crows, :], ckvt)
            ps, alphas = [], []
            for hh in range(hc):
                h = ci * hc + hh
                cols = []
                for c in range(nsub):
                    add = mcols[c] + nb_ref[bias_cols[c], h] if c in bias_cols else mcols[c]
                    cols.append(s_all[hh * tq:(hh + 1) * tq, c * LANE:(c + 1) * LANE] + add)
                p, alpha = _flash_update(cols, m_ref, l_ref, slice(h * tq, (h + 1) * tq))
                ps.append(p)
                alphas.append(alpha)
            pv = _dot(jnp.concatenate(ps, axis=0), ckv)
            a = jnp.concatenate(alphas, axis=0)
            rows = slice(ci * crows, (ci + 1) * crows)
            for half in range(A_KV_RANK // LANE):
                cs = slice(half * LANE, (half + 1) * LANE)
                acc_ref[rows, cs] = a * acc_ref[rows, cs] + pv[:, cs]

    _dispatch_tile(step, i, j, tq, tk)

    @pl.when(j == jlast)
    def _():
        for h in range(heads):
            rows = slice(h * tq, (h + 1) * tq)
            inv_l = 1.0 / _row_sums(l_ref[rows, :])
            o_lat = jnp.concatenate([acc_ref[rows, c * LANE:(c + 1) * LANE] * inv_l
                                     for c in range(A_KV_RANK // LANE)], axis=1)
            oh = _dot(o_lat.astype(MXU_DTYPE), wuv_ref[h])
            o_ref[0, :, h * A_V_DIM:(h + 1) * A_V_DIM] = oh.astype(o_ref.dtype)


def _dsa_attention(q, ckv, ckvt, mask, nb, w_uk, w_uv):
    b, s, nq = q.shape
    tq, tk, hc = DSA_TQ, min(1024, s), 2
    itab, jtab = _causal_pairs(s, tq, tk)
    rows = A_HEADS * tq
    once = pl.Buffered(1)
    grid_spec = pltpu.PrefetchScalarGridSpec(
        num_scalar_prefetch=2,
        grid=(b, int(itab.shape[0])),
        in_specs=[
            pl.BlockSpec((1, tq, nq), lambda bb, p, it, jt: (bb, it[p], 0)),
            pl.BlockSpec(w_uk.shape, lambda bb, p, it, jt: (0, 0, 0), pipeline_mode=once),
            pl.BlockSpec((1, A_KV_RANK, tk), lambda bb, p, it, jt: (bb, 0, jt[p])),
            pl.BlockSpec((1, tk, A_KV_RANK), lambda bb, p, it, jt: (bb, jt[p], 0)),
            pl.BlockSpec((1, tq, tk), lambda bb, p, it, jt: (bb, it[p], jt[p])),
            pl.BlockSpec(nb.shape, lambda bb, p, it, jt: (0, 0, 0, 0), pipeline_mode=once),
            pl.BlockSpec(w_uv.shape, lambda bb, p, it, jt: (0, 0, 0), pipeline_mode=once),
        ],
        out_specs=pl.BlockSpec((1, tq, A_HEADS * A_V_DIM), lambda bb, p, it, jt: (bb, it[p], 0)),
        scratch_shapes=[
            pltpu.VMEM((rows, A_KV_RANK), MXU_DTYPE),
            pltpu.VMEM((rows, LANE), jnp.float32),
            pltpu.VMEM((rows, LANE), jnp.float32),
            pltpu.VMEM((rows, A_KV_RANK), jnp.float32),
        ],
    )
    return pl.pallas_call(
        functools.partial(_dsa_attn_kernel, tq=tq, tk=tk, hc=hc),
        grid_spec=grid_spec,
        out_shape=jax.ShapeDtypeStruct((b, s, A_HEADS * A_V_DIM), MXU_DTYPE),
        compiler_params=_cparams("parallel", "arbitrary"),
        name="dsa_attention",
    )(itab, jtab, q, w_uk, ckvt, ckv, mask, nb, w_uv)


def _proj_diff_kernel(x_ref, sc_ref, sh_ref, wq_ref, wk_ref, wv_ref, q_ref, kt_ref, v_ref):
    h = (x_ref[0] * (1.0 + sc_ref[...]) + sh_ref[...]).astype(MXU_DTYPE)
    q = _dot(h, wq_ref[...]) * (B_HEAD_DIM ** -0.5 * LOG2E)
    kt = _dot(h, wk_ref[...]).T
    v = _dot(h, wv_ref[...])
    for m in range(B_MAPS):
        q_ref[0, m] = q[:, m * B_HEAD_DIM:(m + 1) * B_HEAD_DIM].astype(q_ref.dtype)
        kt_ref[0, m] = kt[m * B_HEAD_DIM:(m + 1) * B_HEAD_DIM, :].astype(kt_ref.dtype)
    dv = 2 * B_HEAD_DIM
    for hd in range(B_HEADS):
        v_ref[0, hd] = v[:, hd * dv:(hd + 1) * dv].astype(v_ref.dtype)


def _proj_diff(x, mod, layer, w_in):
    b, s, d = x.shape
    tm = 512
    n = w_in.shape[1] // 3
    dv = 2 * B_HEAD_DIM
    ws = [w_in[:, k * n:(k + 1) * n].astype(MXU_DTYPE) for k in range(3)]
    return pl.pallas_call(
        _proj_diff_kernel,
        grid=(b, s // tm),
        in_specs=[pl.BlockSpec((1, tm, d), lambda bb, t: (bb, t, 0)),
                  _mod_spec(layer, 1, d), _mod_spec(layer, 0, d)] + [_const_spec((d, n))] * 3,
        out_specs=[
            pl.BlockSpec((1, B_MAPS, tm, B_HEAD_DIM), lambda bb, t: (bb, 0, t, 0)),
            pl.BlockSpec((1, B_MAPS, B_HEAD_DIM, tm), lambda bb, t: (bb, 0, 0, t)),
            pl.BlockSpec((1, B_HEADS, tm, dv), lambda bb, t: (bb, 0, t, 0)),
        ],
        out_shape=[
            jax.ShapeDtypeStruct((b, B_MAPS, s, B_HEAD_DIM), MXU_DTYPE),
            jax.ShapeDtypeStruct((b, B_MAPS, B_HEAD_DIM, s), MXU_DTYPE),
            jax.ShapeDtypeStruct((b, B_HEADS, s, dv), MXU_DTYPE),
        ],
        compiler_params=_cparams("parallel", "parallel"),
        name="proj_diff",
    )(x, mod, mod, *ws)


def _diff_attn_kernel(itab, jtab, q_ref, kt_ref, v_ref, nb_ref, lam_ref, g_ref, o_ref,
                      m_ref, l_ref, acc_ref, *, tq, tk, lam_init):
    pidx = pl.program_id(1)
    i = itab[pidx]
    j = jtab[pidx]
    jlast = (i * tq + tq - 1) // tk
    dv = 2 * B_HEAD_DIM

    @pl.when(j == 0)
    def _():
        m_ref[...] = jnp.full(m_ref.shape, M_INIT, jnp.float32)
        l_ref[...] = jnp.zeros(l_ref.shape, jnp.float32)
        acc_ref[...] = jnp.zeros(acc_ref.shape, jnp.float32)

    def step(bias_cols, width):
        nsub = width // LANE

        def logits(h):
            return [_dot(q_ref[0, 2 * h + mm], kt_ref[0, 2 * h + mm, :, :width]) for mm in range(2)]

        s_next = logits(0)
        for h in range(B_HEADS):
            s_pair = s_next
            if h + 1 < B_HEADS:
                s_next = logits(h + 1)
            ps, alphas = [], []
            for mm in range(2):
                col = 2 * h + mm
                cols = [s_pair[mm][:, c * LANE:(c + 1) * LANE] for c in range(nsub)]
                cols = [cols[c] + nb_ref[bias_cols[c], col] if c in bias_cols else cols[c]
                        for c in range(nsub)]
                p, alpha = _flash_update(cols, m_ref, l_ref, slice(col * tq, (col + 1) * tq))
                ps.append(p)
                alphas.append(alpha)
            pv = _dot(jnp.concatenate(ps, axis=0), v_ref[0, h, :width, :])
            rows = slice(2 * h * tq, (2 * h + 2) * tq)
            acc_ref[rows, :] = jnp.concatenate(alphas, axis=0) * acc_ref[rows, :] + pv

    _dispatch_tile(step, i, j, tq, tk)

    @pl.when(j == jlast)
    def _():
        lam = lam_ref[...]
        lam_full = (jnp.exp(jnp.sum(lam[0:1] * lam[1:2], axis=-1, keepdims=True))
                    - jnp.exp(jnp.sum(lam[2:3] * lam[3:4], axis=-1, keepdims=True)) + lam_init)
        for h in range(B_HEADS):
            r1 = slice(2 * h * tq, (2 * h + 1) * tq)
            r2 = slice((2 * h + 1) * tq, (2 * h + 2) * tq)
            o = (acc_ref[r1, :] / _row_sums(l_ref[r1, :])
                 - lam_full * (acc_ref[r2, :] / _row_sums(l_ref[r2, :])))
            o = o * lax.rsqrt(jnp.mean(o * o, axis=-1, keepdims=True) + LN_EPS) * g_ref[...]
            o_ref[0, :, h * dv:(h + 1) * dv] = (o * (1.0 - lam_init)).astype(o_ref.dtype)


def _diff_attention(qm, ktm, vh, nb, lam, subln_g, lam_init):
    b, maps, s, dh = qm.shape
    tq, tk = DIFF_TQ, min(1024, s)
    dv = 2 * B_HEAD_DIM
    assert dv == LANE
    itab, jtab = _causal_pairs(s, tq, tk)
    once = pl.Buffered(1)
    grid_spec = pltpu.PrefetchScalarGridSpec(
        num_scalar_prefetch=2,
        grid=(b, int(itab.shape[0])),
        in_specs=[
            pl.BlockSpec((1, maps, tq, dh), lambda bb, p, it, jt: (bb, 0, it[p], 0)),
            pl.BlockSpec((1, maps, dh, tk), lambda bb, p, it, jt: (bb, 0, 0, jt[p])),
            pl.BlockSpec((1, B_HEADS, tk, dv), lambda bb, p, it, jt: (bb, 0, jt[p], 0)),
            pl.BlockSpec(nb.shape, lambda bb, p, it, jt: (0, 0, 0, 0), pipeline_mode=once),
            pl.BlockSpec(lam.shape, lambda bb, p, it, jt: (0, 0)),
            pl.BlockSpec((1, dv), lambda bb, p, it, jt: (0, 0)),
        ],
        out_specs=pl.BlockSpec((1, tq, B_HEADS * dv), lambda bb, p, it, jt: (bb, it[p], 0)),
        scratch_shapes=[
            pltpu.VMEM((maps * tq, LANE), jnp.float32),
            pltpu.VMEM((maps * tq, LANE), jnp.float32),
            pltpu.VMEM((maps * tq, dv), jnp.float32),
        ],
    )
    return pl.pallas_call(
        functools.partial(_diff_attn_kernel, tq=tq, tk=tk, lam_init=lam_init),
        grid_spec=grid_spec,
        out_shape=jax.ShapeDtypeStruct((b, s, B_HEADS * dv), MXU_DTYPE),
        compiler_params=_cparams("parallel", "arbitrary"),
        name="diff_attention",
    )(itab, jtab, qm, ktm, vh, nb, lam, subln_g.reshape(1, dv))


def _post_mixer_kernel(o_ref, x_ref, gt_ref, sc_ref, sh_ref, gc_ref, wo_ref, w1_ref, w2_ref,
                       lng_ref, lnb_ref, out_ref, *, tf, alpha):
    y = _dot(o_ref[0], wo_ref[...])
    x = _layer_norm_rows(alpha * x_ref[0] + (1.0 + gt_ref[...]) * y, lng_ref[0:1], lnb_ref[0:1])
    h = (x * (1.0 + sc_ref[...]) + sh_ref[...]).astype(MXU_DTYPE)
    y = jnp.zeros(x.shape, jnp.float32)
    for c in range(w1_ref.shape[1] // tf):
        a = jnp.maximum(_dot(h, w1_ref[:, c * tf:(c + 1) * tf]), 0.0)
        y = y + _dot((a * a).astype(MXU_DTYPE), w2_ref[c * tf:(c + 1) * tf, :])
    z = alpha * x + (1.0 + gc_ref[...]) * y
    out_ref[0] = _layer_norm_rows(z, lng_ref[1:2], lnb_ref[1:2])


def _post_mixer(o, x, mod, layer, w_o, w1, w2, ln_g, ln_b, alpha):
    b, s, d = x.shape
    dff = w1.shape[1]
    n_in = o.shape[-1]
    tm, tf = 512, 1024
    row = lambda bb, t: (bb, t, 0)
    once = pl.Buffered(1)
    return pl.pallas_call(
        functools.partial(_post_mixer_kernel, tf=tf, alpha=alpha),
        grid=(b, s // tm),
        in_specs=[
            pl.BlockSpec((1, tm, n_in), row), pl.BlockSpec((1, tm, d), row),
            _mod_spec(layer, 2, d), _mod_spec(layer, 4, d), _mod_spec(layer, 3, d),
            _mod_spec(layer, 5, d),
            pl.BlockSpec((n_in, d), lambda bb, t: (0, 0), pipeline_mode=once),
            pl.BlockSpec((d, dff), lambda bb, t: (0, 0), pipeline_mode=once),
            pl.BlockSpec((dff, d), lambda bb, t: (0, 0), pipeline_mode=once),
            _const_spec((2, d)), _const_spec((2, d)),
        ],
        out_specs=pl.BlockSpec((1, tm, d), row),
        out_shape=jax.ShapeDtypeStruct((b, s, d), jnp.float32),
        compiler_params=_cparams("parallel", "parallel"),
        name="post_mixer",
    )(o, x, mod, mod, mod, mod, w_o.astype(MXU_DTYPE), w1.astype(MXU_DTYPE),
      w2.astype(MXU_DTYPE), ln_g, ln_b)


def kernel(x, c, rel_bias, ada_w, ada_b, ln_g, ln_b, a_w_in, a_kv_norm, a_w_uk, a_w_uv, a_w_o,
           b_w_in, b_lambda, b_subln, b_w_o, mlp_w1, mlp_w2):
    depth = ada_w.shape[0]
    b, s, d = x.shape
    alpha = (2 * depth) ** 0.25
    topk = min(TOPK_MAX, s // 4)
    mod = _ada_mod(c, ada_w, ada_b)
    nb = {tq: _bias_tiles(rel_bias, tq) for tq in {DSA_TQ, DIFF_TQ}}
    nb_a, nb_b = nb[DSA_TQ], nb[DIFF_TQ]
    for i in range(depth):
        jm = i // N_MIXERS
        if i % N_MIXERS == 0:
            q, ckv, ckvt, iqt, ik, iwt = _proj_dsa(x, mod, i, a_w_in[jm], a_kv_norm[jm])
            mask = _indexer_mask(ik, iqt, iwt, topk)
            o = _dsa_attention(q, ckv, ckvt, mask, nb_a,
                               a_w_uk[jm].astype(MXU_DTYPE), a_w_uv[jm].astype(MXU_DTYPE))
            w_o = a_w_o[jm]
        else:
            lam_init = 0.8 - 0.6 * math.exp(-0.3 * i)
            qm, ktm, vh = _proj_diff(x, mod, i, b_w_in[jm])
            o = _diff_attention(qm, ktm, vh, nb_b, b_lambda[jm], b_subln[jm], lam_init)
            w_o = b_w_o[jm]
        x = _post_mixer(o, x, mod, i, w_o, mlp_w1[i], mlp_w2[i], ln_g[i], ln_b[i], alpha)
    return x
```

```python
import functools
import math

import jax
import jax.numpy as jnp
from jax import lax
from jax.experimental import pallas as pl
from jax.experimental.pallas import tpu as pltpu

N_MIXERS = 2
A_HEADS = 16
A_QK_DIM = 64
A_V_DIM = 64
A_KV_RANK = 256
IDX_HEADS = 8
IDX_DIM = 64
TOPK_MAX = 256
B_HEADS = 8
B_HEAD_DIM = 64
B_MAPS = 2 * B_HEADS
REL_BUCKETS = 32
REL_MAX_DIST = 128
LN_EPS = 1e-5
NEG = -1e30
M_INIT = 0.5 * NEG
LOG2E = math.log2(math.e)
SEARCH_ROUND = ("interpolate", "value_mid", "interpolate", "key_mid")
SEARCH_OPENING = ("zero", "above_zero")
SEARCH_INTERP_ROUNDS = 6
FOLD_CHAINS = 8

LANE = 128
SUB = 8
VMEM_LIMIT = 56 * 1024 * 1024
TOKEN_TILE = 512
FF_TILE = 1024
ADA_TILE = 768
KEY_TILE = 1024
DSA_TQ = 256
DIFF_TQ = 256
DIAG_PARTS = 4
DSA_HEAD_CHUNK = 2
IDX_TQ = 256
IDX_CK = 512

MXU_DTYPE = jnp.bfloat16


def _dot(a, b):
    return jnp.dot(a, b, preferred_element_type=jnp.float32)


def _cparams(*sem):
    return pltpu.CompilerParams(dimension_semantics=sem, vmem_limit_bytes=VMEM_LIMIT)


def _const_spec(shape):
    nd = len(shape)
    return pl.BlockSpec(shape, lambda *_: (0,) * nd)


def _layer_norm_rows(z, g, b):
    mu = jnp.mean(z, axis=-1, keepdims=True)
    zc = z - mu
    var = jnp.mean(zc * zc, axis=-1, keepdims=True)
    return zc * lax.rsqrt(var + LN_EPS) * g + b


def _ada_kernel(ct_ref, w_ref, b_ref, o_ref, *, nb):
    ct = ct_ref[...]
    s = ct * jax.nn.sigmoid(ct)
    w = w_ref[0]
    rows = [jnp.sum(w * s[:, b:b + 1], axis=0, keepdims=True) for b in range(nb)]
    o_ref[0] = jnp.concatenate(rows, axis=0) + b_ref[0]


def _ada_mod(c, ada_w, ada_b):
    depth, d, n = ada_w.shape
    nb = c.shape[0]
    tn = ADA_TILE
    out = pl.pallas_call(
        functools.partial(_ada_kernel, nb=nb),
        grid=(depth, n // tn),
        in_specs=[
            pl.BlockSpec((d, nb), lambda l, j: (0, 0)),
            pl.BlockSpec((1, d, tn), lambda l, j: (l, 0, j)),
            pl.BlockSpec((1, 1, tn), lambda l, j: (l, 0, j)),
        ],
        out_specs=pl.BlockSpec((1, nb, tn), lambda l, j: (l, 0, j)),
        out_shape=jax.ShapeDtypeStruct((depth, nb, n), jnp.float32),
        compiler_params=_cparams("arbitrary", "arbitrary"),
        name="ada_mod",
    )(c.T, ada_w, ada_b.reshape(depth, 1, n))
    return out.reshape(depth, nb, 6, 1, d)


def _mod_spec(layer, k, d):
    return pl.BlockSpec((None, None, None, 1, d), lambda b, *_: (layer, b, k, 0, 0))


def _rel_bucket(dist):
    n = jnp.maximum(dist, 0)
    max_exact = REL_BUCKETS // 2
    nf = jnp.maximum(n, 1).astype(jnp.float32)
    large = max_exact + (jnp.log(nf / max_exact) / math.log(REL_MAX_DIST / max_exact)
                         * (REL_BUCKETS - max_exact)).astype(jnp.int32)
    large = jnp.minimum(large, REL_BUCKETS - 1)
    return jnp.where(n < max_exact, n, large)


def _bias_tile_kernel(tab_ref, bk_ref, o_ref, *, heads):
    bk = bk_ref[0]
    for h in range(heads):
        far = tab_ref[REL_BUCKETS - 1, h]
        acc = jnp.zeros(bk.shape, jnp.float32)
        for b in range(REL_BUCKETS - 1):
            acc = jnp.where(bk == b, (tab_ref[b, h] - far) * LOG2E, acc)
        o_ref[0, h] = jnp.where(bk < 0, NEG, acc)


def _bias_tiles(rel_bias, tq):
    heads = rel_bias.shape[1]
    r_sub = tq // LANE
    rels = jnp.arange(-2, r_sub + 1, dtype=jnp.int32)
    ql = jnp.arange(tq, dtype=jnp.int32)[None, :, None]
    sl = jnp.arange(LANE, dtype=jnp.int32)[None, None, :]
    dist = ql - rels[:, None, None] * LANE - sl
    bk = jnp.where(dist < 0, -1, _rel_bucket(dist))
    nk = r_sub + 3
    return pl.pallas_call(
        functools.partial(_bias_tile_kernel, heads=heads),
        grid=(nk,),
        in_specs=[
            pl.BlockSpec(memory_space=pltpu.SMEM),
            pl.BlockSpec((1, tq, LANE), lambda k: (k, 0, 0)),
        ],
        out_specs=pl.BlockSpec((1, heads, tq, LANE), lambda k: (k, 0, 0, 0)),
        out_shape=jax.ShapeDtypeStruct((nk, heads, tq, LANE), jnp.float32),
        compiler_params=_cparams("arbitrary"),
        name="bias_tiles",
    )(rel_bias, bk)


def _causal_pairs(s, tq, tk):
    ii, jj = [], []
    for i in range(s // tq):
        for j in range((i * tq + tq - 1) // tk + 1):
            ii.append(i)
            jj.append(j)
    return jnp.asarray(ii, jnp.int32), jnp.asarray(jj, jnp.int32)


def _proj_dsa_kernel(x_ref, sc_ref, sh_ref, wq_ref, wc_ref, wi_ref, wk_ref, kvg_ref,
                     q_ref, ckv_ref, ckvt_ref, iqt_ref, ik_ref, iwt_ref):
    h = (x_ref[0] * (1.0 + sc_ref[...]) + sh_ref[...]).astype(MXU_DTYPE)
    q_ref[0] = _dot(h, wq_ref[...]).astype(q_ref.dtype)
    ckv = _dot(h, wc_ref[...])
    ckv = ckv * lax.rsqrt(jnp.mean(ckv * ckv, axis=-1, keepdims=True) + LN_EPS) * kvg_ref[...]
    ckv_ref[0] = ckv.astype(ckv_ref.dtype)
    ckvt_ref[0] = ckv.T.astype(ckvt_ref.dtype)
    iqt_ref[0] = _dot(h, wi_ref[...]).T.astype(iqt_ref.dtype)
    ikw = _dot(h, wk_ref[...])
    ik_ref[0] = ikw[:, :IDX_DIM].astype(ik_ref.dtype)
    iwt_ref[0] = ikw.T[IDX_DIM:IDX_DIM + IDX_HEADS, :]


def _proj_dsa(x, mod, layer, w_in, kv_norm):
    b, s, d = x.shape
    tm = TOKEN_TILE
    nq = A_HEADS * A_QK_DIM
    ni = IDX_HEADS * IDX_DIM
    o1, o2, o3 = nq, nq + A_KV_RANK, nq + A_KV_RANK + ni
    wq = w_in[:, :o1].astype(MXU_DTYPE)
    wc = w_in[:, o1:o2].astype(MXU_DTYPE)
    wi = w_in[:, o2:o3].astype(MXU_DTYPE)
    wk = jnp.pad(w_in[:, o3:], ((0, 0), (0, LANE - (w_in.shape[1] - o3)))).astype(MXU_DTYPE)
    row = lambda bb, t: (bb, t, 0)
    col = lambda bb, t: (bb, 0, t)
    return pl.pallas_call(
        _proj_dsa_kernel,
        grid=(b, s // tm),
        in_specs=[
            pl.BlockSpec((1, tm, d), row),
            _mod_spec(layer, 1, d), _mod_spec(layer, 0, d),
            _const_spec((d, nq)), _const_spec((d, A_KV_RANK)), _const_spec((d, ni)),
            _const_spec((d, LANE)), _const_spec((1, A_KV_RANK)),
        ],
        out_specs=[
            pl.BlockSpec((1, tm, nq), row), pl.BlockSpec((1, tm, A_KV_RANK), row),
            pl.BlockSpec((1, A_KV_RANK, tm), col), pl.BlockSpec((1, ni, tm), col),
            pl.BlockSpec((1, tm, IDX_DIM), row), pl.BlockSpec((1, IDX_HEADS, tm), col),
        ],
        out_shape=[
            jax.ShapeDtypeStruct((b, s, nq), MXU_DTYPE),
            jax.ShapeDtypeStruct((b, s, A_KV_RANK), MXU_DTYPE),
            jax.ShapeDtypeStruct((b, A_KV_RANK, s), MXU_DTYPE),
            jax.ShapeDtypeStruct((b, ni, s), MXU_DTYPE),
            jax.ShapeDtypeStruct((b, s, IDX_DIM), MXU_DTYPE),
            jax.ShapeDtypeStruct((b, IDX_HEADS, s), jnp.float32),
        ],
        compiler_params=_cparams("parallel", "parallel"),
        name="proj_dsa",
    )(x, mod, mod, wq, wc, wi, wk, kv_norm.reshape(1, A_KV_RANK))


def _float_key(v):
    bits = pltpu.bitcast(v, jnp.int32)
    return bits ^ ((bits >> 31) & 0x7FFFFFFF)


def _key_float(k):
    return pltpu.bitcast(k ^ ((k >> 31) & 0x7FFFFFFF), jnp.float32)


def _indexer_kernel(ik_ref, iqt_ref, iwt_ref, mask_ref, keys_ref, p_ref, *, tq, ck, topk, s_len):
    i = pl.program_id(1)
    q0 = i * tq
    nkc = (q0 + tq + ck - 1) // ck
    w = iwt_ref[0] * (IDX_HEADS ** -0.5 * IDX_DIM ** -0.5)
    tpos = q0 + lax.broadcasted_iota(jnp.int32, (1, tq), 1)
    srow = lax.broadcasted_iota(jnp.int32, (ck, 1), 0)
    int_max = jnp.int32(2 ** 31 - 1)
    int_min = jnp.int32(-2 ** 31)

    def fold(x, op):
        slabs = [x[r * SUB:(r + 1) * SUB] for r in range(ck // SUB)]
        lanes = FOLD_CHAINS
        chains = slabs[:lanes]
        for r, slab in enumerate(slabs[lanes:]):
            chains[r % lanes] = op(chains[r % lanes], slab)
        slabs = chains
        while len(slabs) > 1:
            slabs = [op(a, b) for a, b in zip(slabs[0::2], slabs[1::2])]
        return slabs[0]

    def score_chunk(kc, carry):
        kmax, kmin = carry
        k0 = pl.multiple_of(kc * ck, ck)
        ik = ik_ref[0, pl.ds(k0, ck), :]
        sc = jnp.zeros((ck, tq), jnp.float32)
        for h in range(IDX_HEADS):
            lg = _dot(ik, iqt_ref[0, h * IDX_DIM:(h + 1) * IDX_DIM, :])
            sc = sc + jnp.maximum(lg, 0.0) * w[h:h + 1, :]
        causal = k0 + srow <= tpos
        key = _float_key(jnp.where(causal, sc, -jnp.inf))
        keys_ref[pl.ds(k0, ck), :] = key
        return (jnp.maximum(kmax, fold(key, jnp.maximum)),
                jnp.minimum(kmin, fold(jnp.where(causal, key, int_max), jnp.minimum)))

    kmax, kmin = lax.fori_loop(0, nkc, score_chunk, (jnp.full((SUB, tq), int_min, jnp.int32),
                                                      jnp.full((SUB, tq), int_max, jnp.int32)))
    kmax = jnp.max(kmax, axis=0, keepdims=True)
    kmin = jnp.min(kmin, axis=0, keepdims=True)

    def count(pred):
        def body(kc, accs):
            k0 = pl.multiple_of(kc * ck, ck)
            accs = list(accs)
            chunk = keys_ref.at[pl.ds(k0, ck), :]
            for r in range(ck // SUB):
                k = chunk[r * SUB:(r + 1) * SUB, :]
                hit = pred(k, k0 + srow[r * SUB:(r + 1) * SUB]).astype(jnp.float32)
                accs[r % FOLD_CHAINS] = accs[r % FOLD_CHAINS] + hit
            return tuple(accs)
        accs = lax.fori_loop(0, nkc, body, (jnp.zeros((SUB, tq), jnp.float32),) * FOLD_CHAINS)
        acc = functools.reduce(jnp.add, accs)
        return jnp.sum(acc, axis=0, keepdims=True).astype(jnp.int32)

    k_eff = jnp.minimum(topk, tpos + 1)
    log_keff = jnp.log(k_eff.astype(jnp.float32))

    def unsettled(lo, hi, cnt_lo):
        return (cnt_lo != k_eff) & (hi > lo + 1)

    def candidate(st, kind, bisect_only):
        lo, hi, cnt_lo, cnt_hi = st
        key_mid = (lo >> 1) + (hi >> 1) + (lo & hi & 1)
        if kind == "key_mid":
            cand = key_mid
        elif kind in ("zero", "above_zero"):
            cand = jnp.full((1, tq), 0 if kind == "zero" else 1, jnp.int32)
        else:
            vlo, vhi = _key_float(lo), _key_float(hi)
            if kind == "interpolate":
                a = jnp.log(cnt_lo.astype(jnp.float32) + 0.5)
                b = jnp.log(cnt_hi.astype(jnp.float32) + 0.5)
                cand = _float_key(vlo + (vhi - vlo) * ((a - log_keff) / (a - b)))
            else:
                cand = _float_key(0.5 * vlo + 0.5 * vhi)
            cand = jnp.where(bisect_only, key_mid, cand)
        return jnp.clip(cand, lo + 1, hi - 1)

    def narrow(st, cand, cnt, upd):
        lo, hi, cnt_lo, cnt_hi = st
        up = upd & (cnt >= k_eff)
        dn = upd & (cnt < k_eff)
        return (jnp.where(up, cand, lo), jnp.where(dn, cand, hi),
                jnp.where(up, cnt, cnt_lo), jnp.where(dn, cnt, cnt_hi))

    def search_pass(st, kind, bisect_only):
        cand = candidate(st, kind, bisect_only)
        cnt = count(lambda k, sp: k >= cand)
        return narrow(st, cand, cnt, unsettled(st[0], st[1], st[2]))

    def run_rounds(one_pass, active, st):
        def cond(c):
            return jnp.max(active(c[1:]).astype(jnp.int32)) > 0

        def body(c):
            bisect_only = c[0] >= SEARCH_INTERP_ROUNDS
            st = c[1:]
            for kind in SEARCH_ROUND:
                st = one_pass(st, kind, bisect_only)
            return (c[0] + 1,) + st
        return lax.while_loop(cond, body, (jnp.int32(0),) + st)[1:]

    st = (kmin, kmax + 1, tpos + 1, jnp.zeros((1, tq), jnp.int32))
    for kind in SEARCH_OPENING:
        st = search_pass(st, kind, False)

    tau, hi, cnt_ge, cnt_gt = run_rounds(search_pass, lambda st: unsettled(st[0], st[1], st[2]), st)
    need = k_eff - cnt_gt
    overflow = cnt_ge > k_eff
    p_ref[...] = jnp.full((1, tq), s_len, jnp.int32)

    any_overflow = jnp.max(overflow.astype(jnp.int32)) > 0

    @pl.when(any_overflow)
    def _():
        def pos_step(it, pos):
            cand = pos + jnp.left_shift(jnp.int32(1), (s_len.bit_length() - 1) - it)
            below = count(lambda k, sp: (k == tau) & (sp < cand))
            return jnp.where(below < need, cand, pos)
        pos = lax.fori_loop(0, s_len.bit_length(), pos_step, jnp.zeros((1, tq), jnp.int32))
        p_ref[...] = jnp.where(overflow, pos, s_len)

    last_eq = p_ref[...]

    def write_chunk(kc, carry):
        k0 = pl.multiple_of(kc * ck, ck)

        @pl.when((kc < nkc) & any_overflow)
        def _():
            k = keys_ref[pl.ds(k0, ck), :]
            sel = (k > tau) | ((k == tau) & (k0 + srow <= last_eq))
            mask_ref[0, :, pl.ds(k0, ck)] = jnp.where(sel, 0.0, NEG).T.astype(mask_ref.dtype)

        @pl.when((kc < nkc) & jnp.logical_not(any_overflow))
        def _():
            sel = keys_ref[pl.ds(k0, ck), :] >= tau
            mask_ref[0, :, pl.ds(k0, ck)] = jnp.where(sel, 0.0, NEG).T.astype(mask_ref.dtype)

        @pl.when(kc >= nkc)
        def _():
            mask_ref[0, :, pl.ds(k0, ck)] = jnp.full((tq, ck), NEG, mask_ref.dtype)
        return carry

    lax.fori_loop(0, s_len // ck, write_chunk, 0)


def _indexer_mask(ik, iqt, iwt, topk):
    b, s, di = ik.shape
    ni = iqt.shape[1]
    tq, ck = IDX_TQ, IDX_CK
    return pl.pallas_call(
        functools.partial(_indexer_kernel, tq=tq, ck=ck, topk=topk, s_len=s),
        grid=(b, s // tq),
        in_specs=[
            pl.BlockSpec((1, s, di), lambda bb, i: (bb, 0, 0)),
            pl.BlockSpec((1, ni, tq), lambda bb, i: (bb, 0, i)),
            pl.BlockSpec((1, IDX_HEADS, tq), lambda bb, i: (bb, 0, i)),
        ],
        out_specs=pl.BlockSpec((1, tq, s), lambda bb, i: (bb, i, 0)),
        out_shape=jax.ShapeDtypeStruct((b, s, s), jnp.bfloat16),
        scratch_shapes=[pltpu.VMEM((s, tq), jnp.int32), pltpu.VMEM((1, tq), jnp.int32)],
        compiler_params=_cparams("parallel", "parallel"),
        name="indexer_topk",
    )(ik, iqt, iwt)


def _flash_update(s_cols, m_ref, l_ref, rows):
    m_old = m_ref[rows, :]
    m_cur = functools.reduce(jnp.maximum, s_cols)
    m_new = jnp.maximum(m_old, jnp.max(m_cur, axis=-1, keepdims=True))
    alpha = jnp.exp2(m_old - m_new)
    ps = [jnp.exp2(sc - m_new) for sc in s_cols]
    l_ref[rows, :] = alpha * l_ref[rows, :] + functools.reduce(jnp.add, ps)
    m_ref[rows, :] = m_new
    return jnp.concatenate(ps, axis=1).astype(MXU_DTYPE), alpha


def _row_sums(l):
    ones = jnp.ones((LANE, LANE), MXU_DTYPE)
    hi = l.astype(MXU_DTYPE)
    lo = (l - hi.astype(jnp.float32)).astype(MXU_DTYPE)
    return _dot(hi, ones) + _dot(lo, ones)


def _dispatch_tile(step, i, j, tq, tk):
    assert tq * DIAG_PARTS == tk
    sub_q = tq // LANE
    causal_cols = (i + 1) * tq - j * tk
    for w in range(1, DIAG_PARTS + 1):
        first = (w - 1) * sub_q
        cols = {first + r: r + 2 for r in range(-1 if first else 0, sub_q)}
        pl.when(causal_cols == w * tq)(functools.partial(step, cols, w * tq))
    nsub = tk // LANE
    before_block_start = causal_cols == tk + tq
    pl.when(before_block_start)(functools.partial(step, {nsub - 1: 1}, tk))
    pl.when((causal_cols > tk) & jnp.logical_not(before_block_start))(
        functools.partial(step, {}, tk))


def _dsa_attn_kernel(itab, jtab, q_ref, wuk_ref, ckvt_ref, ckv_ref, mask_ref, nb_ref, wuv_ref,
                     o_ref, qlat_ref, m_ref, l_ref, acc_ref, *, tq, tk, hc):
    pidx = pl.program_id(1)
    i = itab[pidx]
    j = jtab[pidx]
    jlast = (i * tq + tq - 1) // tk
    heads = A_HEADS
    nchunk = heads // hc
    crows = hc * tq

    @pl.when(j == 0)
    def _():
        for h in range(heads):
            qh = q_ref[0, :, h * A_QK_DIM:(h + 1) * A_QK_DIM]
            ql = _dot(qh, wuk_ref[h]) * (A_QK_DIM ** -0.5 * LOG2E)
            qlat_ref[h * tq:(h + 1) * tq, :] = ql.astype(qlat_ref.dtype)
        m_ref[...] = jnp.full(m_ref.shape, M_INIT, jnp.float32)
        l_ref[...] = jnp.zeros(l_ref.shape, jnp.float32)
        acc_ref[...] = jnp.zeros(acc_ref.shape, jnp.float32)

    def step(bias_cols, width):
        nsub = width // LANE
        ckvt = ckvt_ref[0, :, :width]
        ckv = ckv_ref[0, :width, :]
        maskf = mask_ref[0, :, :width].astype(jnp.float32)
        mcols = [maskf[:, c * LANE:(c + 1) * LANE] for c in range(nsub)]
        s_next = _dot(qlat_ref[0:crows, :], ckvt)
        for ci in range(nchunk):
            s_all = s_next
            if ci + 1 < nchunk:
                s_next = _dot(qlat_ref[(ci + 1) * crows:(ci + 2) * crows, :], ckvt)
            ps, alphas = [], []
            for hh in range(hc):
                h = ci * hc + hh
                cols = []
                for c in range(nsub):
                    add = mcols[c] + nb_ref[bias_cols[c], h] if c in bias_cols else mcols[c]
                    cols.append(s_all[hh * tq:(hh + 1) * tq, c * LANE:(c + 1) * LANE] + add)
                p, alpha = _flash_update(cols, m_ref, l_ref, slice(h * tq, (h + 1) * tq))
                ps.append(p)
                alphas.append(alpha)
            pv = _dot(jnp.concatenate(ps, axis=0), ckv)
            a = jnp.concatenate(alphas, axis=0)
            rows = slice(ci * crows, (ci + 1) * crows)
            for half in range(A_KV_RANK // LANE):
                cs = slice(half * LANE, (half + 1) * LANE)
                acc_ref[rows, cs] = a * acc_ref[rows, cs] + pv[:, cs]

    _dispatch_tile(step, i, j, tq, tk)

    @pl.when(j == jlast)
    def _():
        for h in range(heads):
            rows = slice(h * tq, (h + 1) * tq)
            inv_l = 1.0 / _row_sums(l_ref[rows, :])
            o_lat = jnp.concatenate([acc_ref[rows, c * LANE:(c + 1) * LANE] * inv_l
                                     for c in range(A_KV_RANK // LANE)], axis=1)
            oh = _dot(o_lat.astype(MXU_DTYPE), wuv_ref[h])
            o_ref[0, :, h * A_V_DIM:(h + 1) * A_V_DIM] = oh.astype(o_ref.dtype)


def _dsa_attention(q, ckv, ckvt, mask, nb, w_uk, w_uv):
    b, s, nq = q.shape
    tq, tk, hc = DSA_TQ, min(KEY_TILE, s), DSA_HEAD_CHUNK
    itab, jtab = _causal_pairs(s, tq, tk)
    rows = A_HEADS * tq
    once = pl.Buffered(1)
    grid_spec = pltpu.PrefetchScalarGridSpec(
        num_scalar_prefetch=2,
        grid=(b, int(itab.shape[0])),
        in_specs=[
            pl.BlockSpec((1, tq, nq), lambda bb, p, it, jt: (bb, it[p], 0)),
            pl.BlockSpec(w_uk.shape, lambda bb, p, it, jt: (0, 0, 0), pipeline_mode=once),
            pl.BlockSpec((1, A_KV_RANK, tk), lambda bb, p, it, jt: (bb, 0, jt[p])),
            pl.BlockSpec((1, tk, A_KV_RANK), lambda bb, p, it, jt: (bb, jt[p], 0)),
            pl.BlockSpec((1, tq, tk), lambda bb, p, it, jt: (bb, it[p], jt[p])),
            pl.BlockSpec(nb.shape, lambda bb, p, it, jt: (0, 0, 0, 0), pipeline_mode=once),
            pl.BlockSpec(w_uv.shape, lambda bb, p, it, jt: (0, 0, 0), pipeline_mode=once),
        ],
        out_specs=pl.BlockSpec((1, tq, A_HEADS * A_V_DIM), lambda bb, p, it, jt: (bb, it[p], 0)),
        scratch_shapes=[
            pltpu.VMEM((rows, A_KV_RANK), MXU_DTYPE),
            pltpu.VMEM((rows, LANE), jnp.float32),
            pltpu.VMEM((rows, LANE), jnp.float32),
            pltpu.VMEM((rows, A_KV_RANK), jnp.float32),
        ],
    )
    return pl.pallas_call(
        functools.partial(_dsa_attn_kernel, tq=tq, tk=tk, hc=hc),
        grid_spec=grid_spec,
        out_shape=jax.ShapeDtypeStruct((b, s, A_HEADS * A_V_DIM), MXU_DTYPE),
        compiler_params=_cparams("parallel", "arbitrary"),
        name="dsa_attention",
    )(itab, jtab, q, w_uk, ckvt, ckv, mask, nb, w_uv)


def _proj_diff_kernel(x_ref, sc_ref, sh_ref, wq_ref, wk_ref, wv_ref, q_ref, kt_ref, v_ref):
    h = (x_ref[0] * (1.0 + sc_ref[...]) + sh_ref[...]).astype(MXU_DTYPE)
    q = _dot(h, wq_ref[...]) * (B_HEAD_DIM ** -0.5 * LOG2E)
    kt = _dot(h, wk_ref[...]).T
    v = _dot(h, wv_ref[...])
    for m in range(B_MAPS):
        q_ref[0, m] = q[:, m * B_HEAD_DIM:(m + 1) * B_HEAD_DIM].astype(q_ref.dtype)
        kt_ref[0, m] = kt[m * B_HEAD_DIM:(m + 1) * B_HEAD_DIM, :].astype(kt_ref.dtype)
    dv = 2 * B_HEAD_DIM
    for hd in range(B_HEADS):
        v_ref[0, hd] = v[:, hd * dv:(hd + 1) * dv].astype(v_ref.dtype)


def _proj_diff(x, mod, layer, w_in):
    b, s, d = x.shape
    tm = TOKEN_TILE
    n = w_in.shape[1] // 3
    dv = 2 * B_HEAD_DIM
    ws = [w_in[:, k * n:(k + 1) * n].astype(MXU_DTYPE) for k in range(3)]
    return pl.pallas_call(
        _proj_diff_kernel,
        grid=(b, s // tm),
        in_specs=[pl.BlockSpec((1, tm, d), lambda bb, t: (bb, t, 0)),
                  _mod_spec(layer, 1, d), _mod_spec(layer, 0, d)] + [_const_spec((d, n))] * 3,
        out_specs=[
            pl.BlockSpec((1, B_MAPS, tm, B_HEAD_DIM), lambda bb, t: (bb, 0, t, 0)),
            pl.BlockSpec((1, B_MAPS, B_HEAD_DIM, tm), lambda bb, t: (bb, 0, 0, t)),
            pl.BlockSpec((1, B_HEADS, tm, dv), lambda bb, t: (bb, 0, t, 0)),
        ],
        out_shape=[
            jax.ShapeDtypeStruct((b, B_MAPS, s, B_HEAD_DIM), MXU_DTYPE),
            jax.ShapeDtypeStruct((b, B_MAPS, B_HEAD_DIM, s), MXU_DTYPE),
            jax.ShapeDtypeStruct((b, B_HEADS, s, dv), MXU_DTYPE),
        ],
        compiler_params=_cparams("parallel", "parallel"),
        name="proj_diff",
    )(x, mod, mod, *ws)


def _diff_attn_kernel(itab, jtab, q_ref, kt_ref, v_ref, nb_ref, lam_ref, g_ref, o_ref,
                      m_ref, l_ref, acc_ref, *, tq, tk, lam_init):
    pidx = pl.program_id(1)
    i = itab[pidx]
    j = jtab[pidx]
    jlast = (i * tq + tq - 1) // tk
    dv = 2 * B_HEAD_DIM

    @pl.when(j == 0)
    def _():
        m_ref[...] = jnp.full(m_ref.shape, M_INIT, jnp.float32)
        l_ref[...] = jnp.zeros(l_ref.shape, jnp.float32)
        acc_ref[...] = jnp.zeros(acc_ref.shape, jnp.float32)

    def step(bias_cols, width):
        nsub = width // LANE

        def logits(h):
            return [_dot(q_ref[0, 2 * h + mm], kt_ref[0, 2 * h + mm, :, :width]) for mm in range(2)]

        s_next = logits(0)
        for h in range(B_HEADS):
            s_pair = s_next
            if h + 1 < B_HEADS:
                s_next = logits(h + 1)
            ps, alphas = [], []
            for mm in range(2):
                col = 2 * h + mm
                cols = [s_pair[mm][:, c * LANE:(c + 1) * LANE] for c in range(nsub)]
                cols = [cols[c] + nb_ref[bias_cols[c], col] if c in bias_cols else cols[c]
                        for c in range(nsub)]
                p, alpha = _flash_update(cols, m_ref, l_ref, slice(col * tq, (col + 1) * tq))
                ps.append(p)
                alphas.append(alpha)
            pv = _dot(jnp.concatenate(ps, axis=0), v_ref[0, h, :width, :])
            rows = slice(2 * h * tq, (2 * h + 2) * tq)
            acc_ref[rows, :] = jnp.concatenate(alphas, axis=0) * acc_ref[rows, :] + pv

    _dispatch_tile(step, i, j, tq, tk)

    @pl.when(j == jlast)
    def _():
        lam = lam_ref[...]
        lam_full = (jnp.exp(jnp.sum(lam[0:1] * lam[1:2], axis=-1, keepdims=True))
                    - jnp.exp(jnp.sum(lam[2:3] * lam[3:4], axis=-1, keepdims=True)) + lam_init)
        for h in range(B_HEADS):
            r1 = slice(2 * h * tq, (2 * h + 1) * tq)
            r2 = slice((2 * h + 1) * tq, (2 * h + 2) * tq)
            o = (acc_ref[r1, :] / _row_sums(l_ref[r1, :])
                 - lam_full * (acc_ref[r2, :] / _row_sums(l_ref[r2, :])))
            o = o * lax.rsqrt(jnp.mean(o * o, axis=-1, keepdims=True) + LN_EPS) * g_ref[...]
            o_ref[0, :, h * dv:(h + 1) * dv] = (o * (1.0 - lam_init)).astype(o_ref.dtype)


def _diff_attention(qm, ktm, vh, nb, lam, subln_g, lam_init):
    b, maps, s, dh = qm.shape
    tq, tk = DIFF_TQ, min(KEY_TILE, s)
    dv = 2 * B_HEAD_DIM
    assert dv == LANE
    itab, jtab = _causal_pairs(s, tq, tk)
    once = pl.Buffered(1)
    grid_spec = pltpu.PrefetchScalarGridSpec(
        num_scalar_prefetch=2,
        grid=(b, int(itab.shape[0])),
        in_specs=[
            pl.BlockSpec((1, maps, tq, dh), lambda bb, p, it, jt: (bb, 0, it[p], 0)),
            pl.BlockSpec((1, maps, dh, tk), lambda bb, p, it, jt: (bb, 0, 0, jt[p])),
            pl.BlockSpec((1, B_HEADS, tk, dv), lambda bb, p, it, jt: (bb, 0, jt[p], 0)),
            pl.BlockSpec(nb.shape, lambda bb, p, it, jt: (0, 0, 0, 0), pipeline_mode=once),
            pl.BlockSpec(lam.shape, lambda bb, p, it, jt: (0, 0)),
            pl.BlockSpec((1, dv), lambda bb, p, it, jt: (0, 0)),
        ],
        out_specs=pl.BlockSpec((1, tq, B_HEADS * dv), lambda bb, p, it, jt: (bb, it[p], 0)),
        scratch_shapes=[
            pltpu.VMEM((maps * tq, LANE), jnp.float32),
            pltpu.VMEM((maps * tq, LANE), jnp.float32),
            pltpu.VMEM((maps * tq, dv), jnp.float32),
        ],
    )
    return pl.pallas_call(
        functools.partial(_diff_attn_kernel, tq=tq, tk=tk, lam_init=lam_init),
        grid_spec=grid_spec,
        out_shape=jax.ShapeDtypeStruct((b, s, B_HEADS * dv), MXU_DTYPE),
        compiler_params=_cparams("parallel", "arbitrary"),
        name="diff_attention",
    )(itab, jtab, qm, ktm, vh, nb, lam, subln_g.reshape(1, dv))


def _post_mixer_kernel(o_ref, x_ref, gt_ref, sc_ref, sh_ref, gc_ref, wo_ref, w1_ref, w2_ref,
                       lng_ref, lnb_ref, out_ref, *, tf, alpha):
    y = _dot(o_ref[0], wo_ref[...])
    x = _layer_norm_rows(alpha * x_ref[0] + (1.0 + gt_ref[...]) * y, lng_ref[0:1], lnb_ref[0:1])
    h = (x * (1.0 + sc_ref[...]) + sh_ref[...]).astype(MXU_DTYPE)
    y = jnp.zeros(x.shape, jnp.float32)
    for c in range(w1_ref.shape[1] // tf):
        a = jnp.maximum(_dot(h, w1_ref[:, c * tf:(c + 1) * tf]), 0.0)
        y = y + _dot((a * a).astype(MXU_DTYPE), w2_ref[c * tf:(c + 1) * tf, :])
    z = alpha * x + (1.0 + gc_ref[...]) * y
    out_ref[0] = _layer_norm_rows(z, lng_ref[1:2], lnb_ref[1:2])


def _post_mixer(o, x, mod, layer, w_o, w1, w2, ln_g, ln_b, alpha):
    b, s, d = x.shape
    dff = w1.shape[1]
    n_in = o.shape[-1]
    tm, tf = TOKEN_TILE, FF_TILE
    row = lambda bb, t: (bb, t, 0)
    once = pl.Buffered(1)
    return pl.pallas_call(
        functools.partial(_post_mixer_kernel, tf=tf, alpha=alpha),
        grid=(b, s // tm),
        in_specs=[
            pl.BlockSpec((1, tm, n_in), row), pl.BlockSpec((1, tm, d), row),
            _mod_spec(layer, 2, d), _mod_spec(layer, 4, d), _mod_spec(layer, 3, d),
            _mod_spec(layer, 5, d),
            pl.BlockSpec((n_in, d), lambda bb, t: (0, 0), pipeline_mode=once),
            pl.BlockSpec((d, dff), lambda bb, t: (0, 0), pipeline_mode=once),
            pl.BlockSpec((dff, d), lambda bb, t: (0, 0), pipeline_mode=once),
            _const_spec((2, d)), _const_spec((2, d)),
        ],
        out_specs=pl.BlockSpec((1, tm, d), row),
        out_shape=jax.ShapeDtypeStruct((b, s, d), jnp.float32),
        compiler_params=_cparams("parallel", "parallel"),
        name="post_mixer",
    )(o, x, mod, mod, mod, mod, w_o.astype(MXU_DTYPE), w1.astype(MXU_DTYPE),
      w2.astype(MXU_DTYPE), ln_g, ln_b)


def kernel(x, c, rel_bias, ada_w, ada_b, ln_g, ln_b, a_w_in, a_kv_norm, a_w_uk, a_w_uv, a_w_o,
           b_w_in, b_lambda, b_subln, b_w_o, mlp_w1, mlp_w2):
    depth = ada_w.shape[0]
    b, s, d = x.shape
    alpha = (2 * depth) ** 0.25
    topk = min(TOPK_MAX, s // 4)
    mod = _ada_mod(c, ada_w, ada_b)
    nb = {tq: _bias_tiles(rel_bias, tq) for tq in {DSA_TQ, DIFF_TQ}}
    nb_a, nb_b = nb[DSA_TQ], nb[DIFF_TQ]
    for i in range(depth):
        jm = i // N_MIXERS
        if i % N_MIXERS == 0:
            q, ckv, ckvt, iqt, ik, iwt = _proj_dsa(x, mod, i, a_w_in[jm], a_kv_norm[jm])
            mask = _indexer_mask(ik, iqt, iwt, topk)
            o = _dsa_attention(q, ckv, ckvt, mask, nb_a,
                               a_w_uk[jm].astype(MXU_DTYPE), a_w_uv[jm].astype(MXU_DTYPE))
            w_o = a_w_o[jm]
        else:
            lam_init = 0.8 - 0.6 * math.exp(-0.3 * i)
            qm, ktm, vh = _proj_diff(x, mod, i, b_w_in[jm])
            o = _diff_attention(qm, ktm, vh, nb_b, b_lambda[jm], b_subln[jm], lam_init)
            w_o = b_w_o[jm]
        x = _post_mixer(o, x, mod, i, w_o, mlp_w1[i], mlp_w2[i], ln_g[i], ln_b[i], alpha)
    return x
```

```python
import functools
import math

import jax
import jax.numpy as jnp
from jax import lax
from jax.experimental import pallas as pl
from jax.experimental.pallas import tpu as pltpu

N_MIXERS = 2
A_HEADS = 16
A_QK_DIM = 64
A_V_DIM = 64
A_KV_RANK = 256
IDX_HEADS = 8
IDX_DIM = 64
TOPK_MAX = 256
B_HEADS = 8
B_HEAD_DIM = 64
B_MAPS = 2 * B_HEADS
REL_BUCKETS = 32
REL_MAX_DIST = 128
LN_EPS = 1e-5
NEG = -1e30
M_INIT = 0.5 * NEG
LOG2E = math.log2(math.e)
SEARCH_ROUND = ("interpolate", "value_mid", "interpolate", "key_mid")
SEARCH_OPENING = ("zero", "above_zero")
SEARCH_INTERP_ROUNDS = 6
FOLD_CHAINS = 8

LANE = 128
SUB = 8
VMEM_LIMIT = 56 * 1024 * 1024
TOKEN_TILE = 512
FF_TILE = 1024
ADA_TILE = 768
KEY_TILE = 1024
DSA_TQ = 256
DIFF_TQ = 256
DIAG_PARTS = 4
DSA_HEAD_CHUNK = 2
IDX_TQ = 512
IDX_CK = 512

MXU_DTYPE = jnp.bfloat16


def _dot(a, b):
    return jnp.dot(a, b, preferred_element_type=jnp.float32)


def _cparams(*sem):
    return pltpu.CompilerParams(dimension_semantics=sem, vmem_limit_bytes=VMEM_LIMIT)


def _const_spec(shape):
    nd = len(shape)
    return pl.BlockSpec(shape, lambda *_: (0,) * nd)


def _layer_norm_rows(z, g, b):
    mu = jnp.mean(z, axis=-1, keepdims=True)
    zc = z - mu
    var = jnp.mean(zc * zc, axis=-1, keepdims=True)
    return zc * lax.rsqrt(var + LN_EPS) * g + b


def _ada_kernel(ct_ref, w_ref, b_ref, o_ref, *, nb):
    ct = ct_ref[...]
    s = ct * jax.nn.sigmoid(ct)
    w = w_ref[0]
    rows = [jnp.sum(w * s[:, b:b + 1], axis=0, keepdims=True) for b in range(nb)]
    o_ref[0] = jnp.concatenate(rows, axis=0) + b_ref[0]


def _ada_mod(c, ada_w, ada_b):
    depth, d, n = ada_w.shape
    nb = c.shape[0]
    tn = ADA_TILE
    out = pl.pallas_call(
        functools.partial(_ada_kernel, nb=nb),
        grid=(depth, n // tn),
        in_specs=[
            pl.BlockSpec((d, nb), lambda l, j: (0, 0)),
            pl.BlockSpec((1, d, tn), lambda l, j: (l, 0, j)),
            pl.BlockSpec((1, 1, tn), lambda l, j: (l, 0, j)),
        ],
        out_specs=pl.BlockSpec((1, nb, tn), lambda l, j: (l, 0, j)),
        out_shape=jax.ShapeDtypeStruct((depth, nb, n), jnp.float32),
        compiler_params=_cparams("arbitrary", "arbitrary"),
        name="ada_mod",
    )(c.T, ada_w, ada_b.reshape(depth, 1, n))
    return out.reshape(depth, nb, 6, 1, d)


def _mod_spec(layer, k, d):
    return pl.BlockSpec((None, None, None, 1, d), lambda b, *_: (layer, b, k, 0, 0))


def _rel_bucket(dist):
    n = jnp.maximum(dist, 0)
    max_exact = REL_BUCKETS // 2
    nf = jnp.maximum(n, 1).astype(jnp.float32)
    large = max_exact + (jnp.log(nf / max_exact) / math.log(REL_MAX_DIST / max_exact)
                         * (REL_BUCKETS - max_exact)).astype(jnp.int32)
    large = jnp.minimum(large, REL_BUCKETS - 1)
    return jnp.where(n < max_exact, n, large)


def _bias_tile_kernel(tab_ref, bk_ref, o_ref, *, heads):
    bk = bk_ref[0]
    for h in range(heads):
        far = tab_ref[REL_BUCKETS - 1, h]
        acc = jnp.zeros(bk.shape, jnp.float32)
        for b in range(REL_BUCKETS - 1):
            acc = jnp.where(bk == b, (tab_ref[b, h] - far) * LOG2E, acc)
        o_ref[0, h] = jnp.where(bk < 0, NEG, acc)


def _bias_tiles(rel_bias, tq):
    heads = rel_bias.shape[1]
    r_sub = tq // LANE
    rels = jnp.arange(-2, r_sub + 1, dtype=jnp.int32)
    ql = jnp.arange(tq, dtype=jnp.int32)[None, :, None]
    sl = jnp.arange(LANE, dtype=jnp.int32)[None, None, :]
    dist = ql - rels[:, None, None] * LANE - sl
    bk = jnp.where(dist < 0, -1, _rel_bucket(dist))
    nk = r_sub + 3
    return pl.pallas_call(
        functools.partial(_bias_tile_kernel, heads=heads),
        grid=(nk,),
        in_specs=[
            pl.BlockSpec(memory_space=pltpu.SMEM),
            pl.BlockSpec((1, tq, LANE), lambda k: (k, 0, 0)),
        ],
        out_specs=pl.BlockSpec((1, heads, tq, LANE), lambda k: (k, 0, 0, 0)),
        out_shape=jax.ShapeDtypeStruct((nk, heads, tq, LANE), jnp.float32),
        compiler_params=_cparams("arbitrary"),
        name="bias_tiles",
    )(rel_bias, bk)


def _causal_pairs(s, tq, tk):
    ii, jj = [], []
    for i in range(s // tq):
        for j in range((i * tq + tq - 1) // tk + 1):
            ii.append(i)
            jj.append(j)
    return jnp.asarray(ii, jnp.int32), jnp.asarray(jj, jnp.int32)


def _proj_dsa_kernel(x_ref, sc_ref, sh_ref, wq_ref, wc_ref, wi_ref, wk_ref, kvg_ref,
                     q_ref, ckv_ref, ckvt_ref, iqt_ref, ik_ref, iwt_ref):
    h = (x_ref[0] * (1.0 + sc_ref[...]) + sh_ref[...]).astype(MXU_DTYPE)
    q_ref[0] = _dot(h, wq_ref[...]).astype(q_ref.dtype)
    ckv = _dot(h, wc_ref[...])
    ckv = ckv * lax.rsqrt(jnp.mean(ckv * ckv, axis=-1, keepdims=True) + LN_EPS) * kvg_ref[...]
    ckv_ref[0] = ckv.astype(ckv_ref.dtype)
    ckvt_ref[0] = ckv.T.astype(ckvt_ref.dtype)
    iqt_ref[0] = _dot(h, wi_ref[...]).T.astype(iqt_ref.dtype)
    ikw = _dot(h, wk_ref[...])
    ik_ref[0] = ikw[:, :IDX_DIM].astype(ik_ref.dtype)
    iwt_ref[0] = ikw.T[IDX_DIM:IDX_DIM + IDX_HEADS, :]


def _proj_dsa(x, mod, layer, w_in, kv_norm):
    b, s, d = x.shape
    tm = TOKEN_TILE
    nq = A_HEADS * A_QK_DIM
    ni = IDX_HEADS * IDX_DIM
    o1, o2, o3 = nq, nq + A_KV_RANK, nq + A_KV_RANK + ni
    wq = w_in[:, :o1].astype(MXU_DTYPE)
    wc = w_in[:, o1:o2].astype(MXU_DTYPE)
    wi = w_in[:, o2:o3].astype(MXU_DTYPE)
    wk = jnp.pad(w_in[:, o3:], ((0, 0), (0, LANE - (w_in.shape[1] - o3)))).astype(MXU_DTYPE)
    row = lambda bb, t: (bb, t, 0)
    col = lambda bb, t: (bb, 0, t)
    return pl.pallas_call(
        _proj_dsa_kernel,
        grid=(b, s // tm),
        in_specs=[
            pl.BlockSpec((1, tm, d), row),
            _mod_spec(layer, 1, d), _mod_spec(layer, 0, d),
            _const_spec((d, nq)), _const_spec((d, A_KV_RANK)), _const_spec((d, ni)),
            _const_spec((d, LANE)), _const_spec((1, A_KV_RANK)),
        ],
        out_specs=[
            pl.BlockSpec((1, tm, nq), row), pl.BlockSpec((1, tm, A_KV_RANK), row),
            pl.BlockSpec((1, A_KV_RANK, tm), col), pl.BlockSpec((1, ni, tm), col),
            pl.BlockSpec((1, tm, IDX_DIM), row), pl.BlockSpec((1, IDX_HEADS, tm), col),
        ],
        out_shape=[
            jax.ShapeDtypeStruct((b, s, nq), MXU_DTYPE),
            jax.ShapeDtypeStruct((b, s, A_KV_RANK), MXU_DTYPE),
            jax.ShapeDtypeStruct((b, A_KV_RANK, s), MXU_DTYPE),
            jax.ShapeDtypeStruct((b, ni, s), MXU_DTYPE),
            jax.ShapeDtypeStruct((b, s, IDX_DIM), MXU_DTYPE),
            jax.ShapeDtypeStruct((b, IDX_HEADS, s), jnp.float32),
        ],
        compiler_params=_cparams("parallel", "parallel"),
        name="proj_dsa",
    )(x, mod, mod, wq, wc, wi, wk, kv_norm.reshape(1, A_KV_RANK))


def _float_key(v):
    bits = pltpu.bitcast(v, jnp.int32)
    return bits ^ ((bits >> 31) & 0x7FFFFFFF)


def _key_float(k):
    return pltpu.bitcast(k ^ ((k >> 31) & 0x7FFFFFFF), jnp.float32)


def _indexer_kernel(ik_ref, iqt_ref, iwt_ref, mask_ref, keys_ref, p_ref, *, tq, ck, topk, s_len):
    i = pl.program_id(1)
    q0 = i * tq
    nkc = (q0 + tq + ck - 1) // ck
    w = iwt_ref[0] * (IDX_HEADS ** -0.5 * IDX_DIM ** -0.5)
    tpos = q0 + lax.broadcasted_iota(jnp.int32, (1, tq), 1)
    srow = lax.broadcasted_iota(jnp.int32, (ck, 1), 0)
    int_max = jnp.int32(2 ** 31 - 1)
    int_min = jnp.int32(-2 ** 31)

    def fold(x, op):
        slabs = [x[r * SUB:(r + 1) * SUB] for r in range(ck // SUB)]
        lanes = FOLD_CHAINS
        chains = slabs[:lanes]
        for r, slab in enumerate(slabs[lanes:]):
            chains[r % lanes] = op(chains[r % lanes], slab)
        slabs = chains
        while len(slabs) > 1:
            slabs = [op(a, b) for a, b in zip(slabs[0::2], slabs[1::2])]
        return slabs[0]

    def score_chunk(kc, carry):
        kmax, kmin = carry
        k0 = pl.multiple_of(kc * ck, ck)
        ik = ik_ref[0, pl.ds(k0, ck), :]
        sc = jnp.zeros((ck, tq), jnp.float32)
        for h in range(IDX_HEADS):
            lg = _dot(ik, iqt_ref[0, h * IDX_DIM:(h + 1) * IDX_DIM, :])
            sc = sc + jnp.maximum(lg, 0.0) * w[h:h + 1, :]
        causal = k0 + srow <= tpos
        key = _float_key(jnp.where(causal, sc, -jnp.inf))
        keys_ref[pl.ds(k0, ck), :] = key
        return (jnp.maximum(kmax, fold(key, jnp.maximum)),
                jnp.minimum(kmin, fold(jnp.where(causal, key, int_max), jnp.minimum)))

    kmax, kmin = lax.fori_loop(0, nkc, score_chunk, (jnp.full((SUB, tq), int_min, jnp.int32),
                                                      jnp.full((SUB, tq), int_max, jnp.int32)))
    kmax = jnp.max(kmax, axis=0, keepdims=True)
    kmin = jnp.min(kmin, axis=0, keepdims=True)

    def count(pred):
        def body(kc, accs):
            k0 = pl.multiple_of(kc * ck, ck)
            accs = list(accs)
            chunk = keys_ref.at[pl.ds(k0, ck), :]
            for r in range(ck // SUB):
                k = chunk[r * SUB:(r + 1) * SUB, :]
                hit = pred(k, k0 + srow[r * SUB:(r + 1) * SUB]).astype(jnp.float32)
                accs[r % FOLD_CHAINS] = accs[r % FOLD_CHAINS] + hit
            return tuple(accs)
        accs = lax.fori_loop(0, nkc, body, (jnp.zeros((SUB, tq), jnp.float32),) * FOLD_CHAINS)
        acc = functools.reduce(jnp.add, accs)
        return jnp.sum(acc, axis=0, keepdims=True).astype(jnp.int32)

    k_eff = jnp.minimum(topk, tpos + 1)
    log_keff = jnp.log(k_eff.astype(jnp.float32))

    def unsettled(lo, hi, cnt_lo):
        return (cnt_lo != k_eff) & (hi > lo + 1)

    def candidate(st, kind, bisect_only):
        lo, hi, cnt_lo, cnt_hi = st
        key_mid = (lo >> 1) + (hi >> 1) + (lo & hi & 1)
        if kind == "key_mid":
            cand = key_mid
        elif kind in ("zero", "above_zero"):
            cand = jnp.full((1, tq), 0 if kind == "zero" else 1, jnp.int32)
        else:
            vlo, vhi = _key_float(lo), _key_float(hi)
            if kind == "interpolate":
                a = jnp.log(cnt_lo.astype(jnp.float32) + 0.5)
                b = jnp.log(cnt_hi.astype(jnp.float32) + 0.5)
                cand = _float_key(vlo + (vhi - vlo) * ((a - log_keff) / (a - b)))
            else:
                cand = _float_key(0.5 * vlo + 0.5 * vhi)
            cand = jnp.where(bisect_only, key_mid, cand)
        return jnp.clip(cand, lo + 1, hi - 1)

    def narrow(st, cand, cnt, upd):
        lo, hi, cnt_lo, cnt_hi = st
        up = upd & (cnt >= k_eff)
        dn = upd & (cnt < k_eff)
        return (jnp.where(up, cand, lo), jnp.where(dn, cand, hi),
                jnp.where(up, cnt, cnt_lo), jnp.where(dn, cnt, cnt_hi))

    def search_pass(st, kind, bisect_only):
        cand = candidate(st, kind, bisect_only)
        cnt = count(lambda k, sp: k >= cand)
        return narrow(st, cand, cnt, unsettled(st[0], st[1], st[2]))

    def run_rounds(one_pass, active, st):
        def cond(c):
            return jnp.max(active(c[1:]).astype(jnp.int32)) > 0

        def body(c):
            bisect_only = c[0] >= SEARCH_INTERP_ROUNDS
            st = c[1:]
            for kind in SEARCH_ROUND:
                st = one_pass(st, kind, bisect_only)
            return (c[0] + 1,) + st
        return lax.while_loop(cond, body, (jnp.int32(0),) + st)[1:]

    st = (kmin, kmax + 1, tpos + 1, jnp.zeros((1, tq), jnp.int32))
    for kind in SEARCH_OPENING:
        st = search_pass(st, kind, False)

    tau, hi, cnt_ge, cnt_gt = run_rounds(search_pass, lambda st: unsettled(st[0], st[1], st[2]), st)
    need = k_eff - cnt_gt
    overflow = cnt_ge > k_eff
    p_ref[...] = jnp.full((1, tq), s_len, jnp.int32)

    any_overflow = jnp.max(overflow.astype(jnp.int32)) > 0

    @pl.when(any_overflow)
    def _():
        def pos_step(it, pos):
            cand = pos + jnp.left_shift(jnp.int32(1), (s_len.bit_length() - 1) - it)
            below = count(lambda k, sp: (k == tau) & (sp < cand))
            return jnp.where(below < need, cand, pos)
        pos = lax.fori_loop(0, s_len.bit_length(), pos_step, jnp.zeros((1, tq), jnp.int32))
        p_ref[...] = jnp.where(overflow, pos, s_len)

    last_eq = p_ref[...]

    def write_chunk(kc, carry):
        k0 = pl.multiple_of(kc * ck, ck)

        @pl.when((kc < nkc) & any_overflow)
        def _():
            k = keys_ref[pl.ds(k0, ck), :]
            sel = (k > tau) | ((k == tau) & (k0 + srow <= last_eq))
            mask_ref[0, :, pl.ds(k0, ck)] = jnp.where(sel, 0.0, NEG).T.astype(mask_ref.dtype)

        @pl.when((kc < nkc) & jnp.logical_not(any_overflow))
        def _():
            sel = keys_ref[pl.ds(k0, ck), :] >= tau
            mask_ref[0, :, pl.ds(k0, ck)] = jnp.where(sel, 0.0, NEG).T.astype(mask_ref.dtype)

        @pl.when(kc >= nkc)
        def _():
            mask_ref[0, :, pl.ds(k0, ck)] = jnp.full((tq, ck), NEG, mask_ref.dtype)
        return carry

    lax.fori_loop(0, s_len // ck, write_chunk, 0)


def _indexer_mask(ik, iqt, iwt, topk):
    b, s, di = ik.shape
    ni = iqt.shape[1]
    tq, ck = IDX_TQ, IDX_CK
    return pl.pallas_call(
        functools.partial(_indexer_kernel, tq=tq, ck=ck, topk=topk, s_len=s),
        grid=(b, s // tq),
        in_specs=[
            pl.BlockSpec((1, s, di), lambda bb, i: (bb, 0, 0)),
            pl.BlockSpec((1, ni, tq), lambda bb, i: (bb, 0, i)),
            pl.BlockSpec((1, IDX_HEADS, tq), lambda bb, i: (bb, 0, i)),
        ],
        out_specs=pl.BlockSpec((1, tq, s), lambda bb, i: (bb, i, 0)),
        out_shape=jax.ShapeDtypeStruct((b, s, s), jnp.bfloat16),
        scratch_shapes=[pltpu.VMEM((s, tq), jnp.int32), pltpu.VMEM((1, tq), jnp.int32)],
        compiler_params=_cparams("parallel", "parallel"),
        name="indexer_topk",
    )(ik, iqt, iwt)


def _flash_update(s_cols, m_ref, l_ref, rows):
    m_old = m_ref[rows, :]
    m_cur = functools.reduce(jnp.maximum, s_cols)
    m_new = jnp.maximum(m_old, jnp.max(m_cur, axis=-1, keepdims=True))
    alpha = jnp.exp2(m_old - m_new)
    ps = [jnp.exp2(sc - m_new) for sc in s_cols]
    l_ref[rows, :] = alpha * l_ref[rows, :] + functools.reduce(jnp.add, ps)
    m_ref[rows, :] = m_new
    return jnp.concatenate(ps, axis=1).astype(MXU_DTYPE), alpha


def _row_sums(l):
    ones = jnp.ones((LANE, LANE), MXU_DTYPE)
    hi = l.astype(MXU_DTYPE)
    lo = (l - hi.astype(jnp.float32)).astype(MXU_DTYPE)
    return _dot(hi, ones) + _dot(lo, ones)


def _dispatch_tile(step, i, j, tq, tk):
    assert tq * DIAG_PARTS == tk
    sub_q = tq // LANE
    causal_cols = (i + 1) * tq - j * tk
    for w in range(1, DIAG_PARTS + 1):
        first = (w - 1) * sub_q
        cols = {first + r: r + 2 for r in range(-1 if first else 0, sub_q)}
        pl.when(causal_cols == w * tq)(functools.partial(step, cols, w * tq))
    nsub = tk // LANE
    before_block_start = causal_cols == tk + tq
    pl.when(before_block_start)(functools.partial(step, {nsub - 1: 1}, tk))
    pl.when((causal_cols > tk) & jnp.logical_not(before_block_start))(
        functools.partial(step, {}, tk))


def _dsa_attn_kernel(itab, jtab, q_ref, wuk_ref, ckvt_ref, ckv_ref, mask_ref, nb_ref, wuv_ref,
                     o_ref, qlat_ref, m_ref, l_ref, acc_ref, *, tq, tk, hc):
    pidx = pl.program_id(1)
    i = itab[pidx]
    j = jtab[pidx]
    jlast = (i * tq + tq - 1) // tk
    heads = A_HEADS
    nchunk = heads // hc
    crows = hc * tq

    @pl.when(j == 0)
    def _():
        for h in range(heads):
            qh = q_ref[0, :, h * A_QK_DIM:(h + 1) * A_QK_DIM]
            ql = _dot(qh, wuk_ref[h]) * (A_QK_DIM ** -0.5 * LOG2E)
            qlat_ref[h * tq:(h + 1) * tq, :] = ql.astype(qlat_ref.dtype)
        m_ref[...] = jnp.full(m_ref.shape, M_INIT, jnp.float32)
        l_ref[...] = jnp.zeros(l_ref.shape, jnp.float32)
        acc_ref[...] = jnp.zeros(acc_ref.shape, jnp.float32)

    def step(bias_cols, width):
        nsub = width // LANE
        ckvt = ckvt_ref[0, :, :width]
        ckv = ckv_ref[0, :width, :]
        maskf = mask_ref[0, :, :width].astype(jnp.float32)
        mcols = [maskf[:, c * LANE:(c + 1) * LANE] for c in range(nsub)]
        s_next = _dot(qlat_ref[0:crows, :], ckvt)
        for ci in range(nchunk):
            s_all = s_next
            if ci + 1 < nchunk:
                s_next = _dot(qlat_ref[(ci + 1) * crows:(ci + 2) * crows, :], ckvt)
            ps, alphas = [], []
            for hh in range(hc):
                h = ci * hc + hh
                cols = []
                for c in range(nsub):
                    add = mcols[c] + nb_ref[bias_cols[c], h] if c in bias_cols else mcols[c]
                    cols.append(s_all[hh * tq:(hh + 1) * tq, c * LANE:(c + 1) * LANE] + add)
                p, alpha = _flash_update(cols, m_ref, l_ref, slice(h * tq, (h + 1) * tq))
                ps.append(p)
                alphas.append(alpha)
            pv = _dot(jnp.concatenate(ps, axis=0), ckv)
            a = jnp.concatenate(alphas, axis=0)
            rows = slice(ci * crows, (ci + 1) * crows)
            for half in range(A_KV_RANK // LANE):
                cs = slice(half * LANE, (half + 1) * LANE)
                acc_ref[rows, cs] = a * acc_ref[rows, cs] + pv[:, cs]

    _dispatch_tile(step, i, j, tq, tk)

    @pl.when(j == jlast)
    def _():
        for h in range(heads):
            rows = slice(h * tq, (h + 1) * tq)
            inv_l = 1.0 / _row_sums(l_ref[rows, :])
            o_lat = jnp.concatenate([acc_ref[rows, c * LANE:(c + 1) * LANE] * inv_l
                                     for c in range(A_KV_RANK // LANE)], axis=1)
            oh = _dot(o_lat.astype(MXU_DTYPE), wuv_ref[h])
            o_ref[0, :, h * A_V_DIM:(h + 1) * A_V_DIM] = oh.astype(o_ref.dtype)


def _dsa_attention(q, ckv, ckvt, mask, nb, w_uk, w_uv):
    b, s, nq = q.shape
    tq, tk, hc = DSA_TQ, min(KEY_TILE, s), DSA_HEAD_CHUNK
    itab, jtab = _causal_pairs(s, tq, tk)
    rows = A_HEADS * tq
    once = pl.Buffered(1)
    grid_spec = pltpu.PrefetchScalarGridSpec(
        num_scalar_prefetch=2,
        grid=(b, int(itab.shape[0])),
        in_specs=[
            pl.BlockSpec((1, tq, nq), lambda bb, p, it, jt: (bb, it[p], 0)),
            pl.BlockSpec(w_uk.shape, lambda bb, p, it, jt: (0, 0, 0), pipeline_mode=once),
            pl.BlockSpec((1, A_KV_RANK, tk), lambda bb, p, it, jt: (bb, 0, jt[p])),
            pl.BlockSpec((1, tk, A_KV_RANK), lambda bb, p, it, jt: (bb, jt[p], 0)),
            pl.BlockSpec((1, tq, tk), lambda bb, p, it, jt: (bb, it[p], jt[p])),
            pl.BlockSpec(nb.shape, lambda bb, p, it, jt: (0, 0, 0, 0), pipeline_mode=once),
            pl.BlockSpec(w_uv.shape, lambda bb, p, it, jt: (0, 0, 0), pipeline_mode=once),
        ],
        out_specs=pl.BlockSpec((1, tq, A_HEADS * A_V_DIM), lambda bb, p, it, jt: (bb, it[p], 0)),
        scratch_shapes=[
            pltpu.VMEM((rows, A_KV_RANK), MXU_DTYPE),
            pltpu.VMEM((rows, LANE), jnp.float32),
            pltpu.VMEM((rows, LANE), jnp.float32),
            pltpu.VMEM((rows, A_KV_RANK), jnp.float32),
        ],
    )
    return pl.pallas_call(
        functools.partial(_dsa_attn_kernel, tq=tq, tk=tk, hc=hc),
        grid_spec=grid_spec,
        out_shape=jax.ShapeDtypeStruct((b, s, A_HEADS * A_V_DIM), MXU_DTYPE),
        compiler_params=_cparams("parallel", "arbitrary"),
        name="dsa_attention",
    )(itab, jtab, q, w_uk, ckvt, ckv, mask, nb, w_uv)


def _proj_diff_kernel(x_ref, sc_ref, sh_ref, wq_ref, wk_ref, wv_ref, q_ref, kt_ref, v_ref):
    h = (x_ref[0] * (1.0 + sc_ref[...]) + sh_ref[...]).astype(MXU_DTYPE)
    q = _dot(h, wq_ref[...]) * (B_HEAD_DIM ** -0.5 * LOG2E)
    kt = _dot(h, wk_ref[...]).T
    v = _dot(h, wv_ref[...])
    for m in range(B_MAPS):
        q_ref[0, m] = q[:, m * B_HEAD_DIM:(m + 1) * B_HEAD_DIM].astype(q_ref.dtype)
        kt_ref[0, m] = kt[m * B_HEAD_DIM:(m + 1) * B_HEAD_DIM, :].astype(kt_ref.dtype)
    dv = 2 * B_HEAD_DIM
    for hd in range(B_HEADS):
        v_ref[0, hd] = v[:, hd * dv:(hd + 1) * dv].astype(v_ref.dtype)


def _proj_diff(x, mod, layer, w_in):
    b, s, d = x.shape
    tm = TOKEN_TILE
    n = w_in.shape[1] // 3
    dv = 2 * B_HEAD_DIM
    ws = [w_in[:, k * n:(k + 1) * n].astype(MXU_DTYPE) for k in range(3)]
    return pl.pallas_call(
        _proj_diff_kernel,
        grid=(b, s // tm),
        in_specs=[pl.BlockSpec((1, tm, d), lambda bb, t: (bb, t, 0)),
                  _mod_spec(layer, 1, d), _mod_spec(layer, 0, d)] + [_const_spec((d, n))] * 3,
        out_specs=[
            pl.BlockSpec((1, B_MAPS, tm, B_HEAD_DIM), lambda bb, t: (bb, 0, t, 0)),
            pl.BlockSpec((1, B_MAPS, B_HEAD_DIM, tm), lambda bb, t: (bb, 0, 0, t)),
            pl.BlockSpec((1, B_HEADS, tm, dv), lambda bb, t: (bb, 0, t, 0)),
        ],
        out_shape=[
            jax.ShapeDtypeStruct((b, B_MAPS, s, B_HEAD_DIM), MXU_DTYPE),
            jax.ShapeDtypeStruct((b, B_MAPS, B_HEAD_DIM, s), MXU_DTYPE),
            jax.ShapeDtypeStruct((b, B_HEADS, s, dv), MXU_DTYPE),
        ],
        compiler_params=_cparams("parallel", "parallel"),
        name="proj_diff",
    )(x, mod, mod, *ws)


def _diff_attn_kernel(itab, jtab, q_ref, kt_ref, v_ref, nb_ref, lam_ref, g_ref, o_ref,
                      m_ref, l_ref, acc_ref, *, tq, tk, lam_init):
    pidx = pl.program_id(1)
    i = itab[pidx]
    j = jtab[pidx]
    jlast = (i * tq + tq - 1) // tk
    dv = 2 * B_HEAD_DIM

    @pl.when(j == 0)
    def _():
        m_ref[...] = jnp.full(m_ref.shape, M_INIT, jnp.float32)
        l_ref[...] = jnp.zeros(l_ref.shape, jnp.float32)
        acc_ref[...] = jnp.zeros(acc_ref.shape, jnp.float32)

    def step(bias_cols, width):
        nsub = width // LANE

        def logits(h):
            return [_dot(q_ref[0, 2 * h + mm], kt_ref[0, 2 * h + mm, :, :width]) for mm in range(2)]

        s_next = logits(0)
        for h in range(B_HEADS):
            s_pair = s_next
            if h + 1 < B_HEADS:
                s_next = logits(h + 1)
            ps, alphas = [], []
            for mm in range(2):
                col = 2 * h + mm
                cols = [s_pair[mm][:, c * LANE:(c + 1) * LANE] for c in range(nsub)]
                cols = [cols[c] + nb_ref[bias_cols[c], col] if c in bias_cols else cols[c]
                        for c in range(nsub)]
                p, alpha = _flash_update(cols, m_ref, l_ref, slice(col * tq, (col + 1) * tq))
                ps.append(p)
                alphas.append(alpha)
            pv = _dot(jnp.concatenate(ps, axis=0), v_ref[0, h, :width, :])
            rows = slice(2 * h * tq, (2 * h + 2) * tq)
            acc_ref[rows, :] = jnp.concatenate(alphas, axis=0) * acc_ref[rows, :] + pv

    _dispatch_tile(step, i, j, tq, tk)

    @pl.when(j == jlast)
    def _():
        lam = lam_ref[...]
        lam_full = (jnp.exp(jnp.sum(lam[0:1] * lam[1:2], axis=-1, keepdims=True))
                    - jnp.exp(jnp.sum(lam[2:3] * lam[3:4], axis=-1, keepdims=True)) + lam_init)
        for h in range(B_HEADS):
            r1 = slice(2 * h * tq, (2 * h + 1) * tq)
            r2 = slice((2 * h + 1) * tq, (2 * h + 2) * tq)
            o = (acc_ref[r1, :] / _row_sums(l_ref[r1, :])
                 - lam_full * (acc_ref[r2, :] / _row_sums(l_ref[r2, :])))
            o = o * lax.rsqrt(jnp.mean(o * o, axis=-1, keepdims=True) + LN_EPS) * g_ref[...]
            o_ref[0, :, h * dv:(h + 1) * dv] = (o * (1.0 - lam_init)).astype(o_ref.dtype)


def _diff_attention(qm, ktm, vh, nb, lam, subln_g, lam_init):
    b, maps, s, dh = qm.shape
    tq, tk = DIFF_TQ, min(KEY_TILE, s)
    dv = 2 * B_HEAD_DIM
    assert dv == LANE
    itab, jtab = _causal_pairs(s, tq, tk)
    once = pl.Buffered(1)
    grid_spec = pltpu.PrefetchScalarGridSpec(
        num_scalar_prefetch=2,
        grid=(b, int(itab.shape[0])),
        in_specs=[
            pl.BlockSpec((1, maps, tq, dh), lambda bb, p, it, jt: (bb, 0, it[p], 0)),
            pl.BlockSpec((1, maps, dh, tk), lambda bb, p, it, jt: (bb, 0, 0, jt[p])),
            pl.BlockSpec((1, B_HEADS, tk, dv), lambda bb, p, it, jt: (bb, 0, jt[p], 0)),
            pl.BlockSpec(nb.shape, lambda bb, p, it, jt: (0, 0, 0, 0), pipeline_mode=once),
            pl.BlockSpec(lam.shape, lambda bb, p, it, jt: (0, 0)),
            pl.BlockSpec((1, dv), lambda bb, p, it, jt: (0, 0)),
        ],
        out_specs=pl.BlockSpec((1, tq, B_HEADS * dv), lambda bb, p, it, jt: (bb, it[p], 0)),
        scratch_shapes=[
            pltpu.VMEM((maps * tq, LANE), jnp.float32),
            pltpu.VMEM((maps * tq, LANE), jnp.float32),
            pltpu.VMEM((maps * tq, dv), jnp.float32),
        ],
    )
    return pl.pallas_call(
        functools.partial(_diff_attn_kernel, tq=tq, tk=tk, lam_init=lam_init),
        grid_spec=grid_spec,
        out_shape=jax.ShapeDtypeStruct((b, s, B_HEADS * dv), MXU_DTYPE),
        compiler_params=_cparams("parallel", "arbitrary"),
        name="diff_attention",
    )(itab, jtab, qm, ktm, vh, nb, lam, subln_g.reshape(1, dv))


def _post_mixer_kernel(o_ref, x_ref, gt_ref, sc_ref, sh_ref, gc_ref, wo_ref, w1_ref, w2_ref,
                       lng_ref, lnb_ref, out_ref, *, tf, alpha):
    y = _dot(o_ref[0], wo_ref[...])
    x = _layer_norm_rows(alpha * x_ref[0] + (1.0 + gt_ref[...]) * y, lng_ref[0:1], lnb_ref[0:1])
    h = (x * (1.0 + sc_ref[...]) + sh_ref[...]).astype(MXU_DTYPE)
    y = jnp.zeros(x.shape, jnp.float32)
    for c in range(w1_ref.shape[1] // tf):
        a = jnp.maximum(_dot(h, w1_ref[:, c * tf:(c + 1) * tf]), 0.0)
        y = y + _dot((a * a).astype(MXU_DTYPE), w2_ref[c * tf:(c + 1) * tf, :])
    z = alpha * x + (1.0 + gc_ref[...]) * y
    out_ref[0] = _layer_norm_rows(z, lng_ref[1:2], lnb_ref[1:2])


def _post_mixer(o, x, mod, layer, w_o, w1, w2, ln_g, ln_b, alpha):
    b, s, d = x.shape
    dff = w1.shape[1]
    n_in = o.shape[-1]
    tm, tf = TOKEN_TILE, FF_TILE
    row = lambda bb, t: (bb, t, 0)
    once = pl.Buffered(1)
    return pl.pallas_call(
        functools.partial(_post_mixer_kernel, tf=tf, alpha=alpha),
        grid=(b, s // tm),
        in_specs=[
            pl.BlockSpec((1, tm, n_in), row), pl.BlockSpec((1, tm, d), row),
            _mod_spec(layer, 2, d), _mod_spec(layer, 4, d), _mod_spec(layer, 3, d),
            _mod_spec(layer, 5, d),
            pl.BlockSpec((n_in, d), lambda bb, t: (0, 0), pipeline_mode=once),
            pl.BlockSpec((d, dff), lambda bb, t: (0, 0), pipeline_mode=once),
            pl.BlockSpec((dff, d), lambda bb, t: (0, 0), pipeline_mode=once),
            _const_spec((2, d)), _const_spec((2, d)),
        ],
        out_specs=pl.BlockSpec((1, tm, d), row),
        out_shape=jax.ShapeDtypeStruct((b, s, d), jnp.float32),
        compiler_params=_cparams("parallel", "parallel"),
        name="post_mixer",
    )(o, x, mod, mod, mod, mod, w_o.astype(MXU_DTYPE), w1.astype(MXU_DTYPE),
      w2.astype(MXU_DTYPE), ln_g, ln_b)


def kernel(x, c, rel_bias, ada_w, ada_b, ln_g, ln_b, a_w_in, a_kv_norm, a_w_uk, a_w_uv, a_w_o,
           b_w_in, b_lambda, b_subln, b_w_o, mlp_w1, mlp_w2):
    depth = ada_w.shape[0]
    b, s, d = x.shape
    alpha = (2 * depth) ** 0.25
    topk = min(TOPK_MAX, s // 4)
    mod = _ada_mod(c, ada_w, ada_b)
    nb = {tq: _bias_tiles(rel_bias, tq) for tq in {DSA_TQ, DIFF_TQ}}
    nb_a, nb_b = nb[DSA_TQ], nb[DIFF_TQ]
    for i in range(depth):
        jm = i // N_MIXERS
        if i % N_MIXERS == 0:
            q, ckv, ckvt, iqt, ik, iwt = _proj_dsa(x, mod, i, a_w_in[jm], a_kv_norm[jm])
            mask = _indexer_mask(ik, iqt, iwt, topk)
            o = _dsa_attention(q, ckv, ckvt, mask, nb_a,
                               a_w_uk[jm].astype(MXU_DTYPE), a_w_uv[jm].astype(MXU_DTYPE))
            w_o = a_w_o[jm]
        else:
            lam_init = 0.8 - 0.6 * math.exp(-0.3 * i)
            qm, ktm, vh = _proj_diff(x, mod, i, b_w_in[jm])
            o = _diff_attention(qm, ktm, vh, nb_b, b_lambda[jm], b_subln[jm], lam_init)
            w_o = b_w_o[jm]
        x = _post_mixer(o, x, mod, i, w_o, mlp_w1[i], mlp_w2[i], ln_g[i], ln_b[i], alpha)
    return x
```

```python
import functools
import math

import jax
import jax.numpy as jnp
from jax import lax
from jax.experimental import pallas as pl
from jax.experimental.pallas import tpu as pltpu

N_MIXERS = 2
A_HEADS = 16
A_QK_DIM = 64
A_V_DIM = 64
A_KV_RANK = 256
IDX_HEADS = 8
IDX_DIM = 64
TOPK_MAX = 256
B_HEADS = 8
B_HEAD_DIM = 64
B_MAPS = 2 * B_HEADS
REL_BUCKETS = 32
REL_MAX_DIST = 128
LN_EPS = 1e-5
NEG = -1e30
M_INIT = 0.5 * NEG
LOG2E = math.log2(math.e)
SEARCH_ROUND = ("interpolate", "value_mid", "interpolate", "key_mid")
SEARCH_OPENING = ("zero", "above_zero")
SEARCH_INTERP_ROUNDS = 6
FOLD_CHAINS = 8

LANE = 128
SUB = 8
VMEM_LIMIT = 56 * 1024 * 1024
TOKEN_TILE = 512
FF_TILE = 1024
ADA_TILE = 768
KEY_TILE = 1024
DSA_TQ = 256
DIFF_TQ = 256
DIAG_PARTS = 4
DSA_HEAD_CHUNK = 2
IDX_TQ = 256
IDX_CK = 512

MXU_DTYPE = jnp.bfloat16


def _dot(a, b):
    return jnp.dot(a, b, preferred_element_type=jnp.float32)


def _cparams(*sem):
    return pltpu.CompilerParams(dimension_semantics=sem, vmem_limit_bytes=VMEM_LIMIT)


def _const_spec(shape):
    nd = len(shape)
    return pl.BlockSpec(shape, lambda *_: (0,) * nd)


def _layer_norm_rows(z, g, b):
    mu = jnp.mean(z, axis=-1, keepdims=True)
    zc = z - mu
    var = jnp.mean(zc * zc, axis=-1, keepdims=True)
    return zc * lax.rsqrt(var + LN_EPS) * g + b


def _ada_kernel(ct_ref, w_ref, b_ref, o_ref, *, nb):
    ct = ct_ref[...]
    s = ct * jax.nn.sigmoid(ct)
    w = w_ref[0]
    rows = [jnp.sum(w * s[:, b:b + 1], axis=0, keepdims=True) for b in range(nb)]
    o_ref[0] = jnp.concatenate(rows, axis=0) + b_ref[0]


def _ada_mod(c, ada_w, ada_b):
    depth, d, n = ada_w.shape
    nb = c.shape[0]
    tn = ADA_TILE
    out = pl.pallas_call(
        functools.partial(_ada_kernel, nb=nb),
        grid=(depth, n // tn),
        in_specs=[
            pl.BlockSpec((d, nb), lambda l, j: (0, 0)),
            pl.BlockSpec((1, d, tn), lambda l, j: (l, 0, j)),
            pl.BlockSpec((1, 1, tn), lambda l, j: (l, 0, j)),
        ],
        out_specs=pl.BlockSpec((1, nb, tn), lambda l, j: (l, 0, j)),
        out_shape=jax.ShapeDtypeStruct((depth, nb, n), jnp.float32),
        compiler_params=_cparams("arbitrary", "arbitrary"),
        name="ada_mod",
    )(c.T, ada_w, ada_b.reshape(depth, 1, n))
    return out.reshape(depth, nb, 6, 1, d)


def _mod_spec(layer, k, d):
    return pl.BlockSpec((None, None, None, 1, d), lambda b, *_: (layer, b, k, 0, 0))


def _rel_bucket(dist):
    n = jnp.maximum(dist, 0)
    max_exact = REL_BUCKETS // 2
    nf = jnp.maximum(n, 1).astype(jnp.float32)
    large = max_exact + (jnp.log(nf / max_exact) / math.log(REL_MAX_DIST / max_exact)
                         * (REL_BUCKETS - max_exact)).astype(jnp.int32)
    large = jnp.minimum(large, REL_BUCKETS - 1)
    return jnp.where(n < max_exact, n, large)


def _bias_tile_kernel(tab_ref, bk_ref, o_ref, *, heads):
    bk = bk_ref[0]
    for h in range(heads):
        far = tab_ref[REL_BUCKETS - 1, h]
        acc = jnp.zeros(bk.shape, jnp.float32)
        for b in range(REL_BUCKETS - 1):
            acc = jnp.where(bk == b, (tab_ref[b, h] - far) * LOG2E, acc)
        o_ref[0, h] = jnp.where(bk < 0, NEG, acc)


def _bias_tiles(rel_bias, tq):
    heads = rel_bias.shape[1]
    r_sub = tq // LANE
    rels = jnp.arange(-2, r_sub + 1, dtype=jnp.int32)
    ql = jnp.arange(tq, dtype=jnp.int32)[None, :, None]
    sl = jnp.arange(LANE, dtype=jnp.int32)[None, None, :]
    dist = ql - rels[:, None, None] * LANE - sl
    bk = jnp.where(dist < 0, -1, _rel_bucket(dist))
    nk = r_sub + 3
    return pl.pallas_call(
        functools.partial(_bias_tile_kernel, heads=heads),
        grid=(nk,),
        in_specs=[
            pl.BlockSpec(memory_space=pltpu.SMEM),
            pl.BlockSpec((1, tq, LANE), lambda k: (k, 0, 0)),
        ],
        out_specs=pl.BlockSpec((1, heads, tq, LANE), lambda k: (k, 0, 0, 0)),
        out_shape=jax.ShapeDtypeStruct((nk, heads, tq, LANE), jnp.float32),
        compiler_params=_cparams("arbitrary"),
        name="bias_tiles",
    )(rel_bias, bk)


def _causal_pairs(s, tq, tk):
    ii, jj = [], []
    for i in range(s // tq):
        for j in range((i * tq + tq - 1) // tk + 1):
            ii.append(i)
            jj.append(j)
    return jnp.asarray(ii, jnp.int32), jnp.asarray(jj, jnp.int32)


def _proj_dsa_kernel(x_ref, sc_ref, sh_ref, wq_ref, wc_ref, wi_ref, wk_ref, kvg_ref,
                     q_ref, ckv_ref, ckvt_ref, iqt_ref, ik_ref, iwt_ref):
    h = (x_ref[0] * (1.0 + sc_ref[...]) + sh_ref[...]).astype(MXU_DTYPE)
    q_ref[0] = _dot(h, wq_ref[...]).astype(q_ref.dtype)
    ckv = _dot(h, wc_ref[...])
    ckv = ckv * lax.rsqrt(jnp.mean(ckv * ckv, axis=-1, keepdims=True) + LN_EPS) * kvg_ref[...]
    ckv_ref[0] = ckv.astype(ckv_ref.dtype)
    ckvt_ref[0] = ckv.T.astype(ckvt_ref.dtype)
    iqt_ref[0] = _dot(h, wi_ref[...]).T.astype(iqt_ref.dtype)
    ikw = _dot(h, wk_ref[...])
    ik_ref[0] = ikw[:, :IDX_DIM].astype(ik_ref.dtype)
    iwt_ref[0] = ikw.T[IDX_DIM:IDX_DIM + IDX_HEADS, :]


def _proj_dsa(x, mod, layer, w_in, kv_norm):
    b, s, d = x.shape
    tm = TOKEN_TILE
    nq = A_HEADS * A_QK_DIM
    ni = IDX_HEADS * IDX_DIM
    o1, o2, o3 = nq, nq + A_KV_RANK, nq + A_KV_RANK + ni
    wq = w_in[:, :o1].astype(MXU_DTYPE)
    wc = w_in[:, o1:o2].astype(MXU_DTYPE)
    wi = w_in[:, o2:o3].astype(MXU_DTYPE)
    wk = jnp.pad(w_in[:, o3:], ((0, 0), (0, LANE - (w_in.shape[1] - o3)))).astype(MXU_DTYPE)
    row = lambda bb, t: (bb, t, 0)
    col = lambda bb, t: (bb, 0, t)
    return pl.pallas_call(
        _proj_dsa_kernel,
        grid=(b, s // tm),
        in_specs=[
            pl.BlockSpec((1, tm, d), row),
            _mod_spec(layer, 1, d), _mod_spec(layer, 0, d),
            _const_spec((d, nq)), _const_spec((d, A_KV_RANK)), _const_spec((d, ni)),
            _const_spec((d, LANE)), _const_spec((1, A_KV_RANK)),
        ],
        out_specs=[
            pl.BlockSpec((1, tm, nq), row), pl.BlockSpec((1, tm, A_KV_RANK), row),
            pl.BlockSpec((1, A_KV_RANK, tm), col), pl.BlockSpec((1, ni, tm), col),
            pl.BlockSpec((1, tm, IDX_DIM), row), pl.BlockSpec((1, IDX_HEADS, tm), col),
        ],
        out_shape=[
            jax.ShapeDtypeStruct((b, s, nq), MXU_DTYPE),
            jax.ShapeDtypeStruct((b, s, A_KV_RANK), MXU_DTYPE),
            jax.ShapeDtypeStruct((b, A_KV_RANK, s), MXU_DTYPE),
            jax.ShapeDtypeStruct((b, ni, s), MXU_DTYPE),
            jax.ShapeDtypeStruct((b, s, IDX_DIM), MXU_DTYPE),
            jax.ShapeDtypeStruct((b, IDX_HEADS, s), jnp.float32),
        ],
        compiler_params=_cparams("parallel", "parallel"),
        name="proj_dsa",
    )(x, mod, mod, wq, wc, wi, wk, kv_norm.reshape(1, A_KV_RANK))


def _float_key(v):
    bits = pltpu.bitcast(v, jnp.int32)
    return bits ^ ((bits >> 31) & 0x7FFFFFFF)


def _key_float(k):
    return pltpu.bitcast(k ^ ((k >> 31) & 0x7FFFFFFF), jnp.float32)


def _indexer_kernel(ik_ref, iqt_ref, iwt_ref, mask_ref, keys_ref, p_ref, *, tq, ck, topk, s_len):
    i = pl.program_id(1)
    q0 = i * tq
    nkc = (q0 + tq + ck - 1) // ck
    w = iwt_ref[0] * (IDX_HEADS ** -0.5 * IDX_DIM ** -0.5)
    tpos = q0 + lax.broadcasted_iota(jnp.int32, (1, tq), 1)
    srow = lax.broadcasted_iota(jnp.int32, (ck, 1), 0)
    int_max = jnp.int32(2 ** 31 - 1)
    int_min = jnp.int32(-2 ** 31)

    def fold(x, op):
        slabs = [x[r * SUB:(r + 1) * SUB] for r in range(ck // SUB)]
        lanes = FOLD_CHAINS
        chains = slabs[:lanes]
        for r, slab in enumerate(slabs[lanes:]):
            chains[r % lanes] = op(chains[r % lanes], slab)
        slabs = chains
        while len(slabs) > 1:
            slabs = [op(a, b) for a, b in zip(slabs[0::2], slabs[1::2])]
        return slabs[0]

    def score_chunk(kc, carry):
        kmax, kmin = carry
        k0 = pl.multiple_of(kc * ck, ck)
        ik = ik_ref[0, pl.ds(k0, ck), :]
        sc = jnp.zeros((ck, tq), jnp.float32)
        for h in range(IDX_HEADS):
            lg = _dot(ik, iqt_ref[0, h * IDX_DIM:(h + 1) * IDX_DIM, :])
            sc = sc + jnp.maximum(lg, 0.0) * w[h:h + 1, :]
        causal = k0 + srow <= tpos
        key = _float_key(jnp.where(causal, sc, -jnp.inf))
        keys_ref[pl.ds(k0, ck), :] = key
        return (jnp.maximum(kmax, fold(key, jnp.maximum)),
                jnp.minimum(kmin, fold(jnp.where(causal, key, int_max), jnp.minimum)))

    kmax, kmin = lax.fori_loop(0, nkc, score_chunk, (jnp.full((SUB, tq), int_min, jnp.int32),
                                                      jnp.full((SUB, tq), int_max, jnp.int32)))
    kmax = jnp.max(kmax, axis=0, keepdims=True)
    kmin = jnp.min(kmin, axis=0, keepdims=True)

    def count(pred):
        def body(kc, accs):
            k0 = pl.multiple_of(kc * ck, ck)
            accs = list(accs)
            chunk = keys_ref.at[pl.ds(k0, ck), :]
            for r in range(ck // SUB):
                k = chunk[r * SUB:(r + 1) * SUB, :]
                hit = pred(k, k0 + srow[r * SUB:(r + 1) * SUB]).astype(jnp.float32)
                accs[r % FOLD_CHAINS] = accs[r % FOLD_CHAINS] + hit
            return tuple(accs)
        accs = lax.fori_loop(0, nkc, body, (jnp.zeros((SUB, tq), jnp.float32),) * FOLD_CHAINS)
        acc = functools.reduce(jnp.add, accs)
        return jnp.sum(acc, axis=0, keepdims=True).astype(jnp.int32)

    k_eff = jnp.minimum(topk, tpos + 1)
    log_keff = jnp.log(k_eff.astype(jnp.float32))

    def unsettled(lo, hi, cnt_lo):
        return (cnt_lo != k_eff) & (hi > lo + 1)

    def candidate(st, kind, bisect_only):
        lo, hi, cnt_lo, cnt_hi = st
        key_mid = (lo >> 1) + (hi >> 1) + (lo & hi & 1)
        if kind == "key_mid":
            cand = key_mid
        elif kind in ("zero", "above_zero"):
            cand = jnp.full((1, tq), 0 if kind == "zero" else 1, jnp.int32)
        else:
            vlo, vhi = _key_float(lo), _key_float(hi)
            if kind == "interpolate":
                a = jnp.log(cnt_lo.astype(jnp.float32) + 0.5)
                b = jnp.log(cnt_hi.astype(jnp.float32) + 0.5)
                cand = _float_key(vlo + (vhi - vlo) * ((a - log_keff) / (a - b)))
            else:
                cand = _float_key(0.5 * vlo + 0.5 * vhi)
            cand = jnp.where(bisect_only, key_mid, cand)
        return jnp.clip(cand, lo + 1, hi - 1)

    def narrow(st, cand, cnt, upd):
        lo, hi, cnt_lo, cnt_hi = st
        up = upd & (cnt >= k_eff)
        dn = upd & (cnt < k_eff)
        return (jnp.where(up, cand, lo), jnp.where(dn, cand, hi),
                jnp.where(up, cnt, cnt_lo), jnp.where(dn, cnt, cnt_hi))

    def search_pass(st, kind, bisect_only):
        cand = candidate(st, kind, bisect_only)
        cnt = count(lambda k, sp: k >= cand)
        return narrow(st, cand, cnt, unsettled(st[0], st[1], st[2]))

    def run_rounds(one_pass, active, st):
        def cond(c):
            return jnp.max(active(c[1:]).astype(jnp.int32)) > 0

        def body(c):
            bisect_only = c[0] >= SEARCH_INTERP_ROUNDS
            st = c[1:]
            for kind in SEARCH_ROUND:
                st = one_pass(st, kind, bisect_only)
            return (c[0] + 1,) + st
        return lax.while_loop(cond, body, (jnp.int32(0),) + st)[1:]

    st = (kmin, kmax + 1, tpos + 1, jnp.zeros((1, tq), jnp.int32))
    for kind in SEARCH_OPENING:
        st = search_pass(st, kind, False)

    tau, hi, cnt_ge, cnt_gt = run_rounds(search_pass, lambda st: unsettled(st[0], st[1], st[2]), st)
    need = k_eff - cnt_gt
    overflow = cnt_ge > k_eff
    p_ref[...] = jnp.full((1, tq), s_len, jnp.int32)

    any_overflow = jnp.max(overflow.astype(jnp.int32)) > 0

    @pl.when(any_overflow)
    def _():
        def pos_step(it, pos):
            cand = pos + jnp.left_shift(jnp.int32(1), (s_len.bit_length() - 1) - it)
            below = count(lambda k, sp: (k == tau) & (sp < cand))
            return jnp.where(below < need, cand, pos)
        pos = lax.fori_loop(0, s_len.bit_length(), pos_step, jnp.zeros((1, tq), jnp.int32))
        p_ref[...] = jnp.where(overflow, pos, s_len)

    last_eq = p_ref[...]

    def write_chunk(kc, carry):
        k0 = pl.multiple_of(kc * ck, ck)

        @pl.when((kc < nkc) & any_overflow)
        def _():
            k = keys_ref[pl.ds(k0, ck), :]
            sel = (k > tau) | ((k == tau) & (k0 + srow <= last_eq))
            mask_ref[0, :, pl.ds(k0, ck)] = jnp.where(sel, 0.0, NEG).T.astype(mask_ref.dtype)

        @pl.when((kc < nkc) & jnp.logical_not(any_overflow))
        def _():
            sel = keys_ref[pl.ds(k0, ck), :] >= tau
            mask_ref[0, :, pl.ds(k0, ck)] = jnp.where(sel, 0.0, NEG).T.astype(mask_ref.dtype)

        @pl.when(kc >= nkc)
        def _():
            mask_ref[0, :, pl.ds(k0, ck)] = jnp.full((tq, ck), NEG, mask_ref.dtype)
        return carry

    lax.fori_loop(0, s_len // ck, write_chunk, 0)


def _indexer_mask(ik, iqt, iwt, topk):
    b, s, di = ik.shape
    ni = iqt.shape[1]
    tq, ck = IDX_TQ, IDX_CK
    return pl.pallas_call(
        functools.partial(_indexer_kernel, tq=tq, ck=ck, topk=topk, s_len=s),
        grid=(b, s // tq),
        in_specs=[
            pl.BlockSpec((1, s, di), lambda bb, i: (bb, 0, 0)),
            pl.BlockSpec((1, ni, tq), lambda bb, i: (bb, 0, i)),
            pl.BlockSpec((1, IDX_HEADS, tq), lambda bb, i: (bb, 0, i)),
        ],
        out_specs=pl.BlockSpec((1, tq, s), lambda bb, i: (bb, i, 0)),
        out_shape=jax.ShapeDtypeStruct((b, s, s), jnp.bfloat16),
        scratch_shapes=[pltpu.VMEM((s, tq), jnp.int32), pltpu.VMEM((1, tq), jnp.int32)],
        compiler_params=_cparams("parallel", "parallel"),
        name="indexer_topk",
    )(ik, iqt, iwt)


def _flash_update(s_cols, m_ref, l_ref, rows):
    m_old = m_ref[rows, :]
    m_cur = functools.reduce(jnp.maximum, s_cols)
    m_new = jnp.maximum(m_old, jnp.max(m_cur, axis=-1, keepdims=True))
    alpha = jnp.exp2(m_old - m_new)
    ps = [jnp.exp2(sc - m_new) for sc in s_cols]
    l_ref[rows, :] = alpha * l_ref[rows, :] + functools.reduce(jnp.add, ps)
    m_ref[rows, :] = m_new
    return jnp.concatenate(ps, axis=1).astype(MXU_DTYPE), alpha


def _row_sums(l):
    ones = jnp.ones((LANE, LANE), MXU_DTYPE)
    hi = l.astype(MXU_DTYPE)
    lo = (l - hi.astype(jnp.float32)).astype(MXU_DTYPE)
    return _dot(hi, ones) + _dot(lo, ones)


def _dispatch_tile(step, i, j, tq, tk):
    assert tq * DIAG_PARTS == tk
    sub_q = tq // LANE
    causal_cols = (i + 1) * tq - j * tk
    for w in range(1, DIAG_PARTS + 1):
        first = (w - 1) * sub_q
        cols = {first + r: r + 2 for r in range(-1 if first else 0, sub_q)}
        pl.when(causal_cols == w * tq)(functools.partial(step, cols, w * tq))
    nsub = tk // LANE
    before_block_start = causal_cols == tk + tq
    pl.when(before_block_start)(functools.partial(step, {nsub - 1: 1}, tk))
    pl.when((causal_cols > tk) & jnp.logical_not(before_block_start))(
        functools.partial(step, {}, tk))


def _dsa_attn_kernel(itab, jtab, q_ref, wuk_ref, ckvt_ref, ckv_ref, mask_ref, nb_ref, wuv_ref,
                     o_ref, qlat_ref, m_ref, l_ref, acc_ref, *, tq, tk, hc):
    pidx = pl.program_id(1)
    i = itab[pidx]
    j = jtab[pidx]
    jlast = (i * tq + tq - 1) // tk
    heads = A_HEADS
    nchunk = heads // hc
    crows = hc * tq

    @pl.when(j == 0)
    def _():
        for h in range(heads):
            qh = q_ref[0, :, h * A_QK_DIM:(h + 1) * A_QK_DIM]
            ql = _dot(qh, wuk_ref[h]) * (A_QK_DIM ** -0.5 * LOG2E)
            qlat_ref[h * tq:(h + 1) * tq, :] = ql.astype(qlat_ref.dtype)
        m_ref[...] = jnp.full(m_ref.shape, M_INIT, jnp.float32)
        l_ref[...] = jnp.zeros(l_ref.shape, jnp.float32)
        acc_ref[...] = jnp.zeros(acc_ref.shape, jnp.float32)

    def step(bias_cols, width):
        nsub = width // LANE
        ckvt = ckvt_ref[0, :, :width]
        ckv = ckv_ref[0, :width, :]
        s_next = _dot(qlat_ref[0:crows, :], ckvt)
        for ci in range(nchunk):
            s_all = s_next
            if ci + 1 < nchunk:
                s_next = _dot(qlat_ref[(ci + 1) * crows:(ci + 2) * crows, :], ckvt)
            ps, alphas = [], []
            for hh in range(hc):
                h = ci * hc + hh
                cols = []
                for c in range(nsub):
                    add = mask_ref[0, :, c * LANE:(c + 1) * LANE].astype(jnp.float32)
                    if c in bias_cols:
                        add = add + nb_ref[bias_cols[c], h]
                    cols.append(s_all[hh * tq:(hh + 1) * tq, c * LANE:(c + 1) * LANE] + add)
                p, alpha = _flash_update(cols, m_ref, l_ref, slice(h * tq, (h + 1) * tq))
                ps.append(p)
                alphas.append(alpha)
            pv = _dot(jnp.concatenate(ps, axis=0), ckv)
            a = jnp.concatenate(alphas, axis=0)
            rows = slice(ci * crows, (ci + 1) * crows)
            for half in range(A_KV_RANK // LANE):
                cs = slice(half * LANE, (half + 1) * LANE)
                acc_ref[rows, cs] = a * acc_ref[rows, cs] + pv[:, cs]

    _dispatch_tile(step, i, j, tq, tk)

    @pl.when(j == jlast)
    def _():
        for h in range(heads):
            rows = slice(h * tq, (h + 1) * tq)
            inv_l = 1.0 / _row_sums(l_ref[rows, :])
            o_lat = jnp.concatenate([acc_ref[rows, c * LANE:(c + 1) * LANE] * inv_l
                                     for c in range(A_KV_RANK // LANE)], axis=1)
            oh = _dot(o_lat.astype(MXU_DTYPE), wuv_ref[h])
            o_ref[0, :, h * A_V_DIM:(h + 1) * A_V_DIM] = oh.astype(o_ref.dtype)


def _dsa_attention(q, ckv, ckvt, mask, nb, w_uk, w_uv):
    b, s, nq = q.shape
    tq, tk, hc = DSA_TQ, min(KEY_TILE, s), DSA_HEAD_CHUNK
    itab, jtab = _causal_pairs(s, tq, tk)
    rows = A_HEADS * tq
    once = pl.Buffered(1)
    grid_spec = pltpu.PrefetchScalarGridSpec(
        num_scalar_prefetch=2,
        grid=(b, int(itab.shape[0])),
        in_specs=[
            pl.BlockSpec((1, tq, nq), lambda bb, p, it, jt: (bb, it[p], 0)),
            pl.BlockSpec(w_uk.shape, lambda bb, p, it, jt: (0, 0, 0), pipeline_mode=once),
            pl.BlockSpec((1, A_KV_RANK, tk), lambda bb, p, it, jt: (bb, 0, jt[p])),
            pl.BlockSpec((1, tk, A_KV_RANK), lambda bb, p, it, jt: (bb, jt[p], 0)),
            pl.BlockSpec((1, tq, tk), lambda bb, p, it, jt: (bb, it[p], jt[p])),
            pl.BlockSpec(nb.shape, lambda bb, p, it, jt: (0, 0, 0, 0), pipeline_mode=once),
            pl.BlockSpec(w_uv.shape, lambda bb, p, it, jt: (0, 0, 0), pipeline_mode=once),
        ],
        out_specs=pl.BlockSpec((1, tq, A_HEADS * A_V_DIM), lambda bb, p, it, jt: (bb, it[p], 0)),
        scratch_shapes=[
            pltpu.VMEM((rows, A_KV_RANK), MXU_DTYPE),
            pltpu.VMEM((rows, LANE), jnp.float32),
            pltpu.VMEM((rows, LANE), jnp.float32),
            pltpu.VMEM((rows, A_KV_RANK), jnp.float32),
        ],
    )
    return pl.pallas_call(
        functools.partial(_dsa_attn_kernel, tq=tq, tk=tk, hc=hc),
        grid_spec=grid_spec,
        out_shape=jax.ShapeDtypeStruct((b, s, A_HEADS * A_V_DIM), MXU_DTYPE),
        compiler_params=_cparams("parallel", "arbitrary"),
        name="dsa_attention",
    )(itab, jtab, q, w_uk, ckvt, ckv, mask, nb, w_uv)


def _proj_diff_kernel(x_ref, sc_ref, sh_ref, wq_ref, wk_ref, wv_ref, q_ref, kt_ref, v_ref):
    h = (x_ref[0] * (1.0 + sc_ref[...]) + sh_ref[...]).astype(MXU_DTYPE)
    q = _dot(h, wq_ref[...]) * (B_HEAD_DIM ** -0.5 * LOG2E)
    kt = _dot(h, wk_ref[...]).T
    v = _dot(h, wv_ref[...])
    for m in range(B_MAPS):
        q_ref[0, m] = q[:, m * B_HEAD_DIM:(m + 1) * B_HEAD_DIM].astype(q_ref.dtype)
        kt_ref[0, m] = kt[m * B_HEAD_DIM:(m + 1) * B_HEAD_DIM, :].astype(kt_ref.dtype)
    dv = 2 * B_HEAD_DIM
    for hd in range(B_HEADS):
        v_ref[0, hd] = v[:, hd * dv:(hd + 1) * dv].astype(v_ref.dtype)


def _proj_diff(x, mod, layer, w_in):
    b, s, d = x.shape
    tm = TOKEN_TILE
    n = w_in.shape[1] // 3
    dv = 2 * B_HEAD_DIM
    ws = [w_in[:, k * n:(k + 1) * n].astype(MXU_DTYPE) for k in range(3)]
    return pl.pallas_call(
        _proj_diff_kernel,
        grid=(b, s // tm),
        in_specs=[pl.BlockSpec((1, tm, d), lambda bb, t: (bb, t, 0)),
                  _mod_spec(layer, 1, d), _mod_spec(layer, 0, d)] + [_const_spec((d, n))] * 3,
        out_specs=[
            pl.BlockSpec((1, B_MAPS, tm, B_HEAD_DIM), lambda bb, t: (bb, 0, t, 0)),
            pl.BlockSpec((1, B_MAPS, B_HEAD_DIM, tm), lambda bb, t: (bb, 0, 0, t)),
            pl.BlockSpec((1, B_HEADS, tm, dv), lambda bb, t: (bb, 0, t, 0)),
        ],
        out_shape=[
            jax.ShapeDtypeStruct((b, B_MAPS, s, B_HEAD_DIM), MXU_DTYPE),
            jax.ShapeDtypeStruct((b, B_MAPS, B_HEAD_DIM, s), MXU_DTYPE),
            jax.ShapeDtypeStruct((b, B_HEADS, s, dv), MXU_DTYPE),
        ],
        compiler_params=_cparams("parallel", "parallel"),
        name="proj_diff",
    )(x, mod, mod, *ws)


def _diff_attn_kernel(itab, jtab, q_ref, kt_ref, v_ref, nb_ref, lam_ref, g_ref, o_ref,
                      m_ref, l_ref, acc_ref, *, tq, tk, lam_init):
    pidx = pl.program_id(1)
    i = itab[pidx]
    j = jtab[pidx]
    jlast = (i * tq + tq - 1) // tk
    dv = 2 * B_HEAD_DIM

    @pl.when(j == 0)
    def _():
        m_ref[...] = jnp.full(m_ref.shape, M_INIT, jnp.float32)
        l_ref[...] = jnp.zeros(l_ref.shape, jnp.float32)
        acc_ref[...] = jnp.zeros(acc_ref.shape, jnp.float32)

    def step(bias_cols, width):
        nsub = width // LANE

        def logits(h):
            return [_dot(q_ref[0, 2 * h + mm], kt_ref[0, 2 * h + mm, :, :width]) for mm in range(2)]

        s_next = logits(0)
        for h in range(B_HEADS):
            s_pair = s_next
            if h + 1 < B_HEADS:
                s_next = logits(h + 1)
            ps, alphas = [], []
            for mm in range(2):
                col = 2 * h + mm
                cols = [s_pair[mm][:, c * LANE:(c + 1) * LANE] for c in range(nsub)]
                cols = [cols[c] + nb_ref[bias_cols[c], col] if c in bias_cols else cols[c]
                        for c in range(nsub)]
                p, alpha = _flash_update(cols, m_ref, l_ref, slice(col * tq, (col + 1) * tq))
                ps.append(p)
                alphas.append(alpha)
            pv = _dot(jnp.concatenate(ps, axis=0), v_ref[0, h, :width, :])
            rows = slice(2 * h * tq, (2 * h + 2) * tq)
            acc_ref[rows, :] = jnp.concatenate(alphas, axis=0) * acc_ref[rows, :] + pv

    _dispatch_tile(step, i, j, tq, tk)

    @pl.when(j == jlast)
    def _():
        lam = lam_ref[...]
        lam_full = (jnp.exp(jnp.sum(lam[0:1] * lam[1:2], axis=-1, keepdims=True))
                    - jnp.exp(jnp.sum(lam[2:3] * lam[3:4], axis=-1, keepdims=True)) + lam_init)
        for h in range(B_HEADS):
            r1 = slice(2 * h * tq, (2 * h + 1) * tq)
            r2 = slice((2 * h + 1) * tq, (2 * h + 2) * tq)
            o = (acc_ref[r1, :] / _row_sums(l_ref[r1, :])
                 - lam_full * (acc_ref[r2, :] / _row_sums(l_ref[r2, :])))
            o = o * lax.rsqrt(jnp.mean(o * o, axis=-1, keepdims=True) + LN_EPS) * g_ref[...]
            o_ref[0, :, h * dv:(h + 1) * dv] = (o * (1.0 - lam_init)).astype(o_ref.dtype)


def _diff_attention(qm, ktm, vh, nb, lam, subln_g, lam_init):
    b, maps, s, dh = qm.shape
    tq, tk = DIFF_TQ, min(KEY_TILE, s)
    dv = 2 * B_HEAD_DIM
    assert dv == LANE
    itab, jtab = _causal_pairs(s, tq, tk)
    once = pl.Buffered(1)
    grid_spec = pltpu.PrefetchScalarGridSpec(
        num_scalar_prefetch=2,
        grid=(b, int(itab.shape[0])),
        in_specs=[
            pl.BlockSpec((1, maps, tq, dh), lambda bb, p, it, jt: (bb, 0, it[p], 0)),
            pl.BlockSpec((1, maps, dh, tk), lambda bb, p, it, jt: (bb, 0, 0, jt[p])),
            pl.BlockSpec((1, B_HEADS, tk, dv), lambda bb, p, it, jt: (bb, 0, jt[p], 0)),
            pl.BlockSpec(nb.shape, lambda bb, p, it, jt: (0, 0, 0, 0), pipeline_mode=once),
            pl.BlockSpec(lam.shape, lambda bb, p, it, jt: (0, 0)),
            pl.BlockSpec((1, dv), lambda bb, p, it, jt: (0, 0)),
        ],
        out_specs=pl.BlockSpec((1, tq, B_HEADS * dv), lambda bb, p, it, jt: (bb, it[p], 0)),
        scratch_shapes=[
            pltpu.VMEM((maps * tq, LANE), jnp.float32),
            pltpu.VMEM((maps * tq, LANE), jnp.float32),
            pltpu.VMEM((maps * tq, dv), jnp.float32),
        ],
    )
    return pl.pallas_call(
        functools.partial(_diff_attn_kernel, tq=tq, tk=tk, lam_init=lam_init),
        grid_spec=grid_spec,
        out_shape=jax.ShapeDtypeStruct((b, s, B_HEADS * dv), MXU_DTYPE),
        compiler_params=_cparams("parallel", "arbitrary"),
        name="diff_attention",
    )(itab, jtab, qm, ktm, vh, nb, lam, subln_g.reshape(1, dv))


def _post_mixer_kernel(o_ref, x_ref, gt_ref, sc_ref, sh_ref, gc_ref, wo_ref, w1_ref, w2_ref,
                       lng_ref, lnb_ref, out_ref, *, tf, alpha):
    y = _dot(o_ref[0], wo_ref[...])
    x = _layer_norm_rows(alpha * x_ref[0] + (1.0 + gt_ref[...]) * y, lng_ref[0:1], lnb_ref[0:1])
    h = (x * (1.0 + sc_ref[...]) + sh_ref[...]).astype(MXU_DTYPE)
    y = jnp.zeros(x.shape, jnp.float32)
    for c in range(w1_ref.shape[1] // tf):
        a = jnp.maximum(_dot(h, w1_ref[:, c * tf:(c + 1) * tf]), 0.0)
        y = y + _dot((a * a).astype(MXU_DTYPE), w2_ref[c * tf:(c + 1) * tf, :])
    z = alpha * x + (1.0 + gc_ref[...]) * y
    out_ref[0] = _layer_norm_rows(z, lng_ref[1:2], lnb_ref[1:2])


def _post_mixer(o, x, mod, layer, w_o, w1, w2, ln_g, ln_b, alpha):
    b, s, d = x.shape
    dff = w1.shape[1]
    n_in = o.shape[-1]
    tm, tf = TOKEN_TILE, FF_TILE
    row = lambda bb, t: (bb, t, 0)
    once = pl.Buffered(1)
    return pl.pallas_call(
        functools.partial(_post_mixer_kernel, tf=tf, alpha=alpha),
        grid=(b, s // tm),
        in_specs=[
            pl.BlockSpec((1, tm, n_in), row), pl.BlockSpec((1, tm, d), row),
            _mod_spec(layer, 2, d), _mod_spec(layer, 4, d), _mod_spec(layer, 3, d),
            _mod_spec(layer, 5, d),
            pl.BlockSpec((n_in, d), lambda bb, t: (0, 0), pipeline_mode=once),
            pl.BlockSpec((d, dff), lambda bb, t: (0, 0), pipeline_mode=once),
            pl.BlockSpec((dff, d), lambda bb, t: (0, 0), pipeline_mode=once),
            _const_spec((2, d)), _const_spec((2, d)),
        ],
        out_specs=pl.BlockSpec((1, tm, d), row),
        out_shape=jax.ShapeDtypeStruct((b, s, d), jnp.float32),
        compiler_params=_cparams("parallel", "parallel"),
        name="post_mixer",
    )(o, x, mod, mod, mod, mod, w_o.astype(MXU_DTYPE), w1.astype(MXU_DTYPE),
      w2.astype(MXU_DTYPE), ln_g, ln_b)


def kernel(x, c, rel_bias, ada_w, ada_b, ln_g, ln_b, a_w_in, a_kv_norm, a_w_uk, a_w_uv, a_w_o,
           b_w_in, b_lambda, b_subln, b_w_o, mlp_w1, mlp_w2):
    depth = ada_w.shape[0]
    b, s, d = x.shape
    alpha = (2 * depth) ** 0.25
    topk = min(TOPK_MAX, s // 4)
    mod = _ada_mod(c, ada_w, ada_b)
    nb = {tq: _bias_tiles(rel_bias, tq) for tq in {DSA_TQ, DIFF_TQ}}
    nb_a, nb_b = nb[DSA_TQ], nb[DIFF_TQ]
    for i in range(depth):
        jm = i // N_MIXERS
        if i % N_MIXERS == 0:
            q, ckv, ckvt, iqt, ik, iwt = _proj_dsa(x, mod, i, a_w_in[jm], a_kv_norm[jm])
            mask = _indexer_mask(ik, iqt, iwt, topk)
            o = _dsa_attention(q, ckv, ckvt, mask, nb_a,
                               a_w_uk[jm].astype(MXU_DTYPE), a_w_uv[jm].astype(MXU_DTYPE))
            w_o = a_w_o[jm]
        else:
            lam_init = 0.8 - 0.6 * math.exp(-0.3 * i)
            qm, ktm, vh = _proj_diff(x, mod, i, b_w_in[jm])
            o = _diff_attention(qm, ktm, vh, nb_b, b_lambda[jm], b_subln[jm], lam_init)
            w_o = b_w_o[jm]
        x = _post_mixer(o, x, mod, i, w_o, mlp_w1[i], mlp_w2[i], ln_g[i], ln_b[i], alpha)
    return x
```
